```python
import jax, jax.numpy as jnp
from jax import lax
import numpy as np

D_MODEL = 1024
BATCH = 8
SEQ = 16384
DEPTH = 4

N_MIXERS = 3
N_A = len(range(0, DEPTH, N_MIXERS))
N_B = len(range(1, DEPTH, N_MIXERS))
N_C = len(range(2, DEPTH, N_MIXERS))
N_SUBLAYERS = 3
N_MOD = 3
CONV_WIDTH = 31
N_HEADS = 16
HEAD_DIM = D_MODEL // N_HEADS
Q_BLOCK = 128
POOL_WINDOWS = (2, 4, 8, 16)
POOL_GROUPS = len(POOL_WINDOWS)
POOL_GROUP_DIM = D_MODEL // POOL_GROUPS
D_FF = ((8 * D_MODEL // 3 + 127) // 128) * 128
EPS = 1e-6

kernel_name = "hybrid_conv_stickbreak_pool_macaron"


def _bc(v, ndim):
    return v.reshape((1,) * (ndim - v.ndim) + v.shape)


def rms_norm(x, g):
    xf = x.astype(jnp.float32)
    y = xf * lax.rsqrt(jnp.mean(xf * xf, axis=-1, keepdims=True) + EPS)
    return (y * _bc(g.astype(jnp.float32), x.ndim)).astype(x.dtype)


def layer_norm(x, g, b):
    xf = x.astype(jnp.float32)
    mu = jnp.mean(xf, axis=-1, keepdims=True)
    var = jnp.mean(jnp.square(xf - mu), axis=-1, keepdims=True)
    y = (xf - mu) * lax.rsqrt(var + EPS)
    y = y * _bc(g.astype(jnp.float32), x.ndim) + _bc(b.astype(jnp.float32), x.ndim)
    return y.astype(x.dtype)


def modulate(x, g, shift, scale):
    return rms_norm(x, g) * (1 + scale[:, None, :]) + shift[:, None, :]


def swiglu(h, w1, w3, w2):
    return (jax.nn.silu(h @ w1) * (h @ w3)) @ w2


def conformer_conv(h, w_in, b_in, dw, dw_b, ln_g, ln_b, w_out, b_out):
    u = jax.nn.glu(h @ w_in + _bc(b_in, 3), axis=-1)
    v = lax.conv_general_dilated(
        u, dw[:, None, :].astype(u.dtype), window_strides=(1,),
        padding=[(CONV_WIDTH - 1, 0)],
        dimension_numbers=("NWC", "WIO", "NWC"),
        feature_group_count=D_MODEL) + _bc(dw_b, 3)
    v = jax.nn.silu(layer_norm(v, ln_g, ln_b))
    return v @ w_out + _bc(b_out, 3)


def stick_breaking_attention(h, w_qkv, q_g, k_g, w_o):
    B, S, _ = h.shape
    qkv = h @ w_qkv
    q, k, v = jnp.split(qkv, 3, axis=-1)
    to_heads = lambda t: t.reshape(B, S, N_HEADS, HEAD_DIM).transpose(0, 2, 1, 3)
    q = rms_norm(to_heads(q), q_g)
    k = rms_norm(to_heads(k), k_g)
    qf = q.astype(jnp.float32) * (HEAD_DIM ** -0.5)
    kf = k.astype(jnp.float32)
    vf = to_heads(v).astype(jnp.float32)
    nb = S // Q_BLOCK
    q_blocks = qf.reshape(B, N_HEADS, nb, Q_BLOCK, HEAD_DIM).transpose(2, 0, 1, 3, 4)
    key_pos = jnp.arange(S)

    def block(args):
        qb, bi = args
        z = jnp.einsum('bhqd,bhkd->bhqk', qb, kf)
        t = bi * Q_BLOCK + jnp.arange(Q_BLOCK)
        mask = (key_pos[None, :] < t[:, None])[None, None]
        neg_log_keep = jnp.where(mask, jax.nn.softplus(z), 0.0)
        rcs = lax.cumsum(neg_log_keep, axis=3, reverse=True)
        suffix = jnp.concatenate([rcs[..., 1:], jnp.zeros_like(rcs[..., :1])], axis=-1)
        a = jnp.where(mask, jnp.exp(jax.nn.log_sigmoid(z) - suffix), 0.0)
        return jnp.einsum('bhqk,bhkd->bhqd', a, vf)

    o = lax.map(block, (q_blocks, jnp.arange(nb)))
    o = o.transpose(1, 0, 3, 2, 4).reshape(B, S, D_MODEL).astype(h.dtype)
    return o @ w_o


def multiscale_pool(h, p_w, p_b, p_scale):
    B, S, _ = h.shape
    hf = h.astype(jnp.float32)
    cs0 = jnp.concatenate([jnp.zeros((B, 1, D_MODEL), jnp.float32), lax.cumsum(hf, axis=1)], axis=1)
    t = jnp.arange(S)
    diffs = []
    for g, w in enumerate(POOL_WINDOWS):
        sl = slice(g * POOL_GROUP_DIM, (g + 1) * POOL_GROUP_DIM)
        c_g = cs0[:, :, sl]
        lag = jnp.concatenate([jnp.zeros((B, w - 1, POOL_GROUP_DIM), jnp.float32), c_g[:, :S + 1 - w]], axis=1)
        cnt = jnp.minimum(t + 1, w).astype(jnp.float32)[None, :, None]
        diffs.append((c_g[:, 1:] - lag) / cnt - hf[:, :, sl])
    d = jnp.stack(diffs, axis=2)
    y = jnp.einsum('bsgc,gcd->bsgd', d, p_w.astype(jnp.float32)) + _bc(p_b.astype(jnp.float32), 4)
    return (y.reshape(B, S, D_MODEL) * _bc(p_scale.astype(jnp.float32), 3)).astype(h.dtype)


def _fwd_setup_inputs(seed: int = 0) -> dict:
    key = jax.random.key(seed)
    ks = jax.random.split(key, 26)
    n = lambda k, shape, s: jax.random.normal(k, shape, jnp.float32) * s
    D, F, G = D_MODEL, D_FF, POOL_GROUP_DIM
    return {
        "x": n(ks[0], (BATCH, SEQ, D), 1.0),
        "c": n(ks[1], (BATCH, D), 1.0),
        "cond_w": n(ks[2], (D, D), D ** -0.5),
        "cond_b": n(ks[3], (D,), 0.02),
        "ada_w": n(ks[4], (DEPTH, D, N_SUBLAYERS * N_MOD * D), 0.1 * D ** -0.5),
        "ada_b": n(ks[5], (DEPTH, N_SUBLAYERS * N_MOD * D), 0.02),
        "norm_g": 1.0 + n(ks[6], (DEPTH, N_SUBLAYERS, D), 0.02),
        "ffn_w1": n(ks[7], (DEPTH, 2, D, F), D ** -0.5),
        "ffn_w3": n(ks[8], (DEPTH, 2, D, F), D ** -0.5),
        "ffn_w2": n(ks[9], (DEPTH, 2, F, D), F ** -0.5),
        "a_w_in": n(ks[10], (N_A, D, 2 * D), D ** -0.5),
        "a_b_in": n(ks[11], (N_A, 2 * D), 0.02),
        "a_dw": n(ks[12], (N_A, CONV_WIDTH, D), CONV_WIDTH ** -0.5),
        "a_dw_b": n(ks[13], (N_A, D), 0.02),
        "a_ln_g": 1.0 + n(ks[14], (N_A, D), 0.02),
        "a_ln_b": n(ks[15], (N_A, D), 0.02),
        "a_w_out": n(ks[16], (N_A, D, D), D ** -0.5),
        "a_b_out": n(ks[17], (N_A, D), 0.02),
        "b_w_qkv": n(ks[18], (N_B, D, 3 * D), D ** -0.5),
        "b_q_g": 1.0 + n(ks[19], (N_B, HEAD_DIM), 0.02),
        "b_k_g": 1.0 + n(ks[20], (N_B, HEAD_DIM), 0.02),
        "b_w_o": n(ks[21], (N_B, D, D), D ** -0.5),
        "p_w": n(ks[22], (N_C, POOL_GROUPS, G, G), G ** -0.5),
        "p_b": n(ks[23], (N_C, POOL_GROUPS, G), 0.02),
        "p_scale": 1.0 + n(ks[24], (N_C, D), 0.1),
    }


def _fwd_reference(x, c, cond_w, cond_b, ada_w, ada_b, norm_g, ffn_w1, ffn_w3, ffn_w2,
              a_w_in, a_b_in, a_dw, a_dw_b, a_ln_g, a_ln_b, a_w_out, a_b_out,
              b_w_qkv, b_q_g, b_k_g, b_w_o, p_w, p_b, p_scale):
    B = x.shape[0]
    e = jax.nn.silu(c @ cond_w + cond_b[None, :])
    ia = ib = ic = 0
    for i in range(DEPTH):
        mod = (e @ ada_w[i] + ada_b[i][None, :]).reshape(B, N_SUBLAYERS, N_MOD, D_MODEL)
        shift, scale, gate = mod[:, :, 0], mod[:, :, 1], 1 + mod[:, :, 2]
        h = modulate(x, norm_g[i, 0], shift[:, 0], scale[:, 0])
        x = x + 0.5 * gate[:, 0][:, None, :] * swiglu(h, ffn_w1[i, 0], ffn_w3[i, 0], ffn_w2[i, 0])
        h = modulate(x, norm_g[i, 1], shift[:, 1], scale[:, 1])
        kind = i % N_MIXERS
        if kind == 0:
            y = conformer_conv(h, a_w_in[ia], a_b_in[ia], a_dw[ia], a_dw_b[ia],
                               a_ln_g[ia], a_ln_b[ia], a_w_out[ia], a_b_out[ia])
            ia += 1
        elif kind == 1:
            y = stick_breaking_attention(h, b_w_qkv[ib], b_q_g[ib], b_k_g[ib], b_w_o[ib])
            ib += 1
        else:
            y = multiscale_pool(h, p_w[ic], p_b[ic], p_scale[ic])
            ic += 1
        x = x + gate[:, 1][:, None, :] * y
        h = modulate(x, norm_g[i, 2], shift[:, 2], scale[:, 2])
        x = x + 0.5 * gate[:, 2][:, None, :] * swiglu(h, ffn_w1[i, 1], ffn_w3[i, 1], ffn_w2[i, 1])
    return x


import jax as _jax
import jax.numpy as _jnp

TWIN_FORMAT = 'train_step'
FWD_PARAMS = ['x', 'c', 'cond_w', 'cond_b', 'ada_w', 'ada_b', 'norm_g', 'ffn_w1', 'ffn_w3', 'ffn_w2', 'a_w_in', 'a_b_in', 'a_dw', 'a_dw_b', 'a_ln_g', 'a_ln_b', 'a_w_out', 'a_b_out', 'b_w_qkv', 'b_q_g', 'b_k_g', 'b_w_o', 'p_w', 'p_b', 'p_scale']
TWIN_WEIGHTS = ['cond_w', 'cond_b', 'ada_w', 'ada_b', 'norm_g', 'ffn_w1', 'ffn_w3', 'ffn_w2', 'a_w_in', 'a_b_in', 'a_dw', 'a_dw_b', 'a_ln_g', 'a_ln_b', 'a_w_out', 'a_b_out', 'b_w_qkv', 'b_q_g', 'b_k_g', 'b_w_o', 'p_w', 'p_b', 'p_scale']
TWIN_DIFF_INPUT = 'x'
TWIN_INPUTS = ['x', 'c', 'cond_w', 'cond_b', 'ada_w', 'ada_b', 'norm_g', 'ffn_w1', 'ffn_w3', 'ffn_w2', 'a_w_in', 'a_b_in', 'a_dw', 'a_dw_b', 'a_ln_g', 'a_ln_b', 'a_w_out', 'a_b_out', 'b_w_qkv', 'b_q_g', 'b_k_g', 'b_w_o', 'p_w', 'p_b', 'p_scale', 'loss_target', 'm_cond_w', 'm_cond_b', 'm_ada_w', 'm_ada_b', 'm_norm_g', 'm_ffn_w1', 'm_ffn_w3', 'm_ffn_w2', 'm_a_w_in', 'm_a_b_in', 'm_a_dw', 'm_a_dw_b', 'm_a_ln_g', 'm_a_ln_b', 'm_a_w_out', 'm_a_b_out', 'm_b_w_qkv', 'm_b_q_g', 'm_b_k_g', 'm_b_w_o', 'm_p_w', 'm_p_b', 'm_p_scale', 'v_cond_w', 'v_cond_b', 'v_ada_w', 'v_ada_b', 'v_norm_g', 'v_ffn_w1', 'v_ffn_w3', 'v_ffn_w2', 'v_a_w_in', 'v_a_b_in', 'v_a_dw', 'v_a_dw_b', 'v_a_ln_g', 'v_a_ln_b', 'v_a_w_out', 'v_a_b_out', 'v_b_w_qkv', 'v_b_q_g', 'v_b_k_g', 'v_b_w_o', 'v_p_w', 'v_p_b', 'v_p_scale']
TWIN_OUTPUTS = ['loss', 'grad_x', 'grad_cond_w', 'grad_cond_b', 'grad_ada_w', 'grad_ada_b', 'grad_norm_g', 'grad_ffn_w1', 'grad_ffn_w3', 'grad_ffn_w2', 'grad_a_w_in', 'grad_a_b_in', 'grad_a_dw', 'grad_a_dw_b', 'grad_a_ln_g', 'grad_a_ln_b', 'grad_a_w_out', 'grad_a_b_out', 'grad_b_w_qkv', 'grad_b_q_g', 'grad_b_k_g', 'grad_b_w_o', 'grad_p_w', 'grad_p_b', 'grad_p_scale', 'delta_cond_w', 'delta_cond_b', 'delta_ada_w', 'delta_ada_b', 'delta_norm_g', 'delta_ffn_w1', 'delta_ffn_w3', 'delta_ffn_w2', 'delta_a_w_in', 'delta_a_b_in', 'delta_a_dw', 'delta_a_dw_b', 'delta_a_ln_g', 'delta_a_ln_b', 'delta_a_w_out', 'delta_a_b_out', 'delta_b_w_qkv', 'delta_b_q_g', 'delta_b_k_g', 'delta_b_w_o', 'delta_p_w', 'delta_p_b', 'delta_p_scale', 'new_m_cond_w', 'new_m_cond_b', 'new_m_ada_w', 'new_m_ada_b', 'new_m_norm_g', 'new_m_ffn_w1', 'new_m_ffn_w3', 'new_m_ffn_w2', 'new_m_a_w_in', 'new_m_a_b_in', 'new_m_a_dw', 'new_m_a_dw_b', 'new_m_a_ln_g', 'new_m_a_ln_b', 'new_m_a_w_out', 'new_m_a_b_out', 'new_m_b_w_qkv', 'new_m_b_q_g', 'new_m_b_k_g', 'new_m_b_w_o', 'new_m_p_w', 'new_m_p_b', 'new_m_p_scale', 'new_v_cond_w', 'new_v_cond_b', 'new_v_ada_w', 'new_v_ada_b', 'new_v_norm_g', 'new_v_ffn_w1', 'new_v_ffn_w3', 'new_v_ffn_w2', 'new_v_a_w_in', 'new_v_a_b_in', 'new_v_a_dw', 'new_v_a_dw_b', 'new_v_a_ln_g', 'new_v_a_ln_b', 'new_v_a_w_out', 'new_v_a_b_out', 'new_v_b_w_qkv', 'new_v_b_q_g', 'new_v_b_k_g', 'new_v_b_w_o', 'new_v_p_w', 'new_v_p_b', 'new_v_p_scale']
TWIN_LEAF_KINDS = {'loss': 'loss', 'grad_x': 'grad_x', 'grad_cond_w': 'grad_w', 'grad_cond_b': 'grad_w', 'grad_ada_w': 'grad_w', 'grad_ada_b': 'grad_w', 'grad_norm_g': 'grad_w', 'grad_ffn_w1': 'grad_w', 'grad_ffn_w3': 'grad_w', 'grad_ffn_w2': 'grad_w', 'grad_a_w_in': 'grad_w', 'grad_a_b_in': 'grad_w', 'grad_a_dw': 'grad_w', 'grad_a_dw_b': 'grad_w', 'grad_a_ln_g': 'grad_w', 'grad_a_ln_b': 'grad_w', 'grad_a_w_out': 'grad_w', 'grad_a_b_out': 'grad_w', 'grad_b_w_qkv': 'grad_w', 'grad_b_q_g': 'grad_w', 'grad_b_k_g': 'grad_w', 'grad_b_w_o': 'grad_w', 'grad_p_w': 'grad_w', 'grad_p_b': 'grad_w', 'grad_p_scale': 'grad_w', 'delta_cond_w': 'delta_w', 'delta_cond_b': 'delta_w', 'delta_ada_w': 'delta_w', 'delta_ada_b': 'delta_w', 'delta_norm_g': 'delta_w', 'delta_ffn_w1': 'delta_w', 'delta_ffn_w3': 'delta_w', 'delta_ffn_w2': 'delta_w', 'delta_a_w_in': 'delta_w', 'delta_a_b_in': 'delta_w', 'delta_a_dw': 'delta_w', 'delta_a_dw_b': 'delta_w', 'delta_a_ln_g': 'delta_w', 'delta_a_ln_b': 'delta_w', 'delta_a_w_out': 'delta_w', 'delta_a_b_out': 'delta_w', 'delta_b_w_qkv': 'delta_w', 'delta_b_q_g': 'delta_w', 'delta_b_k_g': 'delta_w', 'delta_b_w_o': 'delta_w', 'delta_p_w': 'delta_w', 'delta_p_b': 'delta_w', 'delta_p_scale': 'delta_w', 'new_m_cond_w': 'new_m', 'new_m_cond_b': 'new_m', 'new_m_ada_w': 'new_m', 'new_m_ada_b': 'new_m', 'new_m_norm_g': 'new_m', 'new_m_ffn_w1': 'new_m', 'new_m_ffn_w3': 'new_m', 'new_m_ffn_w2': 'new_m', 'new_m_a_w_in': 'new_m', 'new_m_a_b_in': 'new_m', 'new_m_a_dw': 'new_m', 'new_m_a_dw_b': 'new_m', 'new_m_a_ln_g': 'new_m', 'new_m_a_ln_b': 'new_m', 'new_m_a_w_out': 'new_m', 'new_m_a_b_out': 'new_m', 'new_m_b_w_qkv': 'new_m', 'new_m_b_q_g': 'new_m', 'new_m_b_k_g': 'new_m', 'new_m_b_w_o': 'new_m', 'new_m_p_w': 'new_m', 'new_m_p_b': 'new_m', 'new_m_p_scale': 'new_m', 'new_v_cond_w': 'new_v', 'new_v_cond_b': 'new_v', 'new_v_ada_w': 'new_v', 'new_v_ada_b': 'new_v', 'new_v_norm_g': 'new_v', 'new_v_ffn_w1': 'new_v', 'new_v_ffn_w3': 'new_v', 'new_v_ffn_w2': 'new_v', 'new_v_a_w_in': 'new_v', 'new_v_a_b_in': 'new_v', 'new_v_a_dw': 'new_v', 'new_v_a_dw_b': 'new_v', 'new_v_a_ln_g': 'new_v', 'new_v_a_ln_b': 'new_v', 'new_v_a_w_out': 'new_v', 'new_v_a_b_out': 'new_v', 'new_v_b_w_qkv': 'new_v', 'new_v_b_q_g': 'new_v', 'new_v_b_k_g': 'new_v', 'new_v_b_w_o': 'new_v', 'new_v_p_w': 'new_v', 'new_v_p_b': 'new_v', 'new_v_p_scale': 'new_v'}


def _forward(args):
    return _fwd_reference(*[args[k] for k in FWD_PARAMS])


def _output_shape():
    def fwd():
        inp = _fwd_setup_inputs(0)
        return _fwd_reference(*[inp[k] for k in FWD_PARAMS])
    out = _jax.eval_shape(fwd)
    return out.shape, out.dtype

N_MICROBATCH = 1
ADAM_LR = 0.001
ADAM_B1 = 0.9
ADAM_B2 = 0.999
ADAM_EPS = 1e-08
ADAM_WD = 0.01
ADAM_STEP = 10
PER_EXAMPLE_BATCH_AXIS = {'x': 0, 'c': 0, 'loss_target': 0}
SHARED_INPUTS = []
_WEIGHT_DTYPES = {'cond_w': _jnp.float32, 'cond_b': _jnp.float32, 'ada_w': _jnp.float32, 'ada_b': _jnp.float32, 'norm_g': _jnp.float32, 'ffn_w1': _jnp.float32, 'ffn_w3': _jnp.float32, 'ffn_w2': _jnp.float32, 'a_w_in': _jnp.float32, 'a_b_in': _jnp.float32, 'a_dw': _jnp.float32, 'a_dw_b': _jnp.float32, 'a_ln_g': _jnp.float32, 'a_ln_b': _jnp.float32, 'a_w_out': _jnp.float32, 'a_b_out': _jnp.float32, 'b_w_qkv': _jnp.float32, 'b_q_g': _jnp.float32, 'b_k_g': _jnp.float32, 'b_w_o': _jnp.float32, 'p_w': _jnp.float32, 'p_b': _jnp.float32, 'p_scale': _jnp.float32}
MOMENT_SCALE = {'cond_w': 4.774381e+00, 'cond_b': 1.119608e+01, 'ada_w': 9.229254e+00, 'ada_b': 3.344718e+01, 'norm_g': 3.813088e+01, 'ffn_w1': 4.459963e-01, 'ffn_w3': 5.115133e-01, 'ffn_w2': 8.320865e-01, 'a_w_in': 1.411953e+00, 'a_b_in': 2.046201e+01, 'a_dw': 3.158207e+00, 'a_dw_b': 4.522936e+01, 'a_ln_g': 5.967608e+01, 'a_ln_b': 4.384799e+01, 'a_w_out': 1.052942e+01, 'a_b_out': 5.314085e+01, 'b_w_qkv': 3.751521e+00, 'b_q_g': 1.271386e+02, 'b_k_g': 1.270498e+02, 'b_w_o': 6.393478e+00, 'p_w': 9.548497e+00, 'p_b': 4.895680e+01, 'p_scale': 9.680847e+01}


def _to_microbatches(a, axis):
    t = _jnp.moveaxis(a, axis, 0)
    t = t.reshape((N_MICROBATCH, t.shape[0] // N_MICROBATCH) + t.shape[1:])
    return _jnp.moveaxis(t, 1, axis + 1)


def setup_inputs(seed: int = 0) -> dict:
    inp = _fwd_setup_inputs(seed)
    key = _jax.random.fold_in(_jax.random.key(seed), 7919)
    shape, _ = _output_shape()
    out = dict(inp)
    out["loss_target"] = _jax.random.normal(_jax.random.fold_in(key, 0), shape, _jnp.float32)
    for i, name in enumerate(TWIN_WEIGHTS):
        w = inp[name].astype(_jnp.float32)
        if MOMENT_SCALE is None:
            s = _jnp.sqrt(_jnp.mean(_jnp.square(w)) + 1e-30)
        else:
            s = MOMENT_SCALE[name]
        km, kv = _jax.random.split(_jax.random.fold_in(key, i + 1))
        out[name] = w
        out["m_" + name] = s * _jax.random.normal(km, w.shape, _jnp.float32)
        out["v_" + name] = (s * s) * _jax.random.uniform(kv, w.shape, _jnp.float32, 0.5, 1.5)
    if N_MICROBATCH > 1:
        for name, axis in PER_EXAMPLE_BATCH_AXIS.items():
            out[name] = _to_microbatches(out[name], axis)
    return {'x': out['x'], 'c': out['c'], 'cond_w': out['cond_w'], 'cond_b': out['cond_b'], 'ada_w': out['ada_w'], 'ada_b': out['ada_b'], 'norm_g': out['norm_g'], 'ffn_w1': out['ffn_w1'], 'ffn_w3': out['ffn_w3'], 'ffn_w2': out['ffn_w2'], 'a_w_in': out['a_w_in'], 'a_b_in': out['a_b_in'], 'a_dw': out['a_dw'], 'a_dw_b': out['a_dw_b'], 'a_ln_g': out['a_ln_g'], 'a_ln_b': out['a_ln_b'], 'a_w_out': out['a_w_out'], 'a_b_out': out['a_b_out'], 'b_w_qkv': out['b_w_qkv'], 'b_q_g': out['b_q_g'], 'b_k_g': out['b_k_g'], 'b_w_o': out['b_w_o'], 'p_w': out['p_w'], 'p_b': out['p_b'], 'p_scale': out['p_scale'], 'loss_target': out['loss_target'], 'm_cond_w': out['m_cond_w'], 'm_cond_b': out['m_cond_b'], 'm_ada_w': out['m_ada_w'], 'm_ada_b': out['m_ada_b'], 'm_norm_g': out['m_norm_g'], 'm_ffn_w1': out['m_ffn_w1'], 'm_ffn_w3': out['m_ffn_w3'], 'm_ffn_w2': out['m_ffn_w2'], 'm_a_w_in': out['m_a_w_in'], 'm_a_b_in': out['m_a_b_in'], 'm_a_dw': out['m_a_dw'], 'm_a_dw_b': out['m_a_dw_b'], 'm_a_ln_g': out['m_a_ln_g'], 'm_a_ln_b': out['m_a_ln_b'], 'm_a_w_out': out['m_a_w_out'], 'm_a_b_out': out['m_a_b_out'], 'm_b_w_qkv': out['m_b_w_qkv'], 'm_b_q_g': out['m_b_q_g'], 'm_b_k_g': out['m_b_k_g'], 'm_b_w_o': out['m_b_w_o'], 'm_p_w': out['m_p_w'], 'm_p_b': out['m_p_b'], 'm_p_scale': out['m_p_scale'], 'v_cond_w': out['v_cond_w'], 'v_cond_b': out['v_cond_b'], 'v_ada_w': out['v_ada_w'], 'v_ada_b': out['v_ada_b'], 'v_norm_g': out['v_norm_g'], 'v_ffn_w1': out['v_ffn_w1'], 'v_ffn_w3': out['v_ffn_w3'], 'v_ffn_w2': out['v_ffn_w2'], 'v_a_w_in': out['v_a_w_in'], 'v_a_b_in': out['v_a_b_in'], 'v_a_dw': out['v_a_dw'], 'v_a_dw_b': out['v_a_dw_b'], 'v_a_ln_g': out['v_a_ln_g'], 'v_a_ln_b': out['v_a_ln_b'], 'v_a_w_out': out['v_a_w_out'], 'v_a_b_out': out['v_a_b_out'], 'v_b_w_qkv': out['v_b_w_qkv'], 'v_b_q_g': out['v_b_q_g'], 'v_b_k_g': out['v_b_k_g'], 'v_b_w_o': out['v_b_w_o'], 'v_p_w': out['v_p_w'], 'v_p_b': out['v_p_b'], 'v_p_scale': out['v_p_scale']}


def _loss(weights, diff, rest, loss_target):
    with _jax.named_scope("forward"):
        args = {**rest, TWIN_DIFF_INPUT: diff, **{k: w.astype(_WEIGHT_DTYPES[k]) for k, w in weights.items()}}
        y = _forward(args)
    with _jax.named_scope("loss_head"):
        err = _jnp.square(y.astype(_jnp.float32) - loss_target)
        return 0.5 * _jnp.sum(_jnp.mean(err, axis=-1)) if err.ndim else 0.5 * err


def _adamw(w, g, m, v):
    m = ADAM_B1 * m + (1.0 - ADAM_B1) * g
    v = ADAM_B2 * v + (1.0 - ADAM_B2) * _jnp.square(g)
    m_hat = m / (1.0 - ADAM_B1 ** ADAM_STEP)
    v_hat = v / (1.0 - ADAM_B2 ** ADAM_STEP)
    delta = -ADAM_LR * (m_hat / (_jnp.sqrt(v_hat) + ADAM_EPS) + ADAM_WD * w)
    return delta, m, v


def reference(x, c, cond_w, cond_b, ada_w, ada_b, norm_g, ffn_w1, ffn_w3, ffn_w2, a_w_in, a_b_in, a_dw, a_dw_b, a_ln_g, a_ln_b, a_w_out, a_b_out, b_w_qkv, b_q_g, b_k_g, b_w_o, p_w, p_b, p_scale, loss_target, m_cond_w, m_cond_b, m_ada_w, m_ada_b, m_norm_g, m_ffn_w1, m_ffn_w3, m_ffn_w2, m_a_w_in, m_a_b_in, m_a_dw, m_a_dw_b, m_a_ln_g, m_a_ln_b, m_a_w_out, m_a_b_out, m_b_w_qkv, m_b_q_g, m_b_k_g, m_b_w_o, m_p_w, m_p_b, m_p_scale, v_cond_w, v_cond_b, v_ada_w, v_ada_b, v_norm_g, v_ffn_w1, v_ffn_w3, v_ffn_w2, v_a_w_in, v_a_b_in, v_a_dw, v_a_dw_b, v_a_ln_g, v_a_ln_b, v_a_w_out, v_a_b_out, v_b_w_qkv, v_b_q_g, v_b_k_g, v_b_w_o, v_p_w, v_p_b, v_p_scale):
    given = dict(x=x, c=c, cond_w=cond_w, cond_b=cond_b, ada_w=ada_w, ada_b=ada_b, norm_g=norm_g, ffn_w1=ffn_w1, ffn_w3=ffn_w3, ffn_w2=ffn_w2, a_w_in=a_w_in, a_b_in=a_b_in, a_dw=a_dw, a_dw_b=a_dw_b, a_ln_g=a_ln_g, a_ln_b=a_ln_b, a_w_out=a_w_out, a_b_out=a_b_out, b_w_qkv=b_w_qkv, b_q_g=b_q_g, b_k_g=b_k_g, b_w_o=b_w_o, p_w=p_w, p_b=p_b, p_scale=p_scale, loss_target=loss_target, m_cond_w=m_cond_w, m_cond_b=m_cond_b, m_ada_w=m_ada_w, m_ada_b=m_ada_b, m_norm_g=m_norm_g, m_ffn_w1=m_ffn_w1, m_ffn_w3=m_ffn_w3, m_ffn_w2=m_ffn_w2, m_a_w_in=m_a_w_in, m_a_b_in=m_a_b_in, m_a_dw=m_a_dw, m_a_dw_b=m_a_dw_b, m_a_ln_g=m_a_ln_g, m_a_ln_b=m_a_ln_b, m_a_w_out=m_a_w_out, m_a_b_out=m_a_b_out, m_b_w_qkv=m_b_w_qkv, m_b_q_g=m_b_q_g, m_b_k_g=m_b_k_g, m_b_w_o=m_b_w_o, m_p_w=m_p_w, m_p_b=m_p_b, m_p_scale=m_p_scale, v_cond_w=v_cond_w, v_cond_b=v_cond_b, v_ada_w=v_ada_w, v_ada_b=v_ada_b, v_norm_g=v_norm_g, v_ffn_w1=v_ffn_w1, v_ffn_w3=v_ffn_w3, v_ffn_w2=v_ffn_w2, v_a_w_in=v_a_w_in, v_a_b_in=v_a_b_in, v_a_dw=v_a_dw, v_a_dw_b=v_a_dw_b, v_a_ln_g=v_a_ln_g, v_a_ln_b=v_a_ln_b, v_a_w_out=v_a_w_out, v_a_b_out=v_a_b_out, v_b_w_qkv=v_b_w_qkv, v_b_q_g=v_b_q_g, v_b_k_g=v_b_k_g, v_b_w_o=v_b_w_o, v_p_w=v_p_w, v_p_b=v_p_b, v_p_scale=v_p_scale)
    weights = {n: given[n] for n in TWIN_WEIGHTS}
    shared = {n: given[n] for n in SHARED_INPUTS}
    per_example = {n: given[n] for n in ['x', 'c']}
    grad_fn = _jax.value_and_grad(_loss, argnums=(0, 1))

    def one_microbatch(ex, loss_target):
        ex = dict(ex)
        diff = ex.pop(TWIN_DIFF_INPUT)
        return grad_fn(weights, diff, {**shared, **ex}, loss_target)

    if N_MICROBATCH == 1:
        loss, (grad_w, grad_x) = one_microbatch(per_example, given["loss_target"])
    else:
        def body(carry, xs):
            loss_sum, grad_sum = carry
            l_k, (gw_k, gx_k) = one_microbatch(xs[0], xs[1])
            with _jax.named_scope("update"):
                return (loss_sum + l_k, _jax.tree.map(_jnp.add, grad_sum, gw_k)), gx_k

        init = (_jnp.zeros((), _jnp.float32), _jax.tree.map(_jnp.zeros_like, weights))
        (loss, grad_w), grad_x = _jax.lax.scan(body, init, (per_example, given["loss_target"]))
    with _jax.named_scope("update"):
        delta_w, new_m, new_v = {}, {}, {}
        for n in TWIN_WEIGHTS:
            delta_w[n], new_m[n], new_v[n] = _adamw(weights[n], grad_w[n], given["m_" + n], given["v_" + n])
    return (loss, grad_x, *[grad_w[n] for n in TWIN_WEIGHTS], *[delta_w[n] for n in TWIN_WEIGHTS],
            *[new_m[n] for n in TWIN_WEIGHTS], *[new_v[n] for n in TWIN_WEIGHTS])
```

```python
import functools

import jax
import jax.numpy as jnp
from jax import lax
from jax.experimental import pallas as pl
from jax.experimental.pallas import tpu as pltpu

F32 = jnp.float32
BF16 = jnp.bfloat16
SDS = jax.ShapeDtypeStruct
MESH = pl.DeviceIdType.MESH
ANY = pl.BlockSpec(memory_space=pl.ANY)

EPS = 1e-6
N_CHIPS = 4
N_DEV = 8
LANES = 128
HEAD_DIM = 64
VMEM_LIMIT_BYTES = 56 * 2**20
TOKEN_TILE = 512
WGRAD_TILE = 1024
ATTN_Q_TILE = 512
ATTN_K_TILE = 256
HALO = 32
ADAM_TILE_BYTES = 1 << 20
POOL_LEVELS = (1, 2, 3, 4)

ADAM_LR, ADAM_B1, ADAM_B2, ADAM_EPS, ADAM_WD, ADAM_STEP = 0.001, 0.9, 0.999, 1e-08, 0.01, 10

WEIGHTS = ['cond_w', 'cond_b', 'ada_w', 'ada_b', 'norm_g', 'ffn_w1', 'ffn_w3', 'ffn_w2', 'a_w_in', 'a_b_in', 'a_dw',
           'a_dw_b', 'a_ln_g', 'a_ln_b', 'a_w_out', 'a_b_out', 'b_w_qkv', 'b_q_g', 'b_k_g', 'b_w_o', 'p_w', 'p_b', 'p_scale']
SMALL = {'norm_g': 2, 'a_b_in': 1, 'a_dw': 2, 'a_dw_b': 1, 'a_ln_g': 1, 'a_ln_b': 1, 'a_b_out': 1, 'p_w': 2, 'p_b': 2,
         'p_scale': 1, 'cond_b': None, 'ada_b': None, 'b_q_g': None, 'b_k_g': None}


def _tile(n, pref):
    return pref if n % pref == 0 else n


def _params(*sem):
    return pltpu.CompilerParams(dimension_semantics=sem, vmem_limit_bytes=VMEM_LIMIT_BYTES)


def _resident(shape):
    nd = len(shape)
    return pl.BlockSpec(shape, lambda *_: (0,) * nd, pipeline_mode=pl.Buffered(1))


def _rows(tm, width):
    return pl.BlockSpec((tm, width), lambda i: (i, 0))


def _acc_spec(rows, width):
    return pl.BlockSpec((rows, width), lambda i: (0, 0))


def _dot(a, b):
    return jnp.dot(a, b, preferred_element_type=F32)


def _dot_nt(a, b):
    return lax.dot_general(a, b, (((1,), (1,)), ((), ())), preferred_element_type=F32)


def _dot_tn(a, b):
    return lax.dot_general(a, b, (((0,), (0,)), ((), ())), preferred_element_type=F32)


def _hdot(a, b, dims=(((1,), (0,)), ((), ()))):
    return lax.dot_general(a, b, dims, preferred_element_type=F32, precision=lax.Precision.HIGHEST)


def _zero_at_first_step(*refs):
    @pl.when(pl.program_id(0) == 0)
    def _():
        for r in refs:
            r[...] = jnp.zeros_like(r)


def _add_rowsum(ref, row, t):
    ref[row:row + 1, :] += jnp.sum(t, axis=0, keepdims=True)


def _rms(x):
    r = lax.rsqrt(jnp.mean(x * x, axis=-1, keepdims=True) + EPS)
    return x * r, r


def _modulate_bwd(dh, n, r, gs, dxo, sums_ref):
    _add_rowsum(sums_ref, 0, dh * n)
    _add_rowsum(sums_ref, 1, dh)
    dn = dh * gs
    return dxo + r * (dn - n * jnp.mean(dn * n, axis=-1, keepdims=True))


def _silu_grad(a, sg):
    return sg * (1.0 + a * (1.0 - sg))


def _vec(*rows):
    d = rows[0].shape[-1]
    rows = [r.reshape(1, d).astype(F32) for r in rows]
    return jnp.concatenate(rows + [jnp.zeros((8 - len(rows), d), F32)], axis=0)


def _ffn_fwd(x, vec, w1, w3, w2):
    S, D = x.shape
    nch, _, fc = w1.shape
    tm = _tile(S, TOKEN_TILE)

    def body(x_ref, vec_ref, w1_ref, w3_ref, w2_ref, xo_ref, a1_ref, a3_ref, y_ref):
        x = x_ref[...]
        n, _ = _rms(x)
        h = (n * vec_ref[0:1, :] + vec_ref[1:2, :]).astype(BF16)
        acc = jnp.zeros((tm, D), F32)
        for j in range(nch):
            a1 = _dot(h, w1_ref[j]).astype(BF16)
            a3 = _dot(h, w3_ref[j]).astype(BF16)
            a1_ref[j] = a1
            a3_ref[j] = a3
            a1 = a1.astype(F32)
            u = a1 * jax.nn.sigmoid(a1) * a3.astype(F32)
            acc = acc + _dot(u.astype(BF16), w2_ref[j])
        y_ref[...] = acc.astype(BF16)
        xo_ref[...] = x + vec_ref[2:3, :] * acc

    chunked = pl.BlockSpec((nch, tm, fc), lambda i: (0, i, 0))
    return pl.pallas_call(
        body, name="ffn_fwd", grid=(S // tm,),
        in_specs=[_rows(tm, D), _resident((8, D)), _resident(w1.shape), _resident(w3.shape), _resident(w2.shape)],
        out_specs=[_rows(tm, D), chunked, chunked, _rows(tm, D)],
        out_shape=[SDS((S, D), F32), SDS((nch, S, fc), BF16), SDS((nch, S, fc), BF16), SDS((S, D), BF16)],
        compiler_params=_params("parallel"),
    )(x, vec, w1, w3, w2)


def _ffn_bwd(x, dxo, a1, a3, y, vec, w1, w3, w2):
    S, D = x.shape
    nch, _, fc = w1.shape
    tm = _tile(S, TOKEN_TILE // 2)

    def body(x_ref, dxo_ref, a1_ref, a3_ref, y_ref, vec_ref, w1_ref, w3_ref, w2_ref,
             dx_ref, h_ref, dy_ref, u_ref, da1_ref, da3_ref, sums_ref):
        _zero_at_first_step(sums_ref)
        x = x_ref[...]
        dxo = dxo_ref[...]
        gs = vec_ref[0:1, :]
        n, r = _rms(x)
        h_ref[...] = (n * gs + vec_ref[1:2, :]).astype(BF16)
        _add_rowsum(sums_ref, 2, dxo * y_ref[...].astype(F32))
        dy = (vec_ref[2:3, :] * dxo).astype(BF16)
        dy_ref[...] = dy
        dh = jnp.zeros((tm, D), F32)
        for j in range(nch):
            a1 = a1_ref[j].astype(F32)
            a3 = a3_ref[j].astype(F32)
            sg = jax.nn.sigmoid(a1)
            s = a1 * sg
            du = _dot_nt(dy, w2_ref[j])
            da1 = (du * a3 * _silu_grad(a1, sg)).astype(BF16)
            da3 = (du * s).astype(BF16)
            u_ref[j] = (s * a3).astype(BF16)
            da1_ref[j] = da1
            da3_ref[j] = da3
            dh = dh + _dot_nt(da1, w1_ref[j]) + _dot_nt(da3, w3_ref[j])
        dx_ref[...] = _modulate_bwd(dh, n, r, gs, dxo, sums_ref)

    chunked = pl.BlockSpec((nch, tm, fc), lambda i: (0, i, 0))
    return pl.pallas_call(
        body, name="ffn_bwd", grid=(S // tm,),
        in_specs=[_rows(tm, D), _rows(tm, D), chunked, chunked, _rows(tm, D), _resident((8, D)),
                  _resident(w1.shape), _resident(w3.shape), _resident(w2.shape)],
        out_specs=[_rows(tm, D), _rows(tm, D), _rows(tm, D), chunked, chunked, chunked, _acc_spec(8, D)],
        out_shape=[SDS((S, D), F32), SDS((S, D), BF16), SDS((S, D), BF16), SDS((nch, S, fc), BF16),
                   SDS((nch, S, fc), BF16), SDS((nch, S, fc), BF16), SDS((8, D), F32)],
        compiler_params=_params("arbitrary"),
    )(x, dxo, a1, a3, y, vec, w1, w3, w2)


def _wgrad(a, b, a_mode, b_mode, nch, name):
    S = a.shape[-2]
    M = a.shape[-1]
    N = b.shape[-1] // (nch if b_mode == 'col' else 1)
    ts = _tile(S, WGRAD_TILE)

    def spec(mode, width):
        if mode == 'full':
            return pl.BlockSpec((ts, width), lambda j, s: (s, 0))
        if mode == 'lead':
            return pl.BlockSpec((None, ts, width), lambda j, s: (j, s, 0))
        return pl.BlockSpec((ts, width), lambda j, s: (s, j))

    def body(a_ref, b_ref, o_ref):
        @pl.when(pl.program_id(1) == 0)
        def _():
            o_ref[...] = jnp.zeros_like(o_ref)
        o_ref[...] += _dot_tn(a_ref[...], b_ref[...])

    return pl.pallas_call(
        body, name=name, grid=(nch, S // ts), in_specs=[spec(a_mode, M), spec(b_mode, N)],
        out_specs=pl.BlockSpec((None, M, N), lambda j, s: (j, 0, 0)), out_shape=SDS((nch, M, N), F32),
        compiler_params=_params("parallel", "arbitrary"),
    )(a, b)


def _in_fwd(x, vec, w, bias, glu, name):
    S, D = x.shape
    nch, _, nc = w.shape
    N = nch * nc
    tm = _tile(S, TOKEN_TILE)

    def body(x_ref, vec_ref, w_ref, b_ref, p_ref, *u_ref):
        n, _ = _rms(x_ref[...])
        h = (n * vec_ref[0:1, :] + vec_ref[1:2, :]).astype(BF16)
        for j in range(nch):
            cols = slice(j * nc, (j + 1) * nc)
            p_ref[:, cols] = (_dot(h, w_ref[j]) + b_ref[:, cols]).astype(p_ref.dtype)
        if glu:
            half = N // 2
            u_ref[0][...] = p_ref[:, :half].astype(F32) * jax.nn.sigmoid(p_ref[:, half:].astype(F32))

    out_specs = [_rows(tm, N)] + ([_rows(tm, N // 2)] if glu else [])
    out_shape = [SDS((S, N), BF16 if glu else F32)] + ([SDS((S, N // 2), F32)] if glu else [])
    return pl.pallas_call(
        body, name=name, grid=(S // tm,),
        in_specs=[_rows(tm, D), _resident((8, D)), _resident(w.shape), _resident((1, N))],
        out_specs=out_specs, out_shape=out_shape, compiler_params=_params("parallel"),
    )(x, vec, w, bias)


def _in_bwd(x, dxo, dp, vec, w, name):
    S, D = x.shape
    nch, _, nc = w.shape
    tm = _tile(S, TOKEN_TILE)

    def body(x_ref, dxo_ref, dp_ref, vec_ref, w_ref, dx_ref, h_ref, sums_ref):
        _zero_at_first_step(sums_ref)
        gs = vec_ref[0:1, :]
        n, r = _rms(x_ref[...])
        h_ref[...] = (n * gs + vec_ref[1:2, :]).astype(BF16)
        dh = jnp.zeros((tm, D), F32)
        for j in range(nch):
            dh = dh + _dot_nt(dp_ref[:, j * nc:(j + 1) * nc], w_ref[j])
        dx_ref[...] = _modulate_bwd(dh, n, r, gs, dxo_ref[...], sums_ref)

    return pl.pallas_call(
        body, name=name, grid=(S // tm,),
        in_specs=[_rows(tm, D), _rows(tm, D), _rows(tm, nch * nc), _resident((8, D)), _resident(w.shape)],
        out_specs=[_rows(tm, D), _rows(tm, D), _acc_spec(8, D)],
        out_shape=[SDS((S, D), F32), SDS((S, D), BF16), SDS((8, D), F32)],
        compiler_params=_params("arbitrary"),
    )(x, dxo, dp, vec, w)


def _out_fwd(x, t, vec, w):
    S, D = x.shape
    tm = _tile(S, TOKEN_TILE)

    def body(x_ref, t_ref, vec_ref, w_ref, xo_ref, y_ref):
        y = _dot(t_ref[...], w_ref[...])
        y_ref[...] = y.astype(BF16)
        xo_ref[...] = x_ref[...] + vec_ref[2:3, :] * y

    return pl.pallas_call(
        body, name="attn_out_fwd", grid=(S // tm,),
        in_specs=[_rows(tm, D), _rows(tm, t.shape[1]), _resident((8, D)), _resident(w.shape)],
        out_specs=[_rows(tm, D), _rows(tm, D)], out_shape=[SDS((S, D), F32), SDS((S, D), BF16)],
        compiler_params=_params("parallel"),
    )(x, t, vec, w)


def _out_bwd(dxo, y, vec, w):
    S, D = dxo.shape
    K = w.shape[0]
    tm = _tile(S, TOKEN_TILE)

    def body(dxo_ref, y_ref, vec_ref, w_ref, dt_ref, dout_ref, sums_ref):
        _zero_at_first_step(sums_ref)
        dxo = dxo_ref[...]
        _add_rowsum(sums_ref, 2, dxo * y_ref[...].astype(F32))
        dout = (vec_ref[2:3, :] * dxo).astype(BF16)
        dout_ref[...] = dout
        dt_ref[...] = _dot_nt(dout, w_ref[...]).astype(BF16)

    return pl.pallas_call(
        body, name="attn_out_bwd", grid=(S // tm,),
        in_specs=[_rows(tm, D), _rows(tm, D), _resident((8, D)), _resident(w.shape)],
        out_specs=[_rows(tm, K), _rows(tm, D), _acc_spec(8, D)],
        out_shape=[SDS((S, K), BF16), SDS((S, D), BF16), SDS((8, D), F32)],
        compiler_params=_params("arbitrary"),
    )(dxo, y, vec, w)


def _prev_halo(tm, width):
    return pl.BlockSpec((HALO, width), lambda i: (jnp.maximum(i * (tm // HALO) - 1, 0), 0))


def _next_halo(tm, width, n_rows):
    last = n_rows // HALO - 1
    return pl.BlockSpec((HALO, width), lambda i: (jnp.minimum((i + 1) * (tm // HALO), last), 0))


def _layer_norm(v, g, b):
    mu = jnp.mean(v, axis=-1, keepdims=True)
    vc = v - mu
    rstd = lax.rsqrt(jnp.mean(vc * vc, axis=-1, keepdims=True) + EPS)
    vh = vc * rstd
    return vh * g + b, vh, rstd


def _conv_mid_fwd(u, x, vec, dw, w_out):
    S, D = x.shape
    taps = dw.shape[0] - 1
    tm = _tile(S, TOKEN_TILE // 2)

    def body(u_ref, uh_ref, x_ref, vec_ref, dw_ref, w_ref, xo_ref, v_ref, y_ref, ext):
        ext[0:HALO, :] = jnp.where(pl.program_id(0) > 0, uh_ref[...], 0.0)
        ext[HALO:, :] = u_ref[...]
        v = jnp.zeros((tm, D), F32) + vec_ref[1:2, :]
        for k in range(taps):
            v = v + dw_ref[k:k + 1, :] * ext[pl.ds(HALO - (taps - 1) + k, tm), :]
        v_ref[...] = v
        l, _, _ = _layer_norm(v, vec_ref[2:3, :], vec_ref[3:4, :])
        q = (l * jax.nn.sigmoid(l)).astype(BF16)
        y = _dot(q, w_ref[...]) + vec_ref[4:5, :]
        y_ref[...] = y.astype(BF16)
        xo_ref[...] = x_ref[...] + vec_ref[0:1, :] * y

    return pl.pallas_call(
        body, name="conv_mid_fwd", grid=(S // tm,),
        in_specs=[_rows(tm, D), _prev_halo(tm, D), _rows(tm, D), _resident((8, D)), _resident(dw.shape), _resident(w_out.shape)],
        out_specs=[_rows(tm, D), _rows(tm, D), _rows(tm, D)],
        out_shape=[SDS((S, D), F32), SDS((S, D), F32), SDS((S, D), BF16)],
        scratch_shapes=[pltpu.VMEM((HALO + tm, D), F32)],
        compiler_params=_params("parallel"),
    )(u, u, x, vec, dw, w_out)


def _conv_mid_bwd(dxo, y, v, vec, w_out):
    S, D = dxo.shape
    tm = _tile(S, TOKEN_TILE)

    def body(dxo_ref, y_ref, v_ref, vec_ref, w_ref, dv_ref, q_ref, dout_ref, sums_ref):
        _zero_at_first_step(sums_ref)
        dxo = dxo_ref[...]
        _add_rowsum(sums_ref, 0, dxo * y_ref[...].astype(F32))
        dout = vec_ref[0:1, :] * dxo
        _add_rowsum(sums_ref, 1, dout)
        dout = dout.astype(BF16)
        dout_ref[...] = dout
        ln_g = vec_ref[2:3, :]
        l, vh, rstd = _layer_norm(v_ref[...], ln_g, vec_ref[3:4, :])
        sg = jax.nn.sigmoid(l)
        q_ref[...] = (l * sg).astype(BF16)
        dl = _dot_nt(dout, w_ref[...]) * _silu_grad(l, sg)
        _add_rowsum(sums_ref, 2, dl * vh)
        _add_rowsum(sums_ref, 3, dl)
        dvh = dl * ln_g
        dv = rstd * (dvh - jnp.mean(dvh, axis=-1, keepdims=True) - vh * jnp.mean(dvh * vh, axis=-1, keepdims=True))
        _add_rowsum(sums_ref, 4, dv)
        dv_ref[...] = dv

    return pl.pallas_call(
        body, name="conv_mid_bwd", grid=(S // tm,),
        in_specs=[_rows(tm, D), _rows(tm, D), _rows(tm, D), _resident((8, D)), _resident(w_out.shape)],
        out_specs=[_rows(tm, D), _rows(tm, D), _rows(tm, D), _acc_spec(8, D)],
        out_shape=[SDS((S, D), F32), SDS((S, D), BF16), SDS((S, D), BF16), SDS((8, D), F32)],
        compiler_params=_params("arbitrary"),
    )(dxo, y, v, vec, w_out)


def _conv_transpose(dv, u, dw):
    S, D = dv.shape
    taps = dw.shape[0] - 1
    tm = _tile(S, TOKEN_TILE // 2)

    def body(dv_ref, dvn_ref, u_ref, uh_ref, dw_ref, du_ref, gdw_ref, extv, extu):
        _zero_at_first_step(gdw_ref)
        i = pl.program_id(0)
        dv = dv_ref[...]
        extv[0:tm, :] = dv
        extv[tm:, :] = jnp.where(i < pl.num_programs(0) - 1, dvn_ref[...], 0.0)
        extu[0:HALO, :] = jnp.where(i > 0, uh_ref[...], 0.0)
        extu[HALO:, :] = u_ref[...]
        du = jnp.zeros((tm, D), F32)
        for k in range(taps):
            du = du + dw_ref[k:k + 1, :] * extv[pl.ds(taps - 1 - k, tm), :]
            _add_rowsum(gdw_ref, k, dv * extu[pl.ds(HALO - (taps - 1) + k, tm), :])
        du_ref[...] = du

    return pl.pallas_call(
        body, name="conv_transpose", grid=(S // tm,),
        in_specs=[_rows(tm, D), _next_halo(tm, D, S), _rows(tm, D), _prev_halo(tm, D), _resident(dw.shape)],
        out_specs=[_rows(tm, D), _acc_spec(dw.shape[0], D)],
        out_shape=[SDS((S, D), F32), SDS(dw.shape, F32)],
        scratch_shapes=[pltpu.VMEM((tm + HALO, D), F32), pltpu.VMEM((HALO + tm, D), F32)],
        compiler_params=_params("arbitrary"),
    )(dv, dv, u, u, dw)


def _glu_bwd(du, p):
    S, D = du.shape
    tm = _tile(S, TOKEN_TILE)

    def body(du_ref, p_ref, dp_ref, sums_ref):
        _zero_at_first_step(sums_ref)
        du = du_ref[...]
        a = p_ref[:, :D].astype(F32)
        sb = jax.nn.sigmoid(p_ref[:, D:].astype(F32))
        da = du * sb
        db = du * a * sb * (1.0 - sb)
        dp_ref[:, :D] = da.astype(BF16)
        dp_ref[:, D:] = db.astype(BF16)
        sums_ref[0:1, :D] += jnp.sum(da, axis=0, keepdims=True)
        sums_ref[0:1, D:] += jnp.sum(db, axis=0, keepdims=True)

    return pl.pallas_call(
        body, name="glu_bwd", grid=(S // tm,), in_specs=[_rows(tm, D), _rows(tm, 2 * D)],
        out_specs=[_rows(tm, 2 * D), _acc_spec(8, 2 * D)], out_shape=[SDS((S, 2 * D), BF16), SDS((8, 2 * D), F32)],
        compiler_params=_params("arbitrary"),
    )(du, p)


def _head_mean(t, bd):
    hi = t.astype(BF16)
    lo = (t - hi.astype(F32)).astype(BF16)
    return (_dot(hi, bd) + _dot(lo, bd)) * (1.0 / HEAD_DIM)


def _head_blocks():
    r = lax.broadcasted_iota(jnp.int32, (LANES, LANES), 0) // HEAD_DIM
    c = lax.broadcasted_iota(jnp.int32, (LANES, LANES), 1) // HEAD_DIM
    return (r == c).astype(BF16)


def _qknorm_fwd(raw, gq, gk):
    S, D3 = raw.shape
    D = D3 // 3
    tm = _tile(S, TOKEN_TILE)

    def body(raw_ref, gq_ref, gk_ref, o_ref):
        bd = _head_blocks()
        for off, g_ref in ((0, gq_ref), (D, gk_ref)):
            for c in range(D // LANES):
                cols = slice(off + c * LANES, off + (c + 1) * LANES)
                xs = raw_ref[:, cols]
                r = lax.rsqrt(_head_mean(xs * xs, bd) + EPS)
                o_ref[:, cols] = (xs * r * g_ref[:, c * LANES:(c + 1) * LANES]).astype(BF16)
        o_ref[:, 2 * D:] = raw_ref[:, 2 * D:].astype(BF16)

    return pl.pallas_call(
        body, name="qknorm_fwd", grid=(S // tm,), in_specs=[_rows(tm, D3), _resident((1, D)), _resident((1, D))],
        out_specs=_rows(tm, D3), out_shape=SDS((S, D3), BF16), compiler_params=_params("parallel"),
    )(raw, gq, gk)


def _qknorm_bwd(dq, dk, dv, raw, gq, gk):
    S, D3 = raw.shape
    D = D3 // 3
    nb = D // LANES
    tm = _tile(S, TOKEN_TILE)

    def body(dq_ref, dk_ref, dv_ref, raw_ref, gq_ref, gk_ref, o_ref, sums_ref):
        _zero_at_first_step(sums_ref)
        bd = _head_blocks()
        for row, (off, g_ref, d_ref) in enumerate(((0, gq_ref, dq_ref), (D, gk_ref, dk_ref))):
            for c in range(nb):
                lanes = slice(c * LANES, (c + 1) * LANES)
                cols = slice(off + c * LANES, off + (c + 1) * LANES)
                xs = raw_ref[:, cols]
                r = lax.rsqrt(_head_mean(xs * xs, bd) + EPS)
                n = xs * r
                dhat = d_ref[c]
                sums_ref[row:row + 1, lanes] += jnp.sum(dhat * n, axis=0, keepdims=True)
                dn = dhat * g_ref[:, lanes]
                o_ref[:, cols] = (r * (dn - n * _head_mean(dn * n, bd))).astype(BF16)
        for c in range(nb):
            o_ref[:, 2 * D + c * LANES:2 * D + (c + 1) * LANES] = dv_ref[c].astype(BF16)

    tiles = pl.BlockSpec((nb, tm, LANES), lambda i: (0, i, 0))
    return pl.pallas_call(
        body, name="qknorm_bwd", grid=(S // tm,),
        in_specs=[tiles, tiles, tiles, _rows(tm, D3), _resident((1, D)), _resident((1, D))],
        out_specs=[_rows(tm, D3), _acc_spec(8, D)], out_shape=[SDS((S, D3), BF16), SDS((8, D), F32)],
        compiler_params=_params("arbitrary"),
    )(dq, dk, dv, raw, gq, gk)


def _softplus_parts(z):
    e = jnp.exp(-jnp.abs(z))
    return jnp.maximum(z, 0.0) + jnp.log1p(e), e


def _attn_tiles(S):
    tq = _tile(S, ATTN_Q_TILE)
    tk = _tile(tq, ATTN_K_TILE)
    return tq, tk


def _attn_consts(tq, tk):
    lane = lax.broadcasted_iota(jnp.int32, (1, LANES), 1)
    first = lane < HEAD_DIM
    r = lax.broadcasted_iota(jnp.int32, (tq, tk), 0)
    c = lax.broadcasted_iota(jnp.int32, (tq, tk), 1)
    kr = lax.broadcasted_iota(jnp.int32, (tk, tk), 0)
    kc = lax.broadcasted_iota(jnp.int32, (tk, tk), 1)
    masks = [(a * tk + c) < r for a in range(tq // tk)]
    later = (kr > kc).astype(BF16)
    earlier = (kr < kc).astype(BF16)
    return first, masks, later, earlier


def _attn_specs(S, tq, nb):
    q_spec = pl.BlockSpec((tq, LANES), lambda hp, i: (i, hp))
    k_spec = pl.BlockSpec((S, LANES), lambda hp, i: (0, nb + hp), pipeline_mode=pl.Buffered(1))
    v_spec = pl.BlockSpec((S, LANES), lambda hp, i: (0, 2 * nb + hp), pipeline_mode=pl.Buffered(1))
    return q_spec, k_spec, v_spec


def _attn_fwd(qkv):
    S, D3 = qkv.shape
    D = D3 // 3
    nb = D // LANES
    tq, tk = _attn_tiles(S)
    nsub = tq // tk

    def body(q_ref, k_ref, v_ref, o_ref, tot_ref):
        i = pl.program_id(1)
        first, masks, later, _ = _attn_consts(tq, tk)
        q = q_ref[...]
        qs = (jnp.where(first, q, jnp.zeros_like(q)), jnp.where(first, jnp.zeros_like(q), q))

        def block(j, carry, mask):
            rows = pl.ds(pl.multiple_of(j * tk, tk), tk)
            kb = k_ref[rows, :]
            vb = v_ref[rows, :]
            out = []
            for h in range(2):
                o, c = carry[h]
                z = _dot_nt(qs[h], kb)
                sp, _ = _softplus_parts(z)
                spm = sp if mask is None else jnp.where(mask, sp, 0.0)
                a = jnp.exp(z - sp - _dot(spm.astype(BF16), later) - c)
                if mask is not None:
                    a = jnp.where(mask, a, 0.0)
                out.append((o + _dot(a.astype(BF16), vb), c + jnp.sum(spm, axis=1, keepdims=True)))
            return tuple(out)

        carry = tuple((jnp.zeros((tq, LANES), F32), jnp.zeros((tq, 1), F32)) for _ in range(2))
        for a in reversed(range(nsub)):
            carry = block(i * nsub + a, carry, masks[a])
        carry = lax.fori_loop(0, i * nsub, lambda jj, cr: block(i * nsub - 1 - jj, cr, None), carry)
        (o_a, c_a), (o_b, c_b) = carry
        o_ref[...] = jnp.where(first, o_a, o_b).astype(BF16)
        tot_ref[...] = jnp.where(first, c_a, c_b)

    q_spec, k_spec, v_spec = _attn_specs(S, tq, nb)
    return pl.pallas_call(
        body, name="attn_fwd", grid=(nb, S // tq), in_specs=[q_spec, k_spec, v_spec],
        out_specs=[q_spec, q_spec], out_shape=[SDS((S, D), BF16), SDS((S, D), F32)],
        compiler_params=_params("parallel", "parallel"),
    )(qkv, qkv, qkv)


def _attn_bwd(qkv, do, tot):
    S, D3 = qkv.shape
    D = D3 // 3
    nb = D // LANES
    tq, tk = _attn_tiles(S)
    nsub = tq // tk

    def body(q_ref, k_ref, v_ref, do_ref, tot_ref, dq_ref, dk_hbm, dv_hbm, dk_acc, dv_acc, sem):
        hp = pl.program_id(0)
        i = pl.program_id(1)

        @pl.when(i == 0)
        def _():
            dk_acc[...] = jnp.zeros_like(dk_acc)
            dv_acc[...] = jnp.zeros_like(dv_acc)

        first, masks, later, earlier = _attn_consts(tq, tk)
        q = q_ref[...]
        do = do_ref[...]
        zero = jnp.zeros_like(q)
        qs = (jnp.where(first, q, zero), jnp.where(first, zero, q))
        dos = (jnp.where(first, do, zero), jnp.where(first, zero, do))
        tots = (tot_ref[:, 0:1], tot_ref[:, HEAD_DIM:HEAD_DIM + 1])

        def block(j, carry, mask):
            rows = pl.ds(pl.multiple_of(j * tk, tk), tk)
            kb = k_ref[rows, :]
            vb = v_ref[rows, :]
            dk_blk = jnp.zeros((tk, LANES), F32)
            dv_blk = jnp.zeros((tk, LANES), F32)
            out = []
            for h in range(2):
                dq, cum, pre = carry[h]
                z = _dot_nt(qs[h], kb)
                sp, e = _softplus_parts(z)
                sig = jnp.where(z >= 0.0, 1.0, e) / (1.0 + e)
                spm = sp if mask is None else jnp.where(mask, sp, 0.0)
                cum = cum + jnp.sum(spm, axis=1, keepdims=True)
                a = jnp.exp(z - sp - (tots[h] - cum) - _dot(spm.astype(BF16), later))
                if mask is not None:
                    a = jnp.where(mask, a, 0.0)
                g = _dot_nt(dos[h], vb) * a
                dz = g * (1.0 - sig) - sig * (pre + _dot(g.astype(BF16), earlier))
                if mask is not None:
                    dz = jnp.where(mask, dz, 0.0)
                dz = dz.astype(BF16)
                dk_blk = dk_blk + _dot_tn(dz, qs[h])
                dv_blk = dv_blk + _dot_tn(a.astype(BF16), dos[h])
                out.append((dq + _dot(dz, kb), cum, pre + jnp.sum(g, axis=1, keepdims=True)))
            dk_acc[rows, :] += dk_blk
            dv_acc[rows, :] += dv_blk
            return tuple(out)

        carry = tuple((jnp.zeros((tq, LANES), F32), jnp.zeros((tq, 1), F32), jnp.zeros((tq, 1), F32)) for _ in range(2))
        carry = lax.fori_loop(0, i * nsub, lambda j, cr: block(j, cr, None), carry)
        for a in range(nsub):
            carry = block(i * nsub + a, carry, masks[a])
        dq_ref[...] = jnp.where(first, carry[0][0], carry[1][0])

        @pl.when(i == pl.num_programs(1) - 1)
        def _():
            ck = pltpu.make_async_copy(dk_acc, dk_hbm.at[hp], sem.at[0])
            cv = pltpu.make_async_copy(dv_acc, dv_hbm.at[hp], sem.at[1])
            ck.start()
            cv.start()
            ck.wait()
            cv.wait()

    q_spec, k_spec, v_spec = _attn_specs(S, tq, nb)
    slab = SDS((nb, S, LANES), F32)
    return pl.pallas_call(
        body, name="attn_bwd", grid=(nb, S // tq), in_specs=[q_spec, k_spec, v_spec, q_spec, q_spec],
        out_specs=[pl.BlockSpec((None, tq, LANES), lambda hp, i: (hp, i, 0)), ANY, ANY], out_shape=[slab, slab, slab],
        scratch_shapes=[pltpu.VMEM((S, LANES), F32), pltpu.VMEM((S, LANES), F32), pltpu.SemaphoreType.DMA((2,))],
        compiler_params=_params("arbitrary", "arbitrary"),
    )(qkv, qkv, qkv, do, tot)


def _trail_sum(ext, bufs, cols, levels, n):
    def src(lo, size):
        return ext[pl.ds(lo, size), cols]
    for l in range(levels):
        lo = 8 * (l + 1)
        dst = bufs[l % 2]
        dst[lo:, :] = src(lo, n - lo) + src(lo - (1 << l), n - lo)
        def src(lo_, size, d=dst):
            return d[pl.ds(lo_, size), :]
    return src(HALO, n - HALO)


def _lead_sum(ext, bufs, cols, levels, n):
    def src(lo, size):
        return ext[pl.ds(lo, size), cols]
    for l in range(levels):
        hi = n - 8 * (l + 1)
        dst = bufs[l % 2]
        dst[0:hi, :] = src(0, hi) + src(1 << l, hi)
        def src(lo_, size, d=dst):
            return d[pl.ds(lo_, size), :]
    return src(0, n - HALO)


def _pool_diffs(x_ref, xh_ref, vec_ref, ext, bufs, tm, D):
    i = pl.program_id(0)
    gs, shift = vec_ref[0:1, :], vec_ref[1:2, :]
    n, r = _rms(x_ref[...])
    nh, _ = _rms(xh_ref[...])
    ext[0:HALO, :] = jnp.where(i > 0, nh * gs + shift, 0.0)
    ext[HALO:, :] = n * gs + shift
    t = i * tm + lax.broadcasted_iota(jnp.int32, (tm, 1), 0)
    dg = D // len(POOL_LEVELS)
    out = []
    for g, lv in enumerate(POOL_LEVELS):
        cols = slice(g * dg, (g + 1) * dg)
        inv = 1.0 / jnp.minimum(t + 1, 1 << lv).astype(F32)
        out.append((_trail_sum(ext, bufs, cols, lv, HALO + tm) * inv - ext[HALO:, cols], inv))
    return out, n, r


def _pool_fwd(x, vec, pw):
    S, D = x.shape
    ng, dg, _ = pw.shape
    tm = _tile(S, TOKEN_TILE)

    def body(x_ref, xh_ref, vec_ref, pw_ref, xo_ref, ext, buf_a, buf_b):
        diffs, _, _ = _pool_diffs(x_ref, xh_ref, vec_ref, ext, (buf_a, buf_b), tm, D)
        for g, (d, _) in enumerate(diffs):
            cols = slice(g * dg, (g + 1) * dg)
            y = (_dot(d.astype(BF16), pw_ref[g]) + vec_ref[4:5, cols]) * vec_ref[3:4, cols]
            xo_ref[:, cols] = x_ref[:, cols] + vec_ref[2:3, cols] * y

    return pl.pallas_call(
        body, name="pool_fwd", grid=(S // tm,),
        in_specs=[_rows(tm, D), _prev_halo(tm, D), _resident((8, D)), _resident(pw.shape)],
        out_specs=_rows(tm, D), out_shape=SDS((S, D), F32),
        scratch_shapes=[pltpu.VMEM((HALO + tm, D), F32), pltpu.VMEM((HALO + tm, dg), F32), pltpu.VMEM((HALO + tm, dg), F32)],
        compiler_params=_params("parallel"),
    )(x, x, vec, pw)


def _pool_bwd(x, dxo, vec, pw):
    S, D = x.shape
    ng, dg, _ = pw.shape
    tm = _tile(S, TOKEN_TILE)

    def body(x_ref, xh_ref, dxo_ref, dxn_ref, vec_ref, pw_ref, dx_ref, gpw_ref, sums_ref, ext, exte, buf_a, buf_b):
        _zero_at_first_step(gpw_ref, sums_ref)
        i = pl.program_id(0)
        bufs = (buf_a, buf_b)
        diffs, n, r = _pool_diffs(x_ref, xh_ref, vec_ref, ext, bufs, tm, D)
        gate, scale = vec_ref[2:3, :], vec_ref[3:4, :]
        dxo = dxo_ref[...]
        dyp_next = jnp.where(i < pl.num_programs(0) - 1, dxn_ref[...], 0.0) * gate * scale
        t_next = (i + 1) * tm + lax.broadcasted_iota(jnp.int32, (HALO, 1), 0)
        for g, (d, inv) in enumerate(diffs):
            cols = slice(g * dg, (g + 1) * dg)
            w = pw_ref[g]
            db = d.astype(BF16)
            ypre = _dot(db, w) + vec_ref[4:5, cols]
            dy = gate[:, cols] * dxo[:, cols]
            sums_ref[2:3, cols] += jnp.sum(dxo[:, cols] * ypre * scale[:, cols], axis=0, keepdims=True)
            sums_ref[3:4, cols] += jnp.sum(dy * ypre, axis=0, keepdims=True)
            dyp = dy * scale[:, cols]
            sums_ref[4:5, cols] += jnp.sum(dyp, axis=0, keepdims=True)
            dypb = dyp.astype(BF16)
            gpw_ref[g] += _dot_tn(db, dypb)
            dd = _dot_nt(dypb, w)
            dd_next = _dot_nt(dyp_next[:, cols].astype(BF16), w)
            inv_next = 1.0 / jnp.minimum(t_next + 1, 1 << POOL_LEVELS[g]).astype(F32)
            exte[0:tm, cols] = dd * inv
            exte[tm:, cols] = dd_next * inv_next
            ext[HALO:, cols] = _lead_sum(exte, bufs, cols, POOL_LEVELS[g], tm + HALO) - dd
        dx_ref[...] = _modulate_bwd(ext[HALO:, :], n, r, vec_ref[0:1, :], dxo, sums_ref)

    return pl.pallas_call(
        body, name="pool_bwd", grid=(S // tm,),
        in_specs=[_rows(tm, D), _prev_halo(tm, D), _rows(tm, D), _next_halo(tm, D, S), _resident((8, D)), _resident(pw.shape)],
        out_specs=[_rows(tm, D), pl.BlockSpec(pw.shape, lambda i: (0, 0, 0)), _acc_spec(8, D)],
        out_shape=[SDS((S, D), F32), SDS(pw.shape, F32), SDS((8, D), F32)],
        scratch_shapes=[pltpu.VMEM((HALO + tm, D), F32), pltpu.VMEM((tm + HALO, D), F32),
                        pltpu.VMEM((HALO + tm, dg), F32), pltpu.VMEM((HALO + tm, dg), F32)],
        compiler_params=_params("arbitrary"),
    )(x, x, dxo, dxo, vec, pw)


def _loss_head(y, target):
    S, D = y.shape
    tm = _tile(S, TOKEN_TILE)

    def body(y_ref, t_ref, dy_ref, sums_ref):
        _zero_at_first_step(sums_ref)
        err = y_ref[...] - t_ref[...]
        _add_rowsum(sums_ref, 0, err * err)
        dy_ref[...] = err * (1.0 / D)

    return pl.pallas_call(
        body, name="loss_head", grid=(S // tm,), in_specs=[_rows(tm, D), _rows(tm, D)],
        out_specs=[_rows(tm, D), _acc_spec(8, D)], out_shape=[SDS((S, D), F32), SDS((8, D), F32)],
        compiler_params=_params("arbitrary"),
    )(y, target)


def _cond_pre(c_cols, cond_w):
    def body(c_ref, w_ref, o_ref):
        o_ref[...] = _hdot(c_ref[...], w_ref[...])
    return pl.pallas_call(body, name="cond_pre", out_shape=SDS((c_cols.shape[0], cond_w.shape[1]), F32))(c_cols, cond_w)


def _cond_e(parts, cond_b):
    def body(p_ref, b_ref, pre_ref, e_ref):
        pre = p_ref[0] + p_ref[2] + p_ref[4] + p_ref[6] + b_ref[...]
        pre_ref[...] = pre
        e_ref[...] = pre * jax.nn.sigmoid(pre)
    shape = SDS(parts.shape[1:], F32)
    return pl.pallas_call(body, name="cond_e", out_shape=[shape, shape])(parts, cond_b)


def _mod_cols(e, ada_w, ada_b_cols):
    L, D, nc = ada_w.shape
    B = e.shape[0]

    def body(e_ref, w_ref, b_ref, o_ref):
        o_ref[...] = _hdot(e_ref[...], w_ref[...]) + b_ref[...]

    return pl.pallas_call(
        body, name="mod_cols", grid=(L,),
        in_specs=[pl.BlockSpec((B, D), lambda l: (0, 0)), pl.BlockSpec((None, D, nc), lambda l: (l, 0, 0)),
                  pl.BlockSpec((None, 1, nc), lambda l: (l, 0, 0))],
        out_specs=pl.BlockSpec((None, B, nc), lambda l: (l, 0, 0)), out_shape=SDS((L, B, nc), F32),
        compiler_params=_params("parallel"),
    )(e, ada_w, ada_b_cols)


def _mod_bwd(e, dmod_cols, ada_w):
    L, D, nc = ada_w.shape
    B = e.shape[0]

    def body(e_ref, d_ref, w_ref, gw_ref, de_ref):
        _zero_at_first_step(de_ref)
        gw_ref[...] = _hdot(e_ref[...], d_ref[...], (((0,), (0,)), ((), ())))
        de_ref[...] += _hdot(d_ref[...], w_ref[...], (((1,), (1,)), ((), ())))

    return pl.pallas_call(
        body, name="mod_bwd", grid=(L,),
        in_specs=[pl.BlockSpec((B, D), lambda l: (0, 0)), pl.BlockSpec((None, B, nc), lambda l: (l, 0, 0)),
                  pl.BlockSpec((None, D, nc), lambda l: (l, 0, 0))],
        out_specs=[pl.BlockSpec((None, D, nc), lambda l: (l, 0, 0)), pl.BlockSpec((B, D), lambda l: (0, 0))],
        out_shape=[SDS((L, D, nc), F32), SDS((B, D), F32)], compiler_params=_params("arbitrary"),
    )(e, dmod_cols, ada_w)


def _cond_bwd(de_parts, pre, c_cols):
    def body(p_ref, pre_ref, c_ref, gw_ref, gb_ref):
        pre = pre_ref[...]
        dpre = (p_ref[0] + p_ref[2] + p_ref[4] + p_ref[6]) * _silu_grad(pre, jax.nn.sigmoid(pre))
        gb_ref[...] = jnp.sum(dpre, axis=0, keepdims=True)
        gw_ref[...] = _hdot(c_ref[...], dpre, (((0,), (0,)), ((), ())))
    D = pre.shape[1]
    return pl.pallas_call(body, name="cond_bwd", out_shape=[SDS((c_cols.shape[1], D), F32), SDS((1, D), F32)])(de_parts, pre, c_cols)


def _as_rows(a):
    return a.reshape(-1, a.shape[-1])


def _row_tile(rows, width, n_arrays):
    t = max(8, (ADAM_TILE_BYTES // (4 * width)) // 8 * 8)
    while rows % t:
        t -= 8
        if t <= 0:
            return rows
    return t


def _adamw(w, g, m, v):
    shape = w.shape
    w, g, m, v = (_as_rows(a) for a in (w, g, m, v))
    R, C = w.shape
    tr = _row_tile(R, C, 7)

    def body(w_ref, g_ref, m_ref, v_ref, d_ref, nm_ref, nv_ref):
        g = g_ref[...]
        m = ADAM_B1 * m_ref[...] + (1.0 - ADAM_B1) * g
        v = ADAM_B2 * v_ref[...] + (1.0 - ADAM_B2) * (g * g)
        m_hat = m / (1.0 - ADAM_B1 ** ADAM_STEP)
        v_hat = v / (1.0 - ADAM_B2 ** ADAM_STEP)
        d_ref[...] = -ADAM_LR * (m_hat / (jnp.sqrt(v_hat) + ADAM_EPS) + ADAM_WD * w_ref[...])
        nm_ref[...] = m
        nv_ref[...] = v

    spec = _rows(tr, C)
    outs = pl.pallas_call(
        body, name="adamw", grid=(R // tr,), in_specs=[spec] * 4, out_specs=[spec] * 3,
        out_shape=[SDS((R, C), F32)] * 3, compiler_params=_params("parallel"),
    )(w, g, m, v)
    return tuple(o.reshape(shape) for o in outs)


def _sum_lead(a, out_dtype=F32):
    n = a.shape[0]
    shape = a.shape[1:]
    a = a.reshape(n, -1, a.shape[-1])
    _, R, C = a.shape
    tr = _row_tile(R, C, n + 1)

    def body(a_ref, o_ref):
        acc = a_ref[0].astype(F32)
        for k in range(1, n):
            acc = acc + a_ref[k].astype(F32)
        o_ref[...] = acc.astype(out_dtype)

    out = pl.pallas_call(
        body, name="sum_lead", grid=(R // tr,), in_specs=[pl.BlockSpec((n, tr, C), lambda i: (0, i, 0))],
        out_specs=_rows(tr, C), out_shape=SDS((R, C), out_dtype), compiler_params=_params("parallel"),
    )(a)
    return out.reshape(shape)


def _place():
    return lax.axis_index("x"), lax.axis_index("y"), lax.axis_index("c")


def _allgather8(a):
    def body(a_ref, o_ref, send, recv, local):
        mx, my, mc = _place()
        me = 4 * mx + 2 * my + mc
        mine = pltpu.make_async_copy(a_ref, o_ref.at[me], local)
        mine.start()
        copies = []
        for k in range(1, N_DEV):
            peer = (1 - mx if k & 4 else mx, 1 - my if k & 2 else my, 1 - mc if k & 1 else mc)
            cp = pltpu.make_async_remote_copy(a_ref, o_ref.at[me], send.at[k - 1], recv.at[k - 1], device_id=peer, device_id_type=MESH)
            cp.start()
            copies.append(cp)
        for cp in copies:
            cp.wait()
        mine.wait()

    return pl.pallas_call(
        body, name="allgather8", in_specs=[ANY], out_specs=ANY, out_shape=SDS((N_DEV,) + a.shape, a.dtype),
        scratch_shapes=[pltpu.SemaphoreType.DMA((N_DEV - 1,)), pltpu.SemaphoreType.DMA((N_DEV - 1,)), pltpu.SemaphoreType.DMA],
    )(a)


def _chip_exchange(arrs, scatter, name):
    n = len(arrs)
    out_shape = [SDS((N_CHIPS,) + (a.shape[1:] if scatter else a.shape), a.dtype) for a in arrs]

    def body(*refs):
        ins, outs, (send, recv, local) = refs[:n], refs[n:2 * n], refs[2 * n:]
        mx, my, mc = _place()
        chip = 2 * mx + my
        copies = []
        for a in range(n):
            cp = pltpu.make_async_copy(ins[a].at[chip] if scatter else ins[a], outs[a].at[chip], local.at[a])
            cp.start()
            copies.append(cp)
        for k in range(1, N_CHIPS):
            px, py = (1 - mx if k & 2 else mx), (1 - my if k & 1 else my)
            for a in range(n):
                s = (k - 1) * n + a
                cp = pltpu.make_async_remote_copy(ins[a].at[2 * px + py] if scatter else ins[a], outs[a].at[chip],
                                                  send.at[s], recv.at[s], device_id=(px, py, mc), device_id_type=MESH)
                cp.start()
                copies.append(cp)
        for cp in copies:
            cp.wait()

    return pl.pallas_call(
        body, name=name, in_specs=[ANY] * n, out_specs=[ANY] * n, out_shape=out_shape,
        scratch_shapes=[pltpu.SemaphoreType.DMA((3 * n,)), pltpu.SemaphoreType.DMA((3 * n,)), pltpu.SemaphoreType.DMA((n,))],
    )(*arrs)


def _pair_split(groups):
    sizes = [len(g) for g in groups]
    flat = [a for g in groups for a in g]
    n = len(flat)
    out_shape = []
    for g in groups:
        out_shape += [SDS((len(g),) + g[0].shape[1:], g[0].dtype)] * 2

    def body(*refs):
        ins, outs, (send, recv, local) = refs[:n], refs[n:n + 2 * len(groups)], refs[n + 2 * len(groups):]
        mx, my, mc = _place()
        copies = []
        a = 0
        for gi, size in enumerate(sizes):
            kept, got = outs[2 * gi], outs[2 * gi + 1]
            for l in range(size):
                cp = pltpu.make_async_copy(ins[a].at[mc], kept.at[l], local.at[a])
                cp.start()
                copies.append(cp)
                cp = pltpu.make_async_remote_copy(ins[a].at[1 - mc], got.at[l], send.at[a], recv.at[a],
                                                  device_id=(mx, my, 1 - mc), device_id_type=MESH)
                cp.start()
                copies.append(cp)
                a += 1
        for cp in copies:
            cp.wait()

    outs = pl.pallas_call(
        body, name="pair_split", in_specs=[ANY] * n, out_specs=[ANY] * len(out_shape), out_shape=out_shape,
        scratch_shapes=[pltpu.SemaphoreType.DMA((n,)), pltpu.SemaphoreType.DMA((n,)), pltpu.SemaphoreType.DMA((n,))],
    )(*flat)
    return [(outs[2 * gi], outs[2 * gi + 1]) for gi in range(len(groups))]


def _pair_gather(arrs):
    n = len(arrs)

    def body(*refs):
        ins, outs, (send, recv, local) = refs[:n], refs[n:2 * n], refs[2 * n:]
        mx, my, mc = _place()
        copies = []
        for a in range(n):
            cp = pltpu.make_async_copy(ins[a], outs[a].at[mc], local.at[a])
            cp.start()
            copies.append(cp)
            cp = pltpu.make_async_remote_copy(ins[a], outs[a].at[mc], send.at[a], recv.at[a],
                                              device_id=(mx, my, 1 - mc), device_id_type=MESH)
            cp.start()
            copies.append(cp)
        for cp in copies:
            cp.wait()

    return pl.pallas_call(
        body, name="pair_gather", in_specs=[ANY] * n, out_specs=[ANY] * n, out_shape=[SDS((2,) + a.shape, a.dtype) for a in arrs],
        scratch_shapes=[pltpu.SemaphoreType.DMA((n,)), pltpu.SemaphoreType.DMA((n,)), pltpu.SemaphoreType.DMA((n,))],
    )(*arrs)


def _add_cast(a, b, dtype):
    shape = a.shape
    a, b = _as_rows(a), _as_rows(b)
    R, C = a.shape
    tr = _row_tile(R, C, 3)

    def body(a_ref, b_ref, o_ref):
        o_ref[...] = (a_ref[...] + b_ref[...]).astype(dtype)

    out = pl.pallas_call(body, name="pair_sum", grid=(R // tr,), in_specs=[_rows(tr, C)] * 2, out_specs=_rows(tr, C),
                         out_shape=SDS((R, C), dtype), compiler_params=_params("parallel"))(a, b)
    return out.reshape(shape)


def _pack(arrs):
    flat = jnp.concatenate([a.reshape(-1).astype(F32) for a in arrs])
    pad = (-flat.shape[0]) % (8 * LANES)
    return jnp.pad(flat, (0, pad)).reshape(-1, LANES)


def _unpack(buf, shapes):
    flat = buf.reshape(buf.shape[:-2] + (-1,))
    out, off = [], 0
    for s in shapes:
        size = 1
        for d in s:
            size *= d
        out.append(flat[..., off:off + size].reshape(flat.shape[:-1] + tuple(s)))
        off += size
    return out


def _unshard(stacked, axis):
    moved = jnp.moveaxis(stacked, 0, axis)
    return moved.reshape(moved.shape[:axis] + (N_CHIPS * moved.shape[axis + 1],) + moved.shape[axis + 2:])


def _my_shard(full, axis, chip):
    size = full.shape[axis] // N_CHIPS
    return lax.dynamic_slice_in_dim(full, chip * size, size, axis)


def _halves(a, axis):
    s = a.shape
    return jnp.moveaxis(a.reshape(s[:axis] + (2, s[axis] // 2) + s[axis + 1:]), axis, 0)


def kernel(x, c, cond_w, cond_b, ada_w, ada_b, norm_g, ffn_w1, ffn_w3, ffn_w2, a_w_in, a_b_in, a_dw, a_dw_b, a_ln_g, a_ln_b, a_w_out, a_b_out, b_w_qkv, b_q_g, b_k_g, b_w_o, p_w, p_b, p_scale, loss_target, m_cond_w, m_cond_b, m_ada_w, m_ada_b, m_norm_g, m_ffn_w1, m_ffn_w3, m_ffn_w2, m_a_w_in, m_a_b_in, m_a_dw, m_a_dw_b, m_a_ln_g, m_a_ln_b, m_a_w_out, m_a_b_out, m_b_w_qkv, m_b_q_g, m_b_k_g, m_b_w_o, m_p_w, m_p_b, m_p_scale, v_cond_w, v_cond_b, v_ada_w, v_ada_b, v_norm_g, v_ffn_w1, v_ffn_w3, v_ffn_w2, v_a_w_in, v_a_b_in, v_a_dw, v_a_dw_b, v_a_ln_g, v_a_ln_b, v_a_w_out, v_a_b_out, v_b_w_qkv, v_b_q_g, v_b_k_g, v_b_w_o, v_p_w, v_p_b, v_p_scale):
    w_in = dict(cond_w=cond_w, cond_b=cond_b, ada_w=ada_w, ada_b=ada_b, norm_g=norm_g, ffn_w1=ffn_w1, ffn_w3=ffn_w3, ffn_w2=ffn_w2,
                a_w_in=a_w_in, a_b_in=a_b_in, a_dw=a_dw, a_dw_b=a_dw_b, a_ln_g=a_ln_g, a_ln_b=a_ln_b, a_w_out=a_w_out, a_b_out=a_b_out,
                b_w_qkv=b_w_qkv, b_q_g=b_q_g, b_k_g=b_k_g, b_w_o=b_w_o, p_w=p_w, p_b=p_b, p_scale=p_scale)
    m_in = dict(zip(WEIGHTS, (m_cond_w, m_cond_b, m_ada_w, m_ada_b, m_norm_g, m_ffn_w1, m_ffn_w3, m_ffn_w2, m_a_w_in, m_a_b_in, m_a_dw,
                              m_a_dw_b, m_a_ln_g, m_a_ln_b, m_a_w_out, m_a_b_out, m_b_w_qkv, m_b_q_g, m_b_k_g, m_b_w_o, m_p_w, m_p_b, m_p_scale)))
    v_in = dict(zip(WEIGHTS, (v_cond_w, v_cond_b, v_ada_w, v_ada_b, v_norm_g, v_ffn_w1, v_ffn_w3, v_ffn_w2, v_a_w_in, v_a_b_in, v_a_dw,
                              v_a_dw_b, v_a_ln_g, v_a_ln_b, v_a_w_out, v_a_b_out, v_b_w_qkv, v_b_q_g, v_b_k_g, v_b_w_o, v_p_w, v_p_b, v_p_scale)))
    x = x[0]
    target = loss_target[0]
    S, D = x.shape
    L = ada_w.shape[0]
    assert b_q_g.shape[-1] == HEAD_DIM and D % LANES == 0 and S % HALO == 0
    mx, my, mc = _place()
    chip = 2 * mx + my
    me = 2 * chip + mc

    big = [ffn_w1.astype(BF16), ffn_w3.astype(BF16), ffn_w2.astype(BF16), a_w_in.astype(BF16), a_w_out.astype(BF16),
           b_w_qkv.astype(BF16), b_w_o.astype(BF16)]
    sharded = [k for k, ax in SMALL.items() if ax is not None]
    gathered = _chip_exchange(big + [_pack([w_in[k] for k in sharded])], scatter=False, name="weight_gather")
    g_w1, g_w3, g_w2, g_ain, g_aout, g_qkv, g_wo, g_small = gathered
    full = {k: _unshard(a, SMALL[k]) for k, a in zip(sharded, _unpack(g_small, [w_in[k].shape for k in sharded]))}
    n_conv, n_pool = a_w_in.shape[0], p_w.shape[0]
    conv_dw = [jnp.pad(full['a_dw'][ia], ((0, 1), (0, 0))) for ia in range(n_conv)]
    pool_w = [full['p_w'][ic].astype(BF16) for ic in range(n_pool)]
    qk_scale = HEAD_DIM ** -0.5
    gq = jnp.tile(b_q_g[0], D // HEAD_DIM).reshape(1, D) * qk_scale
    gk = jnp.tile(b_k_g[0], D // HEAD_DIM).reshape(1, D)

    c_all = _allgather8(c)[:, 0, :]
    c_cols = _my_shard(c_all, 1, chip)
    pre, e = _cond_e(_allgather8(_cond_pre(c_cols, cond_w)), cond_b.reshape(1, D))
    nc = ada_w.shape[2]
    mod_c = _mod_cols(e, ada_w, _my_shard(ada_b, 1, chip).reshape(L, 1, nc))
    mod_all = _allgather8(mod_c.reshape(L * N_DEV, nc)).reshape(N_CHIPS, 2, L, N_DEV, nc)[:, 0]
    mod = jnp.moveaxis(lax.dynamic_index_in_dim(mod_all, me, axis=2, keepdims=False), 0, 1).reshape(L, 3, 3, D)
    shift, scale, gate = mod[:, :, 0], mod[:, :, 1], 1.0 + mod[:, :, 2]
    gains = full['norm_g']

    def mod_vec(i, k, gate_factor=1.0):
        return _vec(gains[i, k] * (1.0 + scale[i, k]), shift[i, k], gate_factor * gate[i, k])

    saved = []
    ia = ib = ic = 0
    for i in range(L):
        for k, half in ((0, 0), (1, None), (2, 1)):
            if half is not None:
                vec = mod_vec(i, k, 0.5)
                w1, w3, w2 = g_w1[:, i, half], g_w3[:, i, half], g_w2[:, i, half]
                xo, a1, a3, y = _ffn_fwd(x, vec, w1, w3, w2)
                saved.append(('ffn', i, k, half, x, vec, (a1, a3, y, w1, w3, w2)))
            elif i % 3 == 0:
                vec = mod_vec(i, k)
                w_a = g_ain[:, ia]
                w_o = g_aout[:, ia].reshape(D, D)
                p, u = _in_fwd(x, vec, w_a, full['a_b_in'][ia].reshape(1, 2 * D), True, "conv_in_fwd")
                cvec = _vec(gate[i, k], full['a_dw_b'][ia], full['a_ln_g'][ia], full['a_ln_b'][ia], full['a_b_out'][ia])
                xo, v, y = _conv_mid_fwd(u, x, cvec, conv_dw[ia], w_o)
                saved.append(('conv', i, k, ia, x, vec, (p, u, v, y, cvec, w_a, w_o)))
                ia += 1
            elif i % 3 == 1:
                vec = mod_vec(i, k)
                w_q = g_qkv[:, ib]
                w_o = g_wo[:, ib].reshape(D, D)
                raw, = _in_fwd(x, vec, w_q, jnp.zeros((1, 3 * D), F32), False, "attn_in_fwd")
                qkv = _qknorm_fwd(raw, gq, gk)
                o, tot = _attn_fwd(qkv)
                xo, y = _out_fwd(x, o, vec, w_o)
                saved.append(('attn', i, k, ib, x, vec, (raw, qkv, o, tot, y, w_q, w_o)))
                ib += 1
            else:
                vec = _vec(gains[i, k] * (1.0 + scale[i, k]), shift[i, k], gate[i, k], full['p_scale'][ic], full['p_b'][ic].reshape(D))
                xo = _pool_fwd(x, vec, pool_w[ic])
                saved.append(('pool', i, k, ic, x, vec, ()))
                ic += 1
            x = xo

    dx, sq = _loss_head(x, target)
    loss = lax.psum(0.5 / D * jnp.sum(sq[0]), ("x", "y", "c"))

    zeros_like_full = lambda k: jnp.zeros(full[k].shape, F32)
    g_full = {k: zeros_like_full(k) for k in sharded}
    g_full['b_q_g'] = jnp.zeros_like(b_q_g)
    g_full['b_k_g'] = jnp.zeros_like(b_k_g)
    dmod = jnp.zeros((L, 3, 3, D), F32)
    big_grads = {k: {} for k in ('ffn_w1', 'ffn_w3', 'ffn_w2', 'a_w_in', 'a_w_out', 'b_w_qkv', 'b_w_o')}

    def put(name, idx, val):
        g_full[name] = g_full[name].at[idx].set(val.reshape(g_full[name][idx].shape))

    for kind, i, k, idx, xin, vec, res in reversed(saved):
        if kind == 'ffn':
            a1, a3, y, w1, w3, w2 = res
            dx, h, dy, u, da1, da3, sums = _ffn_bwd(xin, dx, a1, a3, y, vec, w1, w3, w2)
            big_grads['ffn_w1'][(i, idx)] = _wgrad(h, da1, 'full', 'lead', N_CHIPS, "ffn_w1_grad")
            big_grads['ffn_w3'][(i, idx)] = _wgrad(h, da3, 'full', 'lead', N_CHIPS, "ffn_w3_grad")
            big_grads['ffn_w2'][(i, idx)] = _wgrad(u, dy, 'lead', 'full', N_CHIPS, "ffn_w2_grad")
            dgate = 0.5 * sums[2]
        elif kind == 'conv':
            p, u, v, y, cvec, w_a, w_o = res
            dv, q, dout, csums = _conv_mid_bwd(dx, y, v, cvec, w_o)
            big_grads['a_w_out'][idx] = _wgrad(q, dout, 'full', 'full', 1, "conv_w_out_grad").reshape(N_CHIPS, D // N_CHIPS, D)
            du, gdw = _conv_transpose(dv, u, conv_dw[idx])
            dp, psums = _glu_bwd(du, p)
            dx, h, sums = _in_bwd(xin, dx, dp, vec, w_a, "conv_in_bwd")
            big_grads['a_w_in'][idx] = _wgrad(h, dp, 'full', 'col', N_CHIPS, "conv_w_in_grad")
            dgate = csums[0]
            put('a_b_out', idx, csums[1])
            put('a_ln_g', idx, csums[2])
            put('a_ln_b', idx, csums[3])
            put('a_dw_b', idx, csums[4])
            put('a_dw', idx, gdw[:-1])
            put('a_b_in', idx, psums[0])
        elif kind == 'attn':
            raw, qkv, o, tot, y, w_q, w_o = res
            do, dout, osums = _out_bwd(dx, y, vec, w_o)
            big_grads['b_w_o'][idx] = _wgrad(o, dout, 'full', 'full', 1, "attn_w_o_grad").reshape(N_CHIPS, D // N_CHIPS, D)
            dq, dk, dvv = _attn_bwd(qkv, do, tot)
            draw, qsums = _qknorm_bwd(dq, dk, dvv, raw, gq, gk)
            dx, h, sums = _in_bwd(xin, dx, draw, vec, w_q, "attn_in_bwd")
            big_grads['b_w_qkv'][idx] = _wgrad(h, draw, 'full', 'col', N_CHIPS, "attn_w_qkv_grad")
            dgate = osums[2]
            put('b_q_g', idx, qk_scale * jnp.sum(qsums[0].reshape(-1, HEAD_DIM), axis=0))
            put('b_k_g', idx, jnp.sum(qsums[1].reshape(-1, HEAD_DIM), axis=0))
        else:
            dx, gpw, sums = _pool_bwd(xin, dx, vec, pool_w[idx])
            dgate = sums[2]
            put('p_w', idx, gpw)
            put('p_scale', idx, sums[3])
            put('p_b', idx, sums[4])
        put('norm_g', (i, k), sums[0] * (1.0 + scale[i, k]))
        dmod = dmod.at[i, k].set(jnp.stack([sums[1], sums[0] * gains[i, k], dgate]))
    grad_x = dx[None]

    dmod_all = _allgather8(dmod.reshape(L, 9 * D))
    g_ada_b = _sum_lead(dmod_all)
    dmod_cols = jnp.moveaxis(_my_shard(dmod_all, 2, chip), 0, 1)
    g_ada_w, de_part = _mod_bwd(e, dmod_cols, ada_w)
    g_cond_w, g_cond_b = _cond_bwd(_allgather8(de_part), pre, c_cols)

    small_names = list(SMALL)
    g_full['cond_b'] = g_cond_b.reshape(D)
    g_full['ada_b'] = g_ada_b
    reduced = [k for k in small_names if k not in ('cond_b', 'ada_b')]
    red = _unpack(_sum_lead(_allgather8(_pack([g_full[k] for k in reduced]))), [g_full[k].shape for k in reduced])
    for k, a in zip(reduced, red):
        g_full[k] = a
    grads = {k: (g_full[k] if SMALL[k] is None else _my_shard(g_full[k], SMALL[k], chip)) for k in small_names}
    grads['cond_w'] = g_cond_w
    grads['ada_w'] = g_ada_w

    def stack_halves(name, keys, axis):
        return [_halves(big_grads[name][key], axis) for key in keys]

    ffn_keys = [(i, h) for i in range(L) for h in range(2)]
    groups = [stack_halves('ffn_w1', ffn_keys, 1), stack_halves('ffn_w3', ffn_keys, 1), stack_halves('ffn_w2', ffn_keys, 1),
              stack_halves('a_w_in', range(n_conv), 1), stack_halves('a_w_out', range(n_conv), 1),
              stack_halves('b_w_qkv', range(b_w_qkv.shape[0]), 1), stack_halves('b_w_o', range(b_w_o.shape[0]), 1)]
    pair = [_add_cast(kept, got, BF16) for kept, got in _pair_split(groups)]
    from_chips = _chip_exchange([jnp.moveaxis(a, 1, 0) for a in pair], scatter=True, name="grad_scatter")
    halves = _pair_gather([_sum_lead(a) for a in from_chips])
    for name, a in zip(('ffn_w1', 'ffn_w3', 'ffn_w2', 'a_w_in', 'a_w_out', 'b_w_qkv', 'b_w_o'), halves):
        a = jnp.moveaxis(a, 0, 1)
        grads[name] = a.reshape(w_in[name].shape)

    delta, new_m, new_v = {}, {}, {}
    packed = [_pack([d[k] for k in small_names]) for d in (w_in, grads, m_in, v_in)]
    shapes = [w_in[k].shape for k in small_names]
    for out, buf in zip((delta, new_m, new_v), _adamw(*packed)):
        out.update(zip(small_names, _unpack(buf, shapes)))
    for k in WEIGHTS:
        if k not in SMALL:
            delta[k], new_m[k], new_v[k] = _adamw(w_in[k], grads[k], m_in[k], v_in[k])
    return (loss, grad_x, *[grads[k] for k in WEIGHTS], *[delta[k] for k in WEIGHTS], *[new_m[k] for k in WEIGHTS],
            *[new_v[k] for k in WEIGHTS])
```

```python
import functools

import jax
import jax.numpy as jnp
from jax import lax
from jax.experimental import pallas as pl
from jax.experimental.pallas import tpu as pltpu

F32 = jnp.float32
BF16 = jnp.bfloat16
SDS = jax.ShapeDtypeStruct
MESH = pl.DeviceIdType.MESH
ANY = pl.BlockSpec(memory_space=pl.ANY)

EPS = 1e-6
N_CHIPS = 4
N_DEV = 8
LANES = 128
HEAD_DIM = 64
VMEM_LIMIT_BYTES = 56 * 2**20
TOKEN_TILE = 512
WGRAD_TILE = 1024
ATTN_Q_TILE = 1024
ATTN_K_TILE = 256
LOG2E = 1.4426950408889634
LN2 = 0.6931471805599453
HALO = 32
ADAM_TILE_BYTES = 1 << 20
POOL_LEVELS = (1, 2, 3, 4)

ADAM_LR, ADAM_B1, ADAM_B2, ADAM_EPS, ADAM_WD, ADAM_STEP = 0.001, 0.9, 0.999, 1e-08, 0.01, 10

WEIGHTS = ['cond_w', 'cond_b', 'ada_w', 'ada_b', 'norm_g', 'ffn_w1', 'ffn_w3', 'ffn_w2', 'a_w_in', 'a_b_in', 'a_dw',
           'a_dw_b', 'a_ln_g', 'a_ln_b', 'a_w_out', 'a_b_out', 'b_w_qkv', 'b_q_g', 'b_k_g', 'b_w_o', 'p_w', 'p_b', 'p_scale']
SMALL = {'norm_g': 2, 'a_b_in': 1, 'a_dw': 2, 'a_dw_b': 1, 'a_ln_g': 1, 'a_ln_b': 1, 'a_b_out': 1, 'p_w': 2, 'p_b': 2,
         'p_scale': 1, 'cond_b': None, 'ada_b': None, 'b_q_g': None, 'b_k_g': None}


def _tile(n, pref):
    return pref if n % pref == 0 else n


def _params(*sem):
    return pltpu.CompilerParams(dimension_semantics=sem, vmem_limit_bytes=VMEM_LIMIT_BYTES)


def _resident(shape):
    nd = len(shape)
    return pl.BlockSpec(shape, lambda *_: (0,) * nd, pipeline_mode=pl.Buffered(1))


def _rows(tm, width):
    return pl.BlockSpec((tm, width), lambda i: (i, 0))


def _acc_spec(rows, width):
    return pl.BlockSpec((rows, width), lambda i: (0, 0))


def _dot(a, b):
    return jnp.dot(a, b, preferred_element_type=F32)


def _dot_nt(a, b):
    return lax.dot_general(a, b, (((1,), (1,)), ((), ())), preferred_element_type=F32)


def _dot_tn(a, b):
    return lax.dot_general(a, b, (((0,), (0,)), ((), ())), preferred_element_type=F32)


def _hdot(a, b, dims=(((1,), (0,)), ((), ()))):
    return lax.dot_general(a, b, dims, preferred_element_type=F32, precision=lax.Precision.HIGHEST)


def _zero_at_first_step(*refs):
    @pl.when(pl.program_id(0) == 0)
    def _():
        for r in refs:
            r[...] = jnp.zeros_like(r)


def _add_rowsum(ref, row, t):
    ref[row:row + 1, :] += jnp.sum(t, axis=0, keepdims=True)


def _rms(x):
    r = lax.rsqrt(jnp.mean(x * x, axis=-1, keepdims=True) + EPS)
    return x * r, r


def _modulate_bwd(dh, n, r, gs, dxo, sums_ref):
    _add_rowsum(sums_ref, 0, dh * n)
    _add_rowsum(sums_ref, 1, dh)
    dn = dh * gs
    return dxo + r * (dn - n * jnp.mean(dn * n, axis=-1, keepdims=True))


def _silu_grad(a, sg):
    return sg * (1.0 + a * (1.0 - sg))


def _vec(*rows):
    d = rows[0].shape[-1]
    rows = [r.reshape(1, d).astype(F32) for r in rows]
    return jnp.concatenate(rows + [jnp.zeros((8 - len(rows), d), F32)], axis=0)


def _ffn_fwd(x, vec, w1, w3, w2):
    S, D = x.shape
    nch, _, fc = w1.shape
    tm = _tile(S, TOKEN_TILE)

    def body(x_ref, vec_ref, w1_ref, w3_ref, w2_ref, xo_ref, a1_ref, a3_ref, y_ref):
        x = x_ref[...]
        n, _ = _rms(x)
        h = (n * vec_ref[0:1, :] + vec_ref[1:2, :]).astype(BF16)
        acc = jnp.zeros((tm, D), F32)
        for j in range(nch):
            a1 = _dot(h, w1_ref[j]).astype(BF16)
            a3 = _dot(h, w3_ref[j]).astype(BF16)
            a1_ref[j] = a1
            a3_ref[j] = a3
            a1 = a1.astype(F32)
            u = a1 * jax.nn.sigmoid(a1) * a3.astype(F32)
            acc = acc + _dot(u.astype(BF16), w2_ref[j])
        y_ref[...] = acc.astype(BF16)
        xo_ref[...] = x + vec_ref[2:3, :] * acc

    chunked = pl.BlockSpec((nch, tm, fc), lambda i: (0, i, 0))
    return pl.pallas_call(
        body, name="ffn_fwd", grid=(S // tm,),
        in_specs=[_rows(tm, D), _resident((8, D)), _resident(w1.shape), _resident(w3.shape), _resident(w2.shape)],
        out_specs=[_rows(tm, D), chunked, chunked, _rows(tm, D)],
        out_shape=[SDS((S, D), F32), SDS((nch, S, fc), BF16), SDS((nch, S, fc), BF16), SDS((S, D), BF16)],
        compiler_params=_params("parallel"),
    )(x, vec, w1, w3, w2)


def _ffn_bwd(x, dxo, a1, a3, y, vec, w1, w3, w2):
    S, D = x.shape
    nch, _, fc = w1.shape
    tm = _tile(S, TOKEN_TILE // 2)

    def body(x_ref, dxo_ref, a1_ref, a3_ref, y_ref, vec_ref, w1_ref, w3_ref, w2_ref,
             dx_ref, h_ref, dy_ref, u_ref, da1_ref, da3_ref, sums_ref):
        _zero_at_first_step(sums_ref)
        x = x_ref[...]
        dxo = dxo_ref[...]
        gs = vec_ref[0:1, :]
        n, r = _rms(x)
        h_ref[...] = (n * gs + vec_ref[1:2, :]).astype(BF16)
        _add_rowsum(sums_ref, 2, dxo * y_ref[...].astype(F32))
        dy = (vec_ref[2:3, :] * dxo).astype(BF16)
        dy_ref[...] = dy
        dh = jnp.zeros((tm, D), F32)
        for j in range(nch):
            a1 = a1_ref[j].astype(F32)
            a3 = a3_ref[j].astype(F32)
            sg = jax.nn.sigmoid(a1)
            s = a1 * sg
            du = _dot_nt(dy, w2_ref[j])
            da1 = (du * a3 * _silu_grad(a1, sg)).astype(BF16)
            da3 = (du * s).astype(BF16)
            u_ref[j] = (s * a3).astype(BF16)
            da1_ref[j] = da1
            da3_ref[j] = da3
            dh = dh + _dot_nt(da1, w1_ref[j]) + _dot_nt(da3, w3_ref[j])
        dx_ref[...] = _modulate_bwd(dh, n, r, gs, dxo, sums_ref)

    chunked = pl.BlockSpec((nch, tm, fc), lambda i: (0, i, 0))
    return pl.pallas_call(
        body, name="ffn_bwd", grid=(S // tm,),
        in_specs=[_rows(tm, D), _rows(tm, D), chunked, chunked, _rows(tm, D), _resident((8, D)),
                  _resident(w1.shape), _resident(w3.shape), _resident(w2.shape)],
        out_specs=[_rows(tm, D), _rows(tm, D), _rows(tm, D), chunked, chunked, chunked, _acc_spec(8, D)],
        out_shape=[SDS((S, D), F32), SDS((S, D), BF16), SDS((S, D), BF16), SDS((nch, S, fc), BF16),
                   SDS((nch, S, fc), BF16), SDS((nch, S, fc), BF16), SDS((8, D), F32)],
        compiler_params=_params("arbitrary"),
    )(x, dxo, a1, a3, y, vec, w1, w3, w2)


def _wgrad(a, b, a_mode, b_mode, nch, name, acc, slot):
    S = a.shape[-2]
    M = a.shape[-1]
    N = b.shape[-1] // (nch if b_mode == 'col' else 1)
    assert acc.shape[1:] == (nch, M, N)
    ts = _tile(S, WGRAD_TILE)

    def spec(mode, width):
        if mode == 'full':
            return pl.BlockSpec((ts, width), lambda j, s: (s, 0))
        if mode == 'lead':
            return pl.BlockSpec((None, ts, width), lambda j, s: (j, s, 0))
        return pl.BlockSpec((ts, width), lambda j, s: (s, j))

    def body(a_ref, b_ref, acc_ref, o_ref):
        @pl.when(pl.program_id(1) == 0)
        def _():
            o_ref[...] = jnp.zeros_like(o_ref)
        o_ref[...] += _dot_tn(a_ref[...], b_ref[...])

    return pl.pallas_call(
        body, name=name, grid=(nch, S // ts), in_specs=[spec(a_mode, M), spec(b_mode, N), ANY],
        out_specs=pl.BlockSpec((None, None, M, N), lambda j, s: (slot, j, 0, 0)), out_shape=SDS(acc.shape, F32),
        input_output_aliases={2: 0}, compiler_params=_params("parallel", "arbitrary"),
    )(a, b, acc)


def _in_fwd(x, vec, w, bias, glu, name):
    S, D = x.shape
    nch, _, nc = w.shape
    N = nch * nc
    tm = _tile(S, TOKEN_TILE)

    def body(x_ref, vec_ref, w_ref, b_ref, p_ref, *u_ref):
        n, _ = _rms(x_ref[...])
        h = (n * vec_ref[0:1, :] + vec_ref[1:2, :]).astype(BF16)
        for j in range(nch):
            cols = slice(j * nc, (j + 1) * nc)
            p_ref[:, cols] = (_dot(h, w_ref[j]) + b_ref[:, cols]).astype(p_ref.dtype)
        if glu:
            half = N // 2
            u_ref[0][...] = p_ref[:, :half].astype(F32) * jax.nn.sigmoid(p_ref[:, half:].astype(F32))

    out_specs = [_rows(tm, N)] + ([_rows(tm, N // 2)] if glu else [])
    out_shape = [SDS((S, N), BF16 if glu else F32)] + ([SDS((S, N // 2), F32)] if glu else [])
    return pl.pallas_call(
        body, name=name, grid=(S // tm,),
        in_specs=[_rows(tm, D), _resident((8, D)), _resident(w.shape), _resident((1, N))],
        out_specs=out_specs, out_shape=out_shape, compiler_params=_params("parallel"),
    )(x, vec, w, bias)


def _in_bwd(x, dxo, dp, vec, w, name):
    S, D = x.shape
    nch, _, nc = w.shape
    tm = _tile(S, TOKEN_TILE)

    def body(x_ref, dxo_ref, dp_ref, vec_ref, w_ref, dx_ref, h_ref, sums_ref):
        _zero_at_first_step(sums_ref)
        gs = vec_ref[0:1, :]
        n, r = _rms(x_ref[...])
        h_ref[...] = (n * gs + vec_ref[1:2, :]).astype(BF16)
        dh = jnp.zeros((tm, D), F32)
        for j in range(nch):
            dh = dh + _dot_nt(dp_ref[:, j * nc:(j + 1) * nc], w_ref[j])
        dx_ref[...] = _modulate_bwd(dh, n, r, gs, dxo_ref[...], sums_ref)

    return pl.pallas_call(
        body, name=name, grid=(S // tm,),
        in_specs=[_rows(tm, D), _rows(tm, D), _rows(tm, nch * nc), _resident((8, D)), _resident(w.shape)],
        out_specs=[_rows(tm, D), _rows(tm, D), _acc_spec(8, D)],
        out_shape=[SDS((S, D), F32), SDS((S, D), BF16), SDS((8, D), F32)],
        compiler_params=_params("arbitrary"),
    )(x, dxo, dp, vec, w)


def _out_fwd(x, t, vec, w):
    S, D = x.shape
    tm = _tile(S, TOKEN_TILE)

    def body(x_ref, t_ref, vec_ref, w_ref, xo_ref, y_ref):
        y = _dot(t_ref[...], w_ref[...])
        y_ref[...] = y.astype(BF16)
        xo_ref[...] = x_ref[...] + vec_ref[2:3, :] * y

    return pl.pallas_call(
        body, name="attn_out_fwd", grid=(S // tm,),
        in_specs=[_rows(tm, D), _rows(tm, t.shape[1]), _resident((8, D)), _resident(w.shape)],
        out_specs=[_rows(tm, D), _rows(tm, D)], out_shape=[SDS((S, D), F32), SDS((S, D), BF16)],
        compiler_params=_params("parallel"),
    )(x, t, vec, w)


def _out_bwd(dxo, y, vec, w):
    S, D = dxo.shape
    K = w.shape[0]
    tm = _tile(S, TOKEN_TILE)

    def body(dxo_ref, y_ref, vec_ref, w_ref, dt_ref, dout_ref, sums_ref):
        _zero_at_first_step(sums_ref)
        dxo = dxo_ref[...]
        _add_rowsum(sums_ref, 2, dxo * y_ref[...].astype(F32))
        dout = (vec_ref[2:3, :] * dxo).astype(BF16)
        dout_ref[...] = dout
        dt_ref[...] = _dot_nt(dout, w_ref[...]).astype(BF16)

    return pl.pallas_call(
        body, name="attn_out_bwd", grid=(S // tm,),
        in_specs=[_rows(tm, D), _rows(tm, D), _resident((8, D)), _resident(w.shape)],
        out_specs=[_rows(tm, K), _rows(tm, D), _acc_spec(8, D)],
        out_shape=[SDS((S, K), BF16), SDS((S, D), BF16), SDS((8, D), F32)],
        compiler_params=_params("arbitrary"),
    )(dxo, y, vec, w)


def _prev_halo(tm, width):
    return pl.BlockSpec((HALO, width), lambda i: (jnp.maximum(i * (tm // HALO) - 1, 0), 0))


def _next_halo(tm, width, n_rows):
    last = n_rows // HALO - 1
    return pl.BlockSpec((HALO, width), lambda i: (jnp.minimum((i + 1) * (tm // HALO), last), 0))


def _layer_norm(v, g, b):
    mu = jnp.mean(v, axis=-1, keepdims=True)
    vc = v - mu
    rstd = lax.rsqrt(jnp.mean(vc * vc, axis=-1, keepdims=True) + EPS)
    vh = vc * rstd
    return vh * g + b, vh, rstd


def _conv_mid_fwd(u, x, vec, dw, w_out):
    S, D = x.shape
    taps = dw.shape[0] - 1
    tm = _tile(S, TOKEN_TILE // 2)

    def body(u_ref, uh_ref, x_ref, vec_ref, dw_ref, w_ref, xo_ref, v_ref, y_ref, ext):
        ext[0:HALO, :] = jnp.where(pl.program_id(0) > 0, uh_ref[...], 0.0)
        ext[HALO:, :] = u_ref[...]
        v = jnp.zeros((tm, D), F32) + vec_ref[1:2, :]
        for k in range(taps):
            v = v + dw_ref[k:k + 1, :] * ext[pl.ds(HALO - (taps - 1) + k, tm), :]
        v_ref[...] = v
        l, _, _ = _layer_norm(v, vec_ref[2:3, :], vec_ref[3:4, :])
        q = (l * jax.nn.sigmoid(l)).astype(BF16)
        y = _dot(q, w_ref[...]) + vec_ref[4:5, :]
        y_ref[...] = y.astype(BF16)
        xo_ref[...] = x_ref[...] + vec_ref[0:1, :] * y

    return pl.pallas_call(
        body, name="conv_mid_fwd", grid=(S // tm,),
        in_specs=[_rows(tm, D), _prev_halo(tm, D), _rows(tm, D), _resident((8, D)), _resident(dw.shape), _resident(w_out.shape)],
        out_specs=[_rows(tm, D), _rows(tm, D), _rows(tm, D)],
        out_shape=[SDS((S, D), F32), SDS((S, D), F32), SDS((S, D), BF16)],
        scratch_shapes=[pltpu.VMEM((HALO + tm, D), F32)],
        compiler_params=_params("parallel"),
    )(u, u, x, vec, dw, w_out)


def _conv_mid_bwd(dxo, y, v, vec, w_out):
    S, D = dxo.shape
    tm = _tile(S, TOKEN_TILE)

    def body(dxo_ref, y_ref, v_ref, vec_ref, w_ref, dv_ref, q_ref, dout_ref, sums_ref):
        _zero_at_first_step(sums_ref)
        dxo = dxo_ref[...]
        _add_rowsum(sums_ref, 0, dxo * y_ref[...].astype(F32))
        dout = vec_ref[0:1, :] * dxo
        _add_rowsum(sums_ref, 1, dout)
        dout = dout.astype(BF16)
        dout_ref[...] = dout
        ln_g = vec_ref[2:3, :]
        l, vh, rstd = _layer_norm(v_ref[...], ln_g, vec_ref[3:4, :])
        sg = jax.nn.sigmoid(l)
        q_ref[...] = (l * sg).astype(BF16)
        dl = _dot_nt(dout, w_ref[...]) * _silu_grad(l, sg)
        _add_rowsum(sums_ref, 2, dl * vh)
        _add_rowsum(sums_ref, 3, dl)
        dvh = dl * ln_g
        dv = rstd * (dvh - jnp.mean(dvh, axis=-1, keepdims=True) - vh * jnp.mean(dvh * vh, axis=-1, keepdims=True))
        _add_rowsum(sums_ref, 4, dv)
        dv_ref[...] = dv

    return pl.pallas_call(
        body, name="conv_mid_bwd", grid=(S // tm,),
        in_specs=[_rows(tm, D), _rows(tm, D), _rows(tm, D), _resident((8, D)), _resident(w_out.shape)],
        out_specs=[_rows(tm, D), _rows(tm, D), _rows(tm, D), _acc_spec(8, D)],
        out_shape=[SDS((S, D), F32), SDS((S, D), BF16), SDS((S, D), BF16), SDS((8, D), F32)],
        compiler_params=_params("arbitrary"),
    )(dxo, y, v, vec, w_out)


def _conv_transpose(dv, u, dw):
    S, D = dv.shape
    taps = dw.shape[0] - 1
    tm = _tile(S, TOKEN_TILE // 2)

    def body(dv_ref, dvn_ref, u_ref, uh_ref, dw_ref, du_ref, gdw_ref, extv, extu):
        _zero_at_first_step(gdw_ref)
        i = pl.program_id(0)
        dv = dv_ref[...]
        extv[0:tm, :] = dv
        extv[tm:, :] = jnp.where(i < pl.num_programs(0) - 1, dvn_ref[...], 0.0)
        extu[0:HALO, :] = jnp.where(i > 0, uh_ref[...], 0.0)
        extu[HALO:, :] = u_ref[...]
        du = jnp.zeros((tm, D), F32)
        for k in range(taps):
            du = du + dw_ref[k:k + 1, :] * extv[pl.ds(taps - 1 - k, tm), :]
            _add_rowsum(gdw_ref, k, dv * extu[pl.ds(HALO - (taps - 1) + k, tm), :])
        du_ref[...] = du

    return pl.pallas_call(
        body, name="conv_transpose", grid=(S // tm,),
        in_specs=[_rows(tm, D), _next_halo(tm, D, S), _rows(tm, D), _prev_halo(tm, D), _resident(dw.shape)],
        out_specs=[_rows(tm, D), _acc_spec(dw.shape[0], D)],
        out_shape=[SDS((S, D), F32), SDS(dw.shape, F32)],
        scratch_shapes=[pltpu.VMEM((tm + HALO, D), F32), pltpu.VMEM((HALO + tm, D), F32)],
        compiler_params=_params("arbitrary"),
    )(dv, dv, u, u, dw)


def _glu_bwd(du, p):
    S, D = du.shape
    tm = _tile(S, TOKEN_TILE)

    def body(du_ref, p_ref, dp_ref, sums_ref):
        _zero_at_first_step(sums_ref)
        du = du_ref[...]
        a = p_ref[:, :D].astype(F32)
        sb = jax.nn.sigmoid(p_ref[:, D:].astype(F32))
        da = du * sb
        db = du * a * sb * (1.0 - sb)
        dp_ref[:, :D] = da.astype(BF16)
        dp_ref[:, D:] = db.astype(BF16)
        sums_ref[0:1, :D] += jnp.sum(da, axis=0, keepdims=True)
        sums_ref[0:1, D:] += jnp.sum(db, axis=0, keepdims=True)

    return pl.pallas_call(
        body, name="glu_bwd", grid=(S // tm,), in_specs=[_rows(tm, D), _rows(tm, 2 * D)],
        out_specs=[_rows(tm, 2 * D), _acc_spec(8, 2 * D)], out_shape=[SDS((S, 2 * D), BF16), SDS((8, 2 * D), F32)],
        compiler_params=_params("arbitrary"),
    )(du, p)


def _head_mean(t, bd):
    hi = t.astype(BF16)
    lo = (t - hi.astype(F32)).astype(BF16)
    return (_dot(hi, bd) + _dot(lo, bd)) * (1.0 / HEAD_DIM)


def _head_blocks():
    r = lax.broadcasted_iota(jnp.int32, (LANES, LANES), 0) // HEAD_DIM
    c = lax.broadcasted_iota(jnp.int32, (LANES, LANES), 1) // HEAD_DIM
    return (r == c).astype(BF16)


def _qknorm_fwd(raw, gq, gk):
    S, D3 = raw.shape
    D = D3 // 3
    tm = _tile(S, TOKEN_TILE)

    def body(raw_ref, gq_ref, gk_ref, o_ref):
        bd = _head_blocks()
        for off, g_ref in ((0, gq_ref), (D, gk_ref)):
            for c in range(D // LANES):
                cols = slice(off + c * LANES, off + (c + 1) * LANES)
                xs = raw_ref[:, cols]
                r = lax.rsqrt(_head_mean(xs * xs, bd) + EPS)
                o_ref[:, cols] = (xs * r * g_ref[:, c * LANES:(c + 1) * LANES]).astype(BF16)
        o_ref[:, 2 * D:] = raw_ref[:, 2 * D:].astype(BF16)

    return pl.pallas_call(
        body, name="qknorm_fwd", grid=(S // tm,), in_specs=[_rows(tm, D3), _resident((1, D)), _resident((1, D))],
        out_specs=_rows(tm, D3), out_shape=SDS((S, D3), BF16), compiler_params=_params("parallel"),
    )(raw, gq, gk)


def _qknorm_bwd(dq, dk, dv, raw, gq, gk):
    S, D3 = raw.shape
    D = D3 // 3
    nb = D // LANES
    tm = _tile(S, TOKEN_TILE)

    def body(dq_ref, dk_ref, dv_ref, raw_ref, gq_ref, gk_ref, o_ref, sums_ref):
        _zero_at_first_step(sums_ref)
        bd = _head_blocks()
        for row, (off, g_ref, d_ref) in enumerate(((0, gq_ref, dq_ref), (D, gk_ref, dk_ref))):
            for c in range(nb):
                lanes = slice(c * LANES, (c + 1) * LANES)
                cols = slice(off + c * LANES, off + (c + 1) * LANES)
                xs = raw_ref[:, cols]
                r = lax.rsqrt(_head_mean(xs * xs, bd) + EPS)
                n = xs * r
                dhat = d_ref[c]
                sums_ref[row:row + 1, lanes] += jnp.sum(dhat * n, axis=0, keepdims=True)
                dn = dhat * g_ref[:, lanes]
                o_ref[:, cols] = (r * (dn - n * _head_mean(dn * n, bd))).astype(BF16)
        for c in range(nb):
            o_ref[:, 2 * D + c * LANES:2 * D + (c + 1) * LANES] = dv_ref[c].astype(BF16)

    tiles = pl.BlockSpec((nb, tm, LANES), lambda i: (0, i, 0))
    return pl.pallas_call(
        body, name="qknorm_bwd", grid=(S // tm,),
        in_specs=[tiles, tiles, tiles, _rows(tm, D3), _resident((1, D)), _resident((1, D))],
        out_specs=[_rows(tm, D3), _acc_spec(8, D)], out_shape=[SDS((S, D3), BF16), SDS((8, D), F32)],
        compiler_params=_params("arbitrary"),
    )(dq, dk, dv, raw, gq, gk)


def _softplus2(z):
    return jnp.maximum(z, jnp.log2(1.0 + jnp.exp2(jnp.minimum(z, 30.0))))


def _attn_tiles(S):
    tq = _tile(S, ATTN_Q_TILE)
    tk = _tile(tq, ATTN_K_TILE)
    return tq, tk


def _attn_consts(tq, tk):
    lane = lax.broadcasted_iota(jnp.int32, (1, LANES), 1)
    first = lane < HEAD_DIM
    r = lax.broadcasted_iota(jnp.int32, (tq, tk), 0)
    c = lax.broadcasted_iota(jnp.int32, (tq, tk), 1)
    kr = lax.broadcasted_iota(jnp.int32, (tk, tk), 0)
    kc = lax.broadcasted_iota(jnp.int32, (tk, tk), 1)
    masks = [(a * tk + c) < r for a in range(tq // tk)]
    later = (kr > kc).astype(BF16)
    upto = (kr <= kc).astype(BF16)
    return first, masks, later, upto


def _attn_specs(S, tq, nb):
    q_spec = pl.BlockSpec((tq, LANES), lambda hp, i: (i, hp))
    k_spec = pl.BlockSpec((S, LANES), lambda hp, i: (0, nb + hp), pipeline_mode=pl.Buffered(1))
    v_spec = pl.BlockSpec((S, LANES), lambda hp, i: (0, 2 * nb + hp), pipeline_mode=pl.Buffered(1))
    return q_spec, k_spec, v_spec


def _attn_fwd(qkv):
    S, D3 = qkv.shape
    D = D3 // 3
    nb = D // LANES
    tq, tk = _attn_tiles(S)
    nsub = tq // tk

    def body(q_ref, k_ref, v_ref, o_ref, tot_ref):
        i = pl.program_id(1)
        first, masks, later, _ = _attn_consts(tq, tk)
        q = q_ref[...]
        qs = (jnp.where(first, q, jnp.zeros_like(q)), jnp.where(first, jnp.zeros_like(q), q))

        def band(jb, carry, band_masks):
            out = []
            for h in range(2):
                o, c = carry[h]
                for a in reversed(range(nsub)):
                    mask = None if band_masks is None else band_masks[a]
                    rows = pl.ds(pl.multiple_of((jb * nsub + a) * tk, tk), tk)
                    kb = k_ref[rows, :]
                    z = _dot_nt(qs[h], kb)
                    sp = _softplus2(z)
                    logsig = z - sp
                    if mask is not None:
                        sp = jnp.where(mask, sp, 0.0)
                    av = jnp.exp2(logsig - _dot(sp.astype(BF16), later) - c)
                    if mask is not None:
                        av = jnp.where(mask, av, 0.0)
                    o = o + _dot(av.astype(BF16), v_ref[rows, :])
                    c = c + jnp.sum(sp, axis=1, keepdims=True)
                out.append((o, c))
            return tuple(out)

        carry = tuple((jnp.zeros((tq, LANES), F32), jnp.zeros((tq, 1), F32)) for _ in range(2))
        carry = band(i, carry, masks)
        carry = lax.fori_loop(0, i, lambda jj, cr: band(i - 1 - jj, cr, None), carry)
        (o_a, c_a), (o_b, c_b) = carry
        o_ref[...] = jnp.where(first, o_a, o_b).astype(BF16)
        tot_ref[...] = jnp.where(first, c_a, c_b)

    q_spec, k_spec, v_spec = _attn_specs(S, tq, nb)
    return pl.pallas_call(
        body, name="attn_fwd", grid=(nb, S // tq), in_specs=[q_spec, k_spec, v_spec],
        out_specs=[q_spec, q_spec], out_shape=[SDS((S, D), BF16), SDS((S, D), F32)],
        compiler_params=_params("parallel", "parallel"),
    )(qkv, qkv, qkv)


def _attn_bwd(qkv, do, tot):
    S, D3 = qkv.shape
    D = D3 // 3
    nb = D // LANES
    tq, tk = _attn_tiles(S)
    nsub = tq // tk
    nkb = S // tk

    def body(q_ref, k_ref, v_ref, do_ref, tot_ref, dq_ref, dk_hbm, dv_hbm, dkt_acc, dvt_acc, stage, sem):
        hp = pl.program_id(0)
        i = pl.program_id(1)

        @pl.when(i == 0)
        def _():
            dkt_acc[...] = jnp.zeros_like(dkt_acc)
            dvt_acc[...] = jnp.zeros_like(dvt_acc)

        first, masks, later, upto = _attn_consts(tq, tk)
        q = q_ref[...]
        do = do_ref[...]
        zero = jnp.zeros_like(q)
        qs = (jnp.where(first, q, zero), jnp.where(first, zero, q))
        dos = (jnp.where(first, do, zero), jnp.where(first, zero, do))
        top = lax.broadcasted_iota(jnp.int32, (LANES, 1), 0) < HEAD_DIM
        qt = q.astype(F32).T
        dot_ = do.astype(F32).T
        qts = (jnp.where(top, qt, 0.0).astype(BF16), jnp.where(top, 0.0, qt).astype(BF16))
        dots = (jnp.where(top, dot_, 0.0).astype(BF16), jnp.where(top, 0.0, dot_).astype(BF16))
        tots = (tot_ref[:, 0:1], tot_ref[:, HEAD_DIM:HEAD_DIM + 1])

        def band(jb, carry, band_masks):
            carry = list(carry)
            for a in range(nsub):
                mask = None if band_masks is None else band_masks[a]
                j = jb * nsub + a
                rows = pl.ds(pl.multiple_of(j * tk, tk), tk)
                kb = k_ref[rows, :]
                vb = v_ref[rows, :]
                dkt = jnp.zeros((LANES, tk), F32)
                dvt = jnp.zeros((LANES, tk), F32)
                for h in range(2):
                    dq, cum, pre = carry[h]
                    z = _dot_nt(qs[h], kb)
                    sp = _softplus2(z)
                    logsig = z - sp
                    if mask is not None:
                        sp = jnp.where(mask, sp, 0.0)
                    cum = cum + jnp.sum(sp, axis=1, keepdims=True)
                    av = jnp.exp2(logsig - _dot(sp.astype(BF16), later) - (tots[h] - cum))
                    if mask is not None:
                        av = jnp.where(mask, av, 0.0)
                    g = _dot_nt(dos[h], vb) * av
                    dz = g - jnp.exp2(logsig) * (pre + _dot(g.astype(BF16), upto))
                    if mask is not None:
                        dz = jnp.where(mask, dz, 0.0)
                    dz = dz.astype(BF16)
                    dkt = dkt + _dot(qts[h], dz)
                    dvt = dvt + _dot(dots[h], av.astype(BF16))
                    carry[h] = (dq + _dot(dz, kb), cum, pre + jnp.sum(g, axis=1, keepdims=True))
                dkt_acc[j] += dkt
                dvt_acc[j] += dvt
            return tuple(carry)

        carry = tuple((jnp.zeros((tq, LANES), F32), jnp.zeros((tq, 1), F32), jnp.zeros((tq, 1), F32)) for _ in range(2))
        carry = lax.fori_loop(0, i, lambda jb, cr: band(jb, cr, None), carry)
        carry = band(i, carry, masks)
        dq_ref[...] = jnp.where(first, carry[0][0], carry[1][0]) * LN2

        @pl.when(i == pl.num_programs(1) - 1)
        def _():
            def flush(j, _):
                rows = pl.ds(pl.multiple_of(j * tk, tk), tk)
                stage[0] = dkt_acc[j].T * LN2
                stage[1] = dvt_acc[j].T
                ck = pltpu.make_async_copy(stage.at[0], dk_hbm.at[hp, rows], sem.at[0])
                cv = pltpu.make_async_copy(stage.at[1], dv_hbm.at[hp, rows], sem.at[1])
                ck.start()
                cv.start()
                ck.wait()
                cv.wait()
                return 0
            lax.fori_loop(0, nkb, flush, 0)

    q_spec, k_spec, v_spec = _attn_specs(S, tq, nb)
    slab = SDS((nb, S, LANES), F32)
    return pl.pallas_call(
        body, name="attn_bwd", grid=(nb, S // tq), in_specs=[q_spec, k_spec, v_spec, q_spec, q_spec],
        out_specs=[pl.BlockSpec((None, tq, LANES), lambda hp, i: (hp, i, 0)), ANY, ANY], out_shape=[slab, slab, slab],
        scratch_shapes=[pltpu.VMEM((nkb, LANES, tk), F32), pltpu.VMEM((nkb, LANES, tk), F32), pltpu.VMEM((2, tk, LANES), F32),
                        pltpu.SemaphoreType.DMA((2,))],
        compiler_params=_params("arbitrary", "arbitrary"),
    )(qkv, qkv, qkv, do, tot)


def _trail_sum(ext, bufs, cols, levels, n):
    def src(lo, size):
        return ext[pl.ds(lo, size), cols]
    for l in range(levels):
        lo = 8 * (l + 1)
        dst = bufs[l % 2]
        dst[lo:, :] = src(lo, n - lo) + src(lo - (1 << l), n - lo)
        def src(lo_, size, d=dst):
            return d[pl.ds(lo_, size), :]
    return src(HALO, n - HALO)


def _lead_sum(ext, bufs, cols, levels, n):
    def src(lo, size):
        return ext[pl.ds(lo, size), cols]
    for l in range(levels):
        hi = n - 8 * (l + 1)
        dst = bufs[l % 2]
        dst[0:hi, :] = src(0, hi) + src(1 << l, hi)
        def src(lo_, size, d=dst):
            return d[pl.ds(lo_, size), :]
    return src(0, n - HALO)


def _pool_diffs(x_ref, xh_ref, vec_ref, ext, bufs, tm, D):
    i = pl.program_id(0)
    gs, shift = vec_ref[0:1, :], vec_ref[1:2, :]
    n, r = _rms(x_ref[...])
    nh, _ = _rms(xh_ref[...])
    ext[0:HALO, :] = jnp.where(i > 0, nh * gs + shift, 0.0)
    ext[HALO:, :] = n * gs + shift
    t = i * tm + lax.broadcasted_iota(jnp.int32, (tm, 1), 0)
    dg = D // len(POOL_LEVELS)
    out = []
    for g, lv in enumerate(POOL_LEVELS):
        cols = slice(g * dg, (g + 1) * dg)
        inv = 1.0 / jnp.minimum(t + 1, 1 << lv).astype(F32)
        out.append((_trail_sum(ext, bufs, cols, lv, HALO + tm) * inv - ext[HALO:, cols], inv))
    return out, n, r


def _pool_fwd(x, vec, pw):
    S, D = x.shape
    ng, dg, _ = pw.shape
    tm = _tile(S, TOKEN_TILE)

    def body(x_ref, xh_ref, vec_ref, pw_ref, xo_ref, ext, buf_a, buf_b):
        diffs, _, _ = _pool_diffs(x_ref, xh_ref, vec_ref, ext, (buf_a, buf_b), tm, D)
        for g, (d, _) in enumerate(diffs):
            cols = slice(g * dg, (g + 1) * dg)
            y = (_dot(d.astype(BF16), pw_ref[g]) + vec_ref[4:5, cols]) * vec_ref[3:4, cols]
            xo_ref[:, cols] = x_ref[:, cols] + vec_ref[2:3, cols] * y

    return pl.pallas_call(
        body, name="pool_fwd", grid=(S // tm,),
        in_specs=[_rows(tm, D), _prev_halo(tm, D), _resident((8, D)), _resident(pw.shape)],
        out_specs=_rows(tm, D), out_shape=SDS((S, D), F32),
        scratch_shapes=[pltpu.VMEM((HALO + tm, D), F32), pltpu.VMEM((HALO + tm, dg), F32), pltpu.VMEM((HALO + tm, dg), F32)],
        compiler_params=_params("parallel"),
    )(x, x, vec, pw)


def _pool_bwd(x, dxo, vec, pw):
    S, D = x.shape
    ng, dg, _ = pw.shape
    tm = _tile(S, TOKEN_TILE)

    def body(x_ref, xh_ref, dxo_ref, dxn_ref, vec_ref, pw_ref, dx_ref, gpw_ref, sums_ref, ext, exte, buf_a, buf_b):
        _zero_at_first_step(gpw_ref, sums_ref)
        i = pl.program_id(0)
        bufs = (buf_a, buf_b)
        diffs, n, r = _pool_diffs(x_ref, xh_ref, vec_ref, ext, bufs, tm, D)
        gate, scale = vec_ref[2:3, :], vec_ref[3:4, :]
        dxo = dxo_ref[...]
        dyp_next = jnp.where(i < pl.num_programs(0) - 1, dxn_ref[...], 0.0) * gate * scale
        t_next = (i + 1) * tm + lax.broadcasted_iota(jnp.int32, (HALO, 1), 0)
        for g, (d, inv) in enumerate(diffs):
            cols = slice(g * dg, (g + 1) * dg)
            w = pw_ref[g]
            db = d.astype(BF16)
            ypre = _dot(db, w) + vec_ref[4:5, cols]
            dy = gate[:, cols] * dxo[:, cols]
            sums_ref[2:3, cols] += jnp.sum(dxo[:, cols] * ypre * scale[:, cols], axis=0, keepdims=True)
            sums_ref[3:4, cols] += jnp.sum(dy * ypre, axis=0, keepdims=True)
            dyp = dy * scale[:, cols]
            sums_ref[4:5, cols] += jnp.sum(dyp, axis=0, keepdims=True)
            dypb = dyp.astype(BF16)
            gpw_ref[g] += _dot_tn(db, dypb)
            dd = _dot_nt(dypb, w)
            dd_next = _dot_nt(dyp_next[:, cols].astype(BF16), w)
            inv_next = 1.0 / jnp.minimum(t_next + 1, 1 << POOL_LEVELS[g]).astype(F32)
            exte[0:tm, cols] = dd * inv
            exte[tm:, cols] = dd_next * inv_next
            ext[HALO:, cols] = _lead_sum(exte, bufs, cols, POOL_LEVELS[g], tm + HALO) - dd
        dx_ref[...] = _modulate_bwd(ext[HALO:, :], n, r, vec_ref[0:1, :], dxo, sums_ref)

    return pl.pallas_call(
        body, name="pool_bwd", grid=(S // tm,),
        in_specs=[_rows(tm, D), _prev_halo(tm, D), _rows(tm, D), _next_halo(tm, D, S), _resident((8, D)), _resident(pw.shape)],
        out_specs=[_rows(tm, D), pl.BlockSpec(pw.shape, lambda i: (0, 0, 0)), _acc_spec(8, D)],
        out_shape=[SDS((S, D), F32), SDS(pw.shape, F32), SDS((8, D), F32)],
        scratch_shapes=[pltpu.VMEM((HALO + tm, D), F32), pltpu.VMEM((tm + HALO, D), F32),
                        pltpu.VMEM((HALO + tm, dg), F32), pltpu.VMEM((HALO + tm, dg), F32)],
        compiler_params=_params("arbitrary"),
    )(x, x, dxo, dxo, vec, pw)


def _loss_head(y, target):
    S, D = y.shape
    tm = _tile(S, TOKEN_TILE)

    def body(y_ref, t_ref, dy_ref, sums_ref):
        _zero_at_first_step(sums_ref)
        err = y_ref[...] - t_ref[...]
        _add_rowsum(sums_ref, 0, err * err)
        dy_ref[...] = err * (1.0 / D)

    return pl.pallas_call(
        body, name="loss_head", grid=(S // tm,), in_specs=[_rows(tm, D), _rows(tm, D)],
        out_specs=[_rows(tm, D), _acc_spec(8, D)], out_shape=[SDS((S, D), F32), SDS((8, D), F32)],
        compiler_params=_params("arbitrary"),
    )(y, target)


def _cond_pre(c_cols, cond_w):
    def body(c_ref, w_ref, o_ref):
        o_ref[...] = _hdot(c_ref[...], w_ref[...])
    return pl.pallas_call(body, name="cond_pre", out_shape=SDS((c_cols.shape[0], cond_w.shape[1]), F32))(c_cols, cond_w)


def _cond_e(parts, cond_b):
    def body(p_ref, b_ref, pre_ref, e_ref):
        pre = p_ref[0] + p_ref[2] + p_ref[4] + p_ref[6] + b_ref[...]
        pre_ref[...] = pre
        e_ref[...] = pre * jax.nn.sigmoid(pre)
    shape = SDS(parts.shape[1:], F32)
    return pl.pallas_call(body, name="cond_e", out_shape=[shape, shape])(parts, cond_b)


def _mod_cols(e, ada_w, ada_b_cols):
    L, D, nc = ada_w.shape
    B = e.shape[0]

    def body(e_ref, w_ref, b_ref, o_ref):
        o_ref[...] = _hdot(e_ref[...], w_ref[...]) + b_ref[...]

    return pl.pallas_call(
        body, name="mod_cols", grid=(L,),
        in_specs=[pl.BlockSpec((B, D), lambda l: (0, 0)), pl.BlockSpec((None, D, nc), lambda l: (l, 0, 0)),
                  pl.BlockSpec((None, 1, nc), lambda l: (l, 0, 0))],
        out_specs=pl.BlockSpec((None, B, nc), lambda l: (l, 0, 0)), out_shape=SDS((L, B, nc), F32),
        compiler_params=_params("parallel"),
    )(e, ada_w, ada_b_cols)


def _mod_bwd(e, dmod_cols, ada_w):
    L, D, nc = ada_w.shape
    B = e.shape[0]

    def body(e_ref, d_ref, w_ref, gw_ref, de_ref):
        _zero_at_first_step(de_ref)
        gw_ref[...] = _hdot(e_ref[...], d_ref[...], (((0,), (0,)), ((), ())))
        de_ref[...] += _hdot(d_ref[...], w_ref[...], (((1,), (1,)), ((), ())))

    return pl.pallas_call(
        body, name="mod_bwd", grid=(L,),
        in_specs=[pl.BlockSpec((B, D), lambda l: (0, 0)), pl.BlockSpec((None, B, nc), lambda l: (l, 0, 0)),
                  pl.BlockSpec((None, D, nc), lambda l: (l, 0, 0))],
        out_specs=[pl.BlockSpec((None, D, nc), lambda l: (l, 0, 0)), pl.BlockSpec((B, D), lambda l: (0, 0))],
        out_shape=[SDS((L, D, nc), F32), SDS((B, D), F32)], compiler_params=_params("arbitrary"),
    )(e, dmod_cols, ada_w)


def _cond_bwd(de_parts, pre, c_cols):
    def body(p_ref, pre_ref, c_ref, gw_ref, gb_ref):
        pre = pre_ref[...]
        dpre = (p_ref[0] + p_ref[2] + p_ref[4] + p_ref[6]) * _silu_grad(pre, jax.nn.sigmoid(pre))
        gb_ref[...] = jnp.sum(dpre, axis=0, keepdims=True)
        gw_ref[...] = _hdot(c_ref[...], dpre, (((0,), (0,)), ((), ())))
    D = pre.shape[1]
    return pl.pallas_call(body, name="cond_bwd", out_shape=[SDS((c_cols.shape[1], D), F32), SDS((1, D), F32)])(de_parts, pre, c_cols)


def _as_rows(a):
    return a.reshape(-1, a.shape[-1])


def _row_tile(rows, width, n_arrays):
    t = max(8, (ADAM_TILE_BYTES // (4 * width)) // 8 * 8)
    while rows % t:
        t -= 8
        if t <= 0:
            return rows
    return t


def _adamw(w, g, m, v):
    shape = w.shape
    w, g, m, v = (_as_rows(a) for a in (w, g, m, v))
    R, C = w.shape
    tr = _row_tile(R, C, 7)

    def body(w_ref, g_ref, m_ref, v_ref, d_ref, nm_ref, nv_ref):
        g = g_ref[...]
        m = ADAM_B1 * m_ref[...] + (1.0 - ADAM_B1) * g
        v = ADAM_B2 * v_ref[...] + (1.0 - ADAM_B2) * (g * g)
        m_hat = m / (1.0 - ADAM_B1 ** ADAM_STEP)
        v_hat = v / (1.0 - ADAM_B2 ** ADAM_STEP)
        d_ref[...] = -ADAM_LR * (m_hat / (jnp.sqrt(v_hat) + ADAM_EPS) + ADAM_WD * w_ref[...])
        nm_ref[...] = m
        nv_ref[...] = v

    spec = _rows(tr, C)
    outs = pl.pallas_call(
        body, name="adamw", grid=(R // tr,), in_specs=[spec] * 4, out_specs=[spec] * 3,
        out_shape=[SDS((R, C), F32)] * 3, compiler_params=_params("parallel"),
    )(w, g, m, v)
    return tuple(o.reshape(shape) for o in outs)


def _sum_lead(a, out_dtype=F32):
    n = a.shape[0]
    shape = a.shape[1:]
    a = a.reshape(n, -1, a.shape[-1])
    _, R, C = a.shape
    tr = _row_tile(R, C, n + 1)

    def body(a_ref, o_ref):
        acc = a_ref[0].astype(F32)
        for k in range(1, n):
            acc = acc + a_ref[k].astype(F32)
        o_ref[...] = acc.astype(out_dtype)

    out = pl.pallas_call(
        body, name="sum_lead", grid=(R // tr,), in_specs=[pl.BlockSpec((n, tr, C), lambda i: (0, i, 0))],
        out_specs=_rows(tr, C), out_shape=SDS((R, C), out_dtype), compiler_params=_params("parallel"),
    )(a)
    return out.reshape(shape)


def _place():
    return lax.axis_index("x"), lax.axis_index("y"), lax.axis_index("c")


def _allgather8(a):
    def body(a_ref, o_ref, send, recv, local):
        mx, my, mc = _place()
        me = 4 * mx + 2 * my + mc
        mine = pltpu.make_async_copy(a_ref, o_ref.at[me], local)
        mine.start()
        copies = []
        for k in range(1, N_DEV):
            peer = (1 - mx if k & 4 else mx, 1 - my if k & 2 else my, 1 - mc if k & 1 else mc)
            cp = pltpu.make_async_remote_copy(a_ref, o_ref.at[me], send.at[k - 1], recv.at[k - 1], device_id=peer, device_id_type=MESH)
            cp.start()
            copies.append(cp)
        for cp in copies:
            cp.wait()
        mine.wait()

    return pl.pallas_call(
        body, name="allgather8", in_specs=[ANY], out_specs=ANY, out_shape=SDS((N_DEV,) + a.shape, a.dtype),
        scratch_shapes=[pltpu.SemaphoreType.DMA((N_DEV - 1,)), pltpu.SemaphoreType.DMA((N_DEV - 1,)), pltpu.SemaphoreType.DMA],
    )(a)


def _chip_exchange(arrs, scatter, name):
    n = len(arrs)
    out_shape = [SDS((N_CHIPS,) + ((a.shape[0],) + a.shape[2:] if scatter else a.shape), a.dtype) for a in arrs]

    def body(*refs):
        ins, outs, (send, recv, local) = refs[:n], refs[n:2 * n], refs[2 * n:]
        mx, my, mc = _place()
        chip = 2 * mx + my
        copies = []
        for a in range(n):
            cp = pltpu.make_async_copy(ins[a].at[:, chip] if scatter else ins[a], outs[a].at[chip], local.at[a])
            cp.start()
            copies.append(cp)
        for k in range(1, N_CHIPS):
            px, py = (1 - mx if k & 2 else mx), (1 - my if k & 1 else my)
            for a in range(n):
                s = (k - 1) * n + a
                cp = pltpu.make_async_remote_copy(ins[a].at[:, 2 * px + py] if scatter else ins[a], outs[a].at[chip],
                                                  send.at[s], recv.at[s], device_id=(px, py, mc), device_id_type=MESH)
                cp.start()
                copies.append(cp)
        for cp in copies:
            cp.wait()

    return pl.pallas_call(
        body, name=name, in_specs=[ANY] * n, out_specs=[ANY] * n, out_shape=out_shape,
        scratch_shapes=[pltpu.SemaphoreType.DMA((3 * n,)), pltpu.SemaphoreType.DMA((3 * n,)), pltpu.SemaphoreType.DMA((n,))],
    )(*arrs)


def _sibling_send(arrs, halves, name):
    n = len(arrs)
    out_shape = [SDS(a.shape[:2] + a.shape[3:] if halves else a.shape, a.dtype) for a in arrs]

    def body(*refs):
        ins, outs, (send, recv) = refs[:n], refs[n:2 * n], refs[2 * n:]
        mx, my, mc = _place()
        copies = []
        for a in range(n):
            cp = pltpu.make_async_remote_copy(ins[a].at[:, :, 1 - mc] if halves else ins[a], outs[a], send.at[a], recv.at[a],
                                              device_id=(mx, my, 1 - mc), device_id_type=MESH)
            cp.start()
            copies.append(cp)
        for cp in copies:
            cp.wait()

    return pl.pallas_call(
        body, name=name, in_specs=[ANY] * n, out_specs=[ANY] * n, out_shape=out_shape,
        scratch_shapes=[pltpu.SemaphoreType.DMA((n,)), pltpu.SemaphoreType.DMA((n,))],
    )(*arrs)


def _add_cast(a, b, dtype):
    shape = a.shape
    a, b = _as_rows(a), _as_rows(b)
    R, C = a.shape
    tr = _row_tile(R, C, 3)

    def body(a_ref, b_ref, o_ref):
        o_ref[...] = (a_ref[...] + b_ref[...]).astype(dtype)

    out = pl.pallas_call(body, name="pair_sum", grid=(R // tr,), in_specs=[_rows(tr, C)] * 2, out_specs=_rows(tr, C),
                         out_shape=SDS((R, C), dtype), compiler_params=_params("parallel"))(a, b)
    return out.reshape(shape)


def _pack(arrs):
    flat = jnp.concatenate([a.reshape(-1).astype(F32) for a in arrs])
    pad = (-flat.shape[0]) % (8 * LANES)
    return jnp.pad(flat, (0, pad)).reshape(-1, LANES)


def _unpack(buf, shapes):
    flat = buf.reshape(buf.shape[:-2] + (-1,))
    out, off = [], 0
    for s in shapes:
        size = 1
        for d in s:
            size *= d
        out.append(flat[..., off:off + size].reshape(flat.shape[:-1] + tuple(s)))
        off += size
    return out


def _unshard(stacked, axis):
    moved = jnp.moveaxis(stacked, 0, axis)
    return moved.reshape(moved.shape[:axis] + (N_CHIPS * moved.shape[axis + 1],) + moved.shape[axis + 2:])


def _my_shard(full, axis, chip):
    size = full.shape[axis] // N_CHIPS
    return lax.dynamic_slice_in_dim(full, chip * size, size, axis)


def kernel(x, c, cond_w, cond_b, ada_w, ada_b, norm_g, ffn_w1, ffn_w3, ffn_w2, a_w_in, a_b_in, a_dw, a_dw_b, a_ln_g, a_ln_b, a_w_out, a_b_out, b_w_qkv, b_q_g, b_k_g, b_w_o, p_w, p_b, p_scale, loss_target, m_cond_w, m_cond_b, m_ada_w, m_ada_b, m_norm_g, m_ffn_w1, m_ffn_w3, m_ffn_w2, m_a_w_in, m_a_b_in, m_a_dw, m_a_dw_b, m_a_ln_g, m_a_ln_b, m_a_w_out, m_a_b_out, m_b_w_qkv, m_b_q_g, m_b_k_g, m_b_w_o, m_p_w, m_p_b, m_p_scale, v_cond_w, v_cond_b, v_ada_w, v_ada_b, v_norm_g, v_ffn_w1, v_ffn_w3, v_ffn_w2, v_a_w_in, v_a_b_in, v_a_dw, v_a_dw_b, v_a_ln_g, v_a_ln_b, v_a_w_out, v_a_b_out, v_b_w_qkv, v_b_q_g, v_b_k_g, v_b_w_o, v_p_w, v_p_b, v_p_scale):
    w_in = dict(cond_w=cond_w, cond_b=cond_b, ada_w=ada_w, ada_b=ada_b, norm_g=norm_g, ffn_w1=ffn_w1, ffn_w3=ffn_w3, ffn_w2=ffn_w2,
                a_w_in=a_w_in, a_b_in=a_b_in, a_dw=a_dw, a_dw_b=a_dw_b, a_ln_g=a_ln_g, a_ln_b=a_ln_b, a_w_out=a_w_out, a_b_out=a_b_out,
                b_w_qkv=b_w_qkv, b_q_g=b_q_g, b_k_g=b_k_g, b_w_o=b_w_o, p_w=p_w, p_b=p_b, p_scale=p_scale)
    m_in = dict(zip(WEIGHTS, (m_cond_w, m_cond_b, m_ada_w, m_ada_b, m_norm_g, m_ffn_w1, m_ffn_w3, m_ffn_w2, m_a_w_in, m_a_b_in, m_a_dw,
                              m_a_dw_b, m_a_ln_g, m_a_ln_b, m_a_w_out, m_a_b_out, m_b_w_qkv, m_b_q_g, m_b_k_g, m_b_w_o, m_p_w, m_p_b, m_p_scale)))
    v_in = dict(zip(WEIGHTS, (v_cond_w, v_cond_b, v_ada_w, v_ada_b, v_norm_g, v_ffn_w1, v_ffn_w3, v_ffn_w2, v_a_w_in, v_a_b_in, v_a_dw,
                              v_a_dw_b, v_a_ln_g, v_a_ln_b, v_a_w_out, v_a_b_out, v_b_w_qkv, v_b_q_g, v_b_k_g, v_b_w_o, v_p_w, v_p_b, v_p_scale)))
    x = x[0]
    target = loss_target[0]
    S, D = x.shape
    L = ada_w.shape[0]
    assert b_q_g.shape[-1] == HEAD_DIM and D % LANES == 0 and S % HALO == 0
    mx, my, mc = _place()
    chip = 2 * mx + my
    me = 2 * chip + mc

    big = [ffn_w1.astype(BF16), ffn_w3.astype(BF16), ffn_w2.astype(BF16), a_w_in.astype(BF16), a_w_out.astype(BF16),
           b_w_qkv.astype(BF16), b_w_o.astype(BF16)]
    sharded = [k for k, ax in SMALL.items() if ax is not None]
    gathered = _chip_exchange(big + [_pack([w_in[k] for k in sharded])], scatter=False, name="weight_gather")
    g_w1, g_w3, g_w2, g_ain, g_aout, g_qkv, g_wo, g_small = gathered
    full = {k: _unshard(a, SMALL[k]) for k, a in zip(sharded, _unpack(g_small, [w_in[k].shape for k in sharded]))}
    n_conv, n_pool = a_w_in.shape[0], p_w.shape[0]
    conv_dw = [jnp.pad(full['a_dw'][ia], ((0, 1), (0, 0))) for ia in range(n_conv)]
    pool_w = [full['p_w'][ic].astype(BF16) for ic in range(n_pool)]
    qk_scale = LOG2E * HEAD_DIM ** -0.5
    gq = jnp.tile(b_q_g[0], D // HEAD_DIM).reshape(1, D) * qk_scale
    gk = jnp.tile(b_k_g[0], D // HEAD_DIM).reshape(1, D)

    c_all = _allgather8(c)[:, 0, :]
    c_cols = _my_shard(c_all, 1, chip)
    pre, e = _cond_e(_allgather8(_cond_pre(c_cols, cond_w)), cond_b.reshape(1, D))
    nc = ada_w.shape[2]
    mod_c = _mod_cols(e, ada_w, _my_shard(ada_b, 1, chip).reshape(L, 1, nc))
    mod_all = _allgather8(mod_c.reshape(L * N_DEV, nc)).reshape(N_CHIPS, 2, L, N_DEV, nc)[:, 0]
    mod = jnp.moveaxis(lax.dynamic_index_in_dim(mod_all, me, axis=2, keepdims=False), 0, 1).reshape(L, 3, 3, D)
    shift, scale, gate = mod[:, :, 0], mod[:, :, 1], 1.0 + mod[:, :, 2]
    gains = full['norm_g']

    def mod_vec(i, k, gate_factor=1.0):
        return _vec(gains[i, k] * (1.0 + scale[i, k]), shift[i, k], gate_factor * gate[i, k])

    saved = []
    ia = ib = ic = 0
    for i in range(L):
        for k, half in ((0, 0), (1, None), (2, 1)):
            if half is not None:
                vec = mod_vec(i, k, 0.5)
                w1, w3, w2 = g_w1[:, i, half], g_w3[:, i, half], g_w2[:, i, half]
                xo, a1, a3, y = _ffn_fwd(x, vec, w1, w3, w2)
                saved.append(('ffn', i, k, half, x, vec, (a1, a3, y, w1, w3, w2)))
            elif i % 3 == 0:
                vec = mod_vec(i, k)
                w_a = g_ain[:, ia]
                w_o = g_aout[:, ia].reshape(D, D)
                p, u = _in_fwd(x, vec, w_a, full['a_b_in'][ia].reshape(1, 2 * D), True, "conv_in_fwd")
                cvec = _vec(gate[i, k], full['a_dw_b'][ia], full['a_ln_g'][ia], full['a_ln_b'][ia], full['a_b_out'][ia])
                xo, v, y = _conv_mid_fwd(u, x, cvec, conv_dw[ia], w_o)
                saved.append(('conv', i, k, ia, x, vec, (p, u, v, y, cvec, w_a, w_o)))
                ia += 1
            elif i % 3 == 1:
                vec = mod_vec(i, k)
                w_q = g_qkv[:, ib]
                w_o = g_wo[:, ib].reshape(D, D)
                raw, = _in_fwd(x, vec, w_q, jnp.zeros((1, 3 * D), F32), False, "attn_in_fwd")
                qkv = _qknorm_fwd(raw, gq, gk)
                o, tot = _attn_fwd(qkv)
                xo, y = _out_fwd(x, o, vec, w_o)
                saved.append(('attn', i, k, ib, x, vec, (raw, qkv, o, tot, y, w_q, w_o)))
                ib += 1
            else:
                vec = _vec(gains[i, k] * (1.0 + scale[i, k]), shift[i, k], gate[i, k], full['p_scale'][ic], full['p_b'][ic].reshape(D))
                xo = _pool_fwd(x, vec, pool_w[ic])
                saved.append(('pool', i, k, ic, x, vec, ()))
                ic += 1
            x = xo

    dx, sq = _loss_head(x, target)
    loss = lax.psum(0.5 / D * jnp.sum(sq[0]), ("x", "y", "c"))

    zeros_like_full = lambda k: jnp.zeros(full[k].shape, F32)
    g_full = {k: zeros_like_full(k) for k in sharded}
    g_full['b_q_g'] = jnp.zeros_like(b_q_g)
    g_full['b_k_g'] = jnp.zeros_like(b_k_g)
    dmod = jnp.zeros((L, 3, 3, D), F32)
    fc = ffn_w1.shape[-1]
    n_attn = b_w_qkv.shape[0]
    big_grads = {'ffn_w1': jnp.zeros((2 * L, N_CHIPS, D, fc), F32), 'ffn_w3': jnp.zeros((2 * L, N_CHIPS, D, fc), F32),
                 'ffn_w2': jnp.zeros((2 * L, N_CHIPS, fc, D), F32), 'a_w_in': jnp.zeros((n_conv, N_CHIPS, D, 2 * D // N_CHIPS), F32),
                 'a_w_out': jnp.zeros((n_conv, 1, D, D), F32), 'b_w_qkv': jnp.zeros((n_attn, N_CHIPS, D, 3 * D // N_CHIPS), F32),
                 'b_w_o': jnp.zeros((n_attn, 1, D, D), F32)}

    def wgrad(name, slot, a, b, a_mode, b_mode, nch):
        big_grads[name] = _wgrad(a, b, a_mode, b_mode, nch, name + "_grad", big_grads[name], slot)

    def put(name, idx, val):
        g_full[name] = g_full[name].at[idx].set(val.reshape(g_full[name][idx].shape))

    for kind, i, k, idx, xin, vec, res in reversed(saved):
        if kind == 'ffn':
            a1, a3, y, w1, w3, w2 = res
            dx, h, dy, u, da1, da3, sums = _ffn_bwd(xin, dx, a1, a3, y, vec, w1, w3, w2)
            wgrad('ffn_w1', 2 * i + idx, h, da1, 'full', 'lead', N_CHIPS)
            wgrad('ffn_w3', 2 * i + idx, h, da3, 'full', 'lead', N_CHIPS)
            wgrad('ffn_w2', 2 * i + idx, u, dy, 'lead', 'full', N_CHIPS)
            dgate = 0.5 * sums[2]
        elif kind == 'conv':
            p, u, v, y, cvec, w_a, w_o = res
            dv, q, dout, csums = _conv_mid_bwd(dx, y, v, cvec, w_o)
            wgrad('a_w_out', idx, q, dout, 'full', 'full', 1)
            du, gdw = _conv_transpose(dv, u, conv_dw[idx])
            dp, psums = _glu_bwd(du, p)
            dx, h, sums = _in_bwd(xin, dx, dp, vec, w_a, "conv_in_bwd")
            wgrad('a_w_in', idx, h, dp, 'full', 'col', N_CHIPS)
            dgate = csums[0]
            put('a_b_out', idx, csums[1])
            put('a_ln_g', idx, csums[2])
            put('a_ln_b', idx, csums[3])
            put('a_dw_b', idx, csums[4])
            put('a_dw', idx, gdw[:-1])
            put('a_b_in', idx, psums[0])
        elif kind == 'attn':
            raw, qkv, o, tot, y, w_q, w_o = res
            do, dout, osums = _out_bwd(dx, y, vec, w_o)
            wgrad('b_w_o', idx, o, dout, 'full', 'full', 1)
            dq, dk, dvv = _attn_bwd(qkv, do, tot)
            draw, qsums = _qknorm_bwd(dq, dk, dvv, raw, gq, gk)
            dx, h, sums = _in_bwd(xin, dx, draw, vec, w_q, "attn_in_bwd")
            wgrad('b_w_qkv', idx, h, draw, 'full', 'col', N_CHIPS)
            dgate = osums[2]
            put('b_q_g', idx, qk_scale * jnp.sum(qsums[0].reshape(-1, HEAD_DIM), axis=0))
            put('b_k_g', idx, jnp.sum(qsums[1].reshape(-1, HEAD_DIM), axis=0))
        else:
            dx, gpw, sums = _pool_bwd(xin, dx, vec, pool_w[idx])
            dgate = sums[2]
            put('p_w', idx, gpw)
            put('p_scale', idx, sums[3])
            put('p_b', idx, sums[4])
        put('norm_g', (i, k), sums[0] * (1.0 + scale[i, k]))
        dmod = dmod.at[i, k].set(jnp.stack([sums[1], sums[0] * gains[i, k], dgate]))
    grad_x = dx[None]

    dmod_all = _allgather8(dmod.reshape(L, 9 * D))
    g_ada_b = _sum_lead(dmod_all)
    dmod_cols = jnp.moveaxis(_my_shard(dmod_all, 2, chip), 0, 1)
    g_ada_w, de_part = _mod_bwd(e, dmod_cols, ada_w)
    g_cond_w, g_cond_b = _cond_bwd(_allgather8(de_part), pre, c_cols)

    small_names = list(SMALL)
    g_full['cond_b'] = g_cond_b.reshape(D)
    g_full['ada_b'] = g_ada_b
    reduced = [k for k in small_names if k not in ('cond_b', 'ada_b')]
    red = _unpack(_sum_lead(_allgather8(_pack([g_full[k] for k in reduced]))), [g_full[k].shape for k in reduced])
    for k, a in zip(reduced, red):
        g_full[k] = a
    grads = {k: (g_full[k] if SMALL[k] is None else _my_shard(g_full[k], SMALL[k], chip)) for k in small_names}
    grads['cond_w'] = g_cond_w
    grads['ada_w'] = g_ada_w

    big_names = list(big_grads)
    by_half = []
    for name in big_names:
        g = big_grads[name]
        rows = g.shape[1] * g.shape[2] // N_CHIPS
        by_half.append(g.reshape(g.shape[0], N_CHIPS, 2, rows // 2, g.shape[3]))
    got = _sibling_send(by_half, True, "pair_send")
    pair = [_add_cast(lax.dynamic_index_in_dim(a, mc, axis=2, keepdims=False), b, BF16) for a, b in zip(by_half, got)]
    mine = [_sum_lead(a) for a in _chip_exchange(pair, scatter=True, name="grad_scatter")]
    theirs = _sibling_send(mine, False, "pair_return")
    for name, a, b in zip(big_names, mine, theirs):
        both = jnp.where(mc == 0, jnp.stack([a, b], axis=1), jnp.stack([b, a], axis=1))
        grads[name] = both.reshape(w_in[name].shape)

    delta, new_m, new_v = {}, {}, {}
    packed = [_pack([d[k] for k in small_names]) for d in (w_in, grads, m_in, v_in)]
    shapes = [w_in[k].shape for k in small_names]
    for out, buf in zip((delta, new_m, new_v), _adamw(*packed)):
        out.update(zip(small_names, _unpack(buf, shapes)))
    for k in WEIGHTS:
        if k not in SMALL:
            delta[k], new_m[k], new_v[k] = _adamw(w_in[k], grads[k], m_in[k], v_in[k])
    return (loss, grad_x, *[grads[k] for k in WEIGHTS], *[delta[k] for k in WEIGHTS], *[new_m[k] for k in WEIGHTS],
            *[new_v[k] for k in WEIGHTS])
```

```python
import functools

import jax
import jax.numpy as jnp
from jax import lax
from jax.experimental import pallas as pl
from jax.experimental.pallas import tpu as pltpu

F32 = jnp.float32
BF16 = jnp.bfloat16
SDS = jax.ShapeDtypeStruct
MESH = pl.DeviceIdType.MESH
ANY = pl.BlockSpec(memory_space=pl.ANY)

EPS = 1e-6
N_CHIPS = 4
N_DEV = 8
LANES = 128
HEAD_DIM = 64
VMEM_LIMIT_BYTES = 56 * 2**20
TOKEN_TILE = 512
WGRAD_TILE = 1024
ATTN_Q_TILE = 1024
ATTN_K_TILE = 256
ATTN_DEAD_BITS = 160.0
LOG2E = 1.4426950408889634
LN2 = 0.6931471805599453
HALO = 32
ADAM_TILE_BYTES = 1 << 20
POOL_LEVELS = (1, 2, 3, 4)

ADAM_LR, ADAM_B1, ADAM_B2, ADAM_EPS, ADAM_WD, ADAM_STEP = 0.001, 0.9, 0.999, 1e-08, 0.01, 10

WEIGHTS = ['cond_w', 'cond_b', 'ada_w', 'ada_b', 'norm_g', 'ffn_w1', 'ffn_w3', 'ffn_w2', 'a_w_in', 'a_b_in', 'a_dw',
           'a_dw_b', 'a_ln_g', 'a_ln_b', 'a_w_out', 'a_b_out', 'b_w_qkv', 'b_q_g', 'b_k_g', 'b_w_o', 'p_w', 'p_b', 'p_scale']
SMALL = {'norm_g': 2, 'a_b_in': 1, 'a_dw': 2, 'a_dw_b': 1, 'a_ln_g': 1, 'a_ln_b': 1, 'a_b_out': 1, 'p_w': 2, 'p_b': 2,
         'p_scale': 1, 'cond_b': None, 'ada_b': None, 'b_q_g': None, 'b_k_g': None}


def _tile(n, pref):
    return pref if n % pref == 0 else n


def _params(*sem):
    return pltpu.CompilerParams(dimension_semantics=sem, vmem_limit_bytes=VMEM_LIMIT_BYTES)


def _resident(shape):
    nd = len(shape)
    return pl.BlockSpec(shape, lambda *_: (0,) * nd, pipeline_mode=pl.Buffered(1))


def _rows(tm, width):
    return pl.BlockSpec((tm, width), lambda i: (i, 0))


def _acc_spec(rows, width):
    return pl.BlockSpec((rows, width), lambda i: (0, 0))


def _dot(a, b):
    return jnp.dot(a, b, preferred_element_type=F32)


def _dot_nt(a, b):
    return lax.dot_general(a, b, (((1,), (1,)), ((), ())), preferred_element_type=F32)


def _dot_tn(a, b):
    return lax.dot_general(a, b, (((0,), (0,)), ((), ())), preferred_element_type=F32)


def _hdot(a, b, dims=(((1,), (0,)), ((), ()))):
    return lax.dot_general(a, b, dims, preferred_element_type=F32, precision=lax.Precision.HIGHEST)


def _zero_at_first_step(*refs):
    @pl.when(pl.program_id(0) == 0)
    def _():
        for r in refs:
            r[...] = jnp.zeros_like(r)


def _add_rowsum(ref, row, t):
    ref[row:row + 1, :] += jnp.sum(t, axis=0, keepdims=True)


def _rms(x):
    r = lax.rsqrt(jnp.mean(x * x, axis=-1, keepdims=True) + EPS)
    return x * r, r


def _modulate_bwd(dh, n, r, gs, dxo, sums_ref):
    _add_rowsum(sums_ref, 0, dh * n)
    _add_rowsum(sums_ref, 1, dh)
    dn = dh * gs
    return dxo + r * (dn - n * jnp.mean(dn * n, axis=-1, keepdims=True))


def _silu_grad(a, sg):
    return sg * (1.0 + a * (1.0 - sg))


def _vec(*rows):
    d = rows[0].shape[-1]
    rows = [r.reshape(1, d).astype(F32) for r in rows]
    return jnp.concatenate(rows + [jnp.zeros((8 - len(rows), d), F32)], axis=0)


def _ffn_fwd(x, vec, w1, w3, w2):
    S, D = x.shape
    nch, _, fc = w1.shape
    tm = _tile(S, TOKEN_TILE)

    def body(x_ref, vec_ref, w1_ref, w3_ref, w2_ref, xo_ref, a1_ref, a3_ref, y_ref):
        x = x_ref[...]
        n, _ = _rms(x)
        h = (n * vec_ref[0:1, :] + vec_ref[1:2, :]).astype(BF16)
        acc = jnp.zeros((tm, D), F32)
        for j in range(nch):
            a1 = _dot(h, w1_ref[j]).astype(BF16)
            a3 = _dot(h, w3_ref[j]).astype(BF16)
            a1_ref[j] = a1
            a3_ref[j] = a3
            a1 = a1.astype(F32)
            u = a1 * jax.nn.sigmoid(a1) * a3.astype(F32)
            acc = acc + _dot(u.astype(BF16), w2_ref[j])
        y_ref[...] = acc.astype(BF16)
        xo_ref[...] = x + vec_ref[2:3, :] * acc

    chunked = pl.BlockSpec((nch, tm, fc), lambda i: (0, i, 0))
    return pl.pallas_call(
        body, name="ffn_fwd", grid=(S // tm,),
        in_specs=[_rows(tm, D), _resident((8, D)), _resident(w1.shape), _resident(w3.shape), _resident(w2.shape)],
        out_specs=[_rows(tm, D), chunked, chunked, _rows(tm, D)],
        out_shape=[SDS((S, D), F32), SDS((nch, S, fc), BF16), SDS((nch, S, fc), BF16), SDS((S, D), BF16)],
        compiler_params=_params("parallel"),
    )(x, vec, w1, w3, w2)


def _ffn_bwd(x, dxo, a1, a3, y, vec, w1, w3, w2):
    S, D = x.shape
    nch, _, fc = w1.shape
    tm = _tile(S, TOKEN_TILE // 2)

    def body(x_ref, dxo_ref, a1_ref, a3_ref, y_ref, vec_ref, w1_ref, w3_ref, w2_ref,
             dx_ref, h_ref, dy_ref, u_ref, da1_ref, da3_ref, sums_ref):
        _zero_at_first_step(sums_ref)
        x = x_ref[...]
        dxo = dxo_ref[...]
        gs = vec_ref[0:1, :]
        n, r = _rms(x)
        h_ref[...] = (n * gs + vec_ref[1:2, :]).astype(BF16)
        _add_rowsum(sums_ref, 2, dxo * y_ref[...].astype(F32))
        dy = (vec_ref[2:3, :] * dxo).astype(BF16)
        dy_ref[...] = dy
        dh = jnp.zeros((tm, D), F32)
        for j in range(nch):
            a1 = a1_ref[j].astype(F32)
            a3 = a3_ref[j].astype(F32)
            sg = jax.nn.sigmoid(a1)
            s = a1 * sg
            du = _dot_nt(dy, w2_ref[j])
            da1 = (du * a3 * _silu_grad(a1, sg)).astype(BF16)
            da3 = (du * s).astype(BF16)
            u_ref[j] = (s * a3).astype(BF16)
            da1_ref[j] = da1
            da3_ref[j] = da3
            dh = dh + _dot_nt(da1, w1_ref[j]) + _dot_nt(da3, w3_ref[j])
        dx_ref[...] = _modulate_bwd(dh, n, r, gs, dxo, sums_ref)

    chunked = pl.BlockSpec((nch, tm, fc), lambda i: (0, i, 0))
    return pl.pallas_call(
        body, name="ffn_bwd", grid=(S // tm,),
        in_specs=[_rows(tm, D), _rows(tm, D), chunked, chunked, _rows(tm, D), _resident((8, D)),
                  _resident(w1.shape), _resident(w3.shape), _resident(w2.shape)],
        out_specs=[_rows(tm, D), _rows(tm, D), _rows(tm, D), chunked, chunked, chunked, _acc_spec(8, D)],
        out_shape=[SDS((S, D), F32), SDS((S, D), BF16), SDS((S, D), BF16), SDS((nch, S, fc), BF16),
                   SDS((nch, S, fc), BF16), SDS((nch, S, fc), BF16), SDS((8, D), F32)],
        compiler_params=_params("arbitrary"),
    )(x, dxo, a1, a3, y, vec, w1, w3, w2)


def _wgrad(a, b, a_mode, b_mode, nch, name, acc, slot):
    S = a.shape[-2]
    M = a.shape[-1]
    N = b.shape[-1] // (nch if b_mode == 'col' else 1)
    assert acc.shape[1:] == (nch, M, N)
    ts = _tile(S, WGRAD_TILE)

    def spec(mode, width):
        if mode == 'full':
            return pl.BlockSpec((ts, width), lambda j, s: (s, 0))
        if mode == 'lead':
            return pl.BlockSpec((None, ts, width), lambda j, s: (j, s, 0))
        return pl.BlockSpec((ts, width), lambda j, s: (s, j))

    def body(a_ref, b_ref, acc_ref, o_ref):
        @pl.when(pl.program_id(1) == 0)
        def _():
            o_ref[...] = jnp.zeros_like(o_ref)
        o_ref[...] += _dot_tn(a_ref[...], b_ref[...])

    return pl.pallas_call(
        body, name=name, grid=(nch, S // ts), in_specs=[spec(a_mode, M), spec(b_mode, N), ANY],
        out_specs=pl.BlockSpec((None, None, M, N), lambda j, s: (slot, j, 0, 0)), out_shape=SDS(acc.shape, F32),
        input_output_aliases={2: 0}, compiler_params=_params("parallel", "arbitrary"),
    )(a, b, acc)


def _in_fwd(x, vec, w, bias, glu, name):
    S, D = x.shape
    nch, _, nc = w.shape
    N = nch * nc
    tm = _tile(S, TOKEN_TILE)

    def body(x_ref, vec_ref, w_ref, b_ref, p_ref, *u_ref):
        n, _ = _rms(x_ref[...])
        h = (n * vec_ref[0:1, :] + vec_ref[1:2, :]).astype(BF16)
        for j in range(nch):
            cols = slice(j * nc, (j + 1) * nc)
            p_ref[:, cols] = (_dot(h, w_ref[j]) + b_ref[:, cols]).astype(p_ref.dtype)
        if glu:
            half = N // 2
            u_ref[0][...] = p_ref[:, :half].astype(F32) * jax.nn.sigmoid(p_ref[:, half:].astype(F32))

    out_specs = [_rows(tm, N)] + ([_rows(tm, N // 2)] if glu else [])
    out_shape = [SDS((S, N), BF16 if glu else F32)] + ([SDS((S, N // 2), F32)] if glu else [])
    return pl.pallas_call(
        body, name=name, grid=(S // tm,),
        in_specs=[_rows(tm, D), _resident((8, D)), _resident(w.shape), _resident((1, N))],
        out_specs=out_specs, out_shape=out_shape, compiler_params=_params("parallel"),
    )(x, vec, w, bias)


def _in_bwd(x, dxo, dp, vec, w, name):
    S, D = x.shape
    nch, _, nc = w.shape
    tm = _tile(S, TOKEN_TILE)

    def body(x_ref, dxo_ref, dp_ref, vec_ref, w_ref, dx_ref, h_ref, sums_ref):
        _zero_at_first_step(sums_ref)
        gs = vec_ref[0:1, :]
        n, r = _rms(x_ref[...])
        h_ref[...] = (n * gs + vec_ref[1:2, :]).astype(BF16)
        dh = jnp.zeros((tm, D), F32)
        for j in range(nch):
            dh = dh + _dot_nt(dp_ref[:, j * nc:(j + 1) * nc], w_ref[j])
        dx_ref[...] = _modulate_bwd(dh, n, r, gs, dxo_ref[...], sums_ref)

    return pl.pallas_call(
        body, name=name, grid=(S // tm,),
        in_specs=[_rows(tm, D), _rows(tm, D), _rows(tm, nch * nc), _resident((8, D)), _resident(w.shape)],
        out_specs=[_rows(tm, D), _rows(tm, D), _acc_spec(8, D)],
        out_shape=[SDS((S, D), F32), SDS((S, D), BF16), SDS((8, D), F32)],
        compiler_params=_params("arbitrary"),
    )(x, dxo, dp, vec, w)


def _out_fwd(x, t, vec, w):
    S, D = x.shape
    tm = _tile(S, TOKEN_TILE)

    def body(x_ref, t_ref, vec_ref, w_ref, xo_ref, y_ref):
        y = _dot(t_ref[...], w_ref[...])
        y_ref[...] = y.astype(BF16)
        xo_ref[...] = x_ref[...] + vec_ref[2:3, :] * y

    return pl.pallas_call(
        body, name="attn_out_fwd", grid=(S // tm,),
        in_specs=[_rows(tm, D), _rows(tm, t.shape[1]), _resident((8, D)), _resident(w.shape)],
        out_specs=[_rows(tm, D), _rows(tm, D)], out_shape=[SDS((S, D), F32), SDS((S, D), BF16)],
        compiler_params=_params("parallel"),
    )(x, t, vec, w)


def _out_bwd(dxo, y, vec, w):
    S, D = dxo.shape
    K = w.shape[0]
    tm = _tile(S, TOKEN_TILE)

    def body(dxo_ref, y_ref, vec_ref, w_ref, dt_ref, dout_ref, sums_ref):
        _zero_at_first_step(sums_ref)
        dxo = dxo_ref[...]
        _add_rowsum(sums_ref, 2, dxo * y_ref[...].astype(F32))
        dout = (vec_ref[2:3, :] * dxo).astype(BF16)
        dout_ref[...] = dout
        dt_ref[...] = _dot_nt(dout, w_ref[...]).astype(BF16)

    return pl.pallas_call(
        body, name="attn_out_bwd", grid=(S // tm,),
        in_specs=[_rows(tm, D), _rows(tm, D), _resident((8, D)), _resident(w.shape)],
        out_specs=[_rows(tm, K), _rows(tm, D), _acc_spec(8, D)],
        out_shape=[SDS((S, K), BF16), SDS((S, D), BF16), SDS((8, D), F32)],
        compiler_params=_params("arbitrary"),
    )(dxo, y, vec, w)


def _prev_halo(tm, width):
    return pl.BlockSpec((HALO, width), lambda i: (jnp.maximum(i * (tm // HALO) - 1, 0), 0))


def _next_halo(tm, width, n_rows):
    last = n_rows // HALO - 1
    return pl.BlockSpec((HALO, width), lambda i: (jnp.minimum((i + 1) * (tm // HALO), last), 0))


def _layer_norm(v, g, b):
    mu = jnp.mean(v, axis=-1, keepdims=True)
    vc = v - mu
    rstd = lax.rsqrt(jnp.mean(vc * vc, axis=-1, keepdims=True) + EPS)
    vh = vc * rstd
    return vh * g + b, vh, rstd


def _conv_mid_fwd(u, x, vec, dw, w_out):
    S, D = x.shape
    taps = dw.shape[0] - 1
    tm = _tile(S, TOKEN_TILE // 2)

    def body(u_ref, uh_ref, x_ref, vec_ref, dw_ref, w_ref, xo_ref, v_ref, y_ref, ext):
        ext[0:HALO, :] = jnp.where(pl.program_id(0) > 0, uh_ref[...], 0.0)
        ext[HALO:, :] = u_ref[...]
        v = jnp.zeros((tm, D), F32) + vec_ref[1:2, :]
        for k in range(taps):
            v = v + dw_ref[k:k + 1, :] * ext[pl.ds(HALO - (taps - 1) + k, tm), :]
        v_ref[...] = v
        l, _, _ = _layer_norm(v, vec_ref[2:3, :], vec_ref[3:4, :])
        q = (l * jax.nn.sigmoid(l)).astype(BF16)
        y = _dot(q, w_ref[...]) + vec_ref[4:5, :]
        y_ref[...] = y.astype(BF16)
        xo_ref[...] = x_ref[...] + vec_ref[0:1, :] * y

    return pl.pallas_call(
        body, name="conv_mid_fwd", grid=(S // tm,),
        in_specs=[_rows(tm, D), _prev_halo(tm, D), _rows(tm, D), _resident((8, D)), _resident(dw.shape), _resident(w_out.shape)],
        out_specs=[_rows(tm, D), _rows(tm, D), _rows(tm, D)],
        out_shape=[SDS((S, D), F32), SDS((S, D), F32), SDS((S, D), BF16)],
        scratch_shapes=[pltpu.VMEM((HALO + tm, D), F32)],
        compiler_params=_params("parallel"),
    )(u, u, x, vec, dw, w_out)


def _conv_mid_bwd(dxo, y, v, vec, w_out):
    S, D = dxo.shape
    tm = _tile(S, TOKEN_TILE)

    def body(dxo_ref, y_ref, v_ref, vec_ref, w_ref, dv_ref, q_ref, dout_ref, sums_ref):
        _zero_at_first_step(sums_ref)
        dxo = dxo_ref[...]
        _add_rowsum(sums_ref, 0, dxo * y_ref[...].astype(F32))
        dout = vec_ref[0:1, :] * dxo
        _add_rowsum(sums_ref, 1, dout)
        dout = dout.astype(BF16)
        dout_ref[...] = dout
        ln_g = vec_ref[2:3, :]
        l, vh, rstd = _layer_norm(v_ref[...], ln_g, vec_ref[3:4, :])
        sg = jax.nn.sigmoid(l)
        q_ref[...] = (l * sg).astype(BF16)
        dl = _dot_nt(dout, w_ref[...]) * _silu_grad(l, sg)
        _add_rowsum(sums_ref, 2, dl * vh)
        _add_rowsum(sums_ref, 3, dl)
        dvh = dl * ln_g
        dv = rstd * (dvh - jnp.mean(dvh, axis=-1, keepdims=True) - vh * jnp.mean(dvh * vh, axis=-1, keepdims=True))
        _add_rowsum(sums_ref, 4, dv)
        dv_ref[...] = dv

    return pl.pallas_call(
        body, name="conv_mid_bwd", grid=(S // tm,),
        in_specs=[_rows(tm, D), _rows(tm, D), _rows(tm, D), _resident((8, D)), _resident(w_out.shape)],
        out_specs=[_rows(tm, D), _rows(tm, D), _rows(tm, D), _acc_spec(8, D)],
        out_shape=[SDS((S, D), F32), SDS((S, D), BF16), SDS((S, D), BF16), SDS((8, D), F32)],
        compiler_params=_params("arbitrary"),
    )(dxo, y, v, vec, w_out)


def _conv_transpose(dv, u, dw):
    S, D = dv.shape
    taps = dw.shape[0] - 1
    tm = _tile(S, TOKEN_TILE // 2)

    def body(dv_ref, dvn_ref, u_ref, uh_ref, dw_ref, du_ref, gdw_ref, extv, extu):
        _zero_at_first_step(gdw_ref)
        i = pl.program_id(0)
        dv = dv_ref[...]
        extv[0:tm, :] = dv
        extv[tm:, :] = jnp.where(i < pl.num_programs(0) - 1, dvn_ref[...], 0.0)
        extu[0:HALO, :] = jnp.where(i > 0, uh_ref[...], 0.0)
        extu[HALO:, :] = u_ref[...]
        du = jnp.zeros((tm, D), F32)
        for k in range(taps):
            du = du + dw_ref[k:k + 1, :] * extv[pl.ds(taps - 1 - k, tm), :]
            _add_rowsum(gdw_ref, k, dv * extu[pl.ds(HALO - (taps - 1) + k, tm), :])
        du_ref[...] = du

    return pl.pallas_call(
        body, name="conv_transpose", grid=(S // tm,),
        in_specs=[_rows(tm, D), _next_halo(tm, D, S), _rows(tm, D), _prev_halo(tm, D), _resident(dw.shape)],
        out_specs=[_rows(tm, D), _acc_spec(dw.shape[0], D)],
        out_shape=[SDS((S, D), F32), SDS(dw.shape, F32)],
        scratch_shapes=[pltpu.VMEM((tm + HALO, D), F32), pltpu.VMEM((HALO + tm, D), F32)],
        compiler_params=_params("arbitrary"),
    )(dv, dv, u, u, dw)


def _glu_bwd(du, p):
    S, D = du.shape
    tm = _tile(S, TOKEN_TILE)

    def body(du_ref, p_ref, dp_ref, sums_ref):
        _zero_at_first_step(sums_ref)
        du = du_ref[...]
        a = p_ref[:, :D].astype(F32)
        sb = jax.nn.sigmoid(p_ref[:, D:].astype(F32))
        da = du * sb
        db = du * a * sb * (1.0 - sb)
        dp_ref[:, :D] = da.astype(BF16)
        dp_ref[:, D:] = db.astype(BF16)
        sums_ref[0:1, :D] += jnp.sum(da, axis=0, keepdims=True)
        sums_ref[0:1, D:] += jnp.sum(db, axis=0, keepdims=True)

    return pl.pallas_call(
        body, name="glu_bwd", grid=(S // tm,), in_specs=[_rows(tm, D), _rows(tm, 2 * D)],
        out_specs=[_rows(tm, 2 * D), _acc_spec(8, 2 * D)], out_shape=[SDS((S, 2 * D), BF16), SDS((8, 2 * D), F32)],
        compiler_params=_params("arbitrary"),
    )(du, p)


def _head_mean(t, bd):
    hi = t.astype(BF16)
    lo = (t - hi.astype(F32)).astype(BF16)
    return (_dot(hi, bd) + _dot(lo, bd)) * (1.0 / HEAD_DIM)


def _head_blocks():
    r = lax.broadcasted_iota(jnp.int32, (LANES, LANES), 0) // HEAD_DIM
    c = lax.broadcasted_iota(jnp.int32, (LANES, LANES), 1) // HEAD_DIM
    return (r == c).astype(BF16)


def _qknorm_fwd(raw, gq, gk):
    S, D3 = raw.shape
    D = D3 // 3
    tm = _tile(S, TOKEN_TILE)

    def body(raw_ref, gq_ref, gk_ref, o_ref):
        bd = _head_blocks()
        for off, g_ref in ((0, gq_ref), (D, gk_ref)):
            for c in range(D // LANES):
                cols = slice(off + c * LANES, off + (c + 1) * LANES)
                xs = raw_ref[:, cols]
                r = lax.rsqrt(_head_mean(xs * xs, bd) + EPS)
                o_ref[:, cols] = (xs * r * g_ref[:, c * LANES:(c + 1) * LANES]).astype(BF16)
        o_ref[:, 2 * D:] = raw_ref[:, 2 * D:].astype(BF16)

    return pl.pallas_call(
        body, name="qknorm_fwd", grid=(S // tm,), in_specs=[_rows(tm, D3), _resident((1, D)), _resident((1, D))],
        out_specs=_rows(tm, D3), out_shape=SDS((S, D3), BF16), compiler_params=_params("parallel"),
    )(raw, gq, gk)


def _qknorm_bwd(dq, dk, dv, raw, gq, gk):
    S, D3 = raw.shape
    D = D3 // 3
    nb = D // LANES
    tm = _tile(S, TOKEN_TILE)

    def body(dq_ref, dk_ref, dv_ref, raw_ref, gq_ref, gk_ref, o_ref, sums_ref):
        _zero_at_first_step(sums_ref)
        bd = _head_blocks()
        for row, (off, g_ref, d_ref) in enumerate(((0, gq_ref, dq_ref), (D, gk_ref, dk_ref))):
            for c in range(nb):
                lanes = slice(c * LANES, (c + 1) * LANES)
                cols = slice(off + c * LANES, off + (c + 1) * LANES)
                xs = raw_ref[:, cols]
                r = lax.rsqrt(_head_mean(xs * xs, bd) + EPS)
                n = xs * r
                dhat = d_ref[c]
                sums_ref[row:row + 1, lanes] += jnp.sum(dhat * n, axis=0, keepdims=True)
                dn = dhat * g_ref[:, lanes]
                o_ref[:, cols] = (r * (dn - n * _head_mean(dn * n, bd))).astype(BF16)
        for c in range(nb):
            o_ref[:, 2 * D + c * LANES:2 * D + (c + 1) * LANES] = dv_ref[c].astype(BF16)

    tiles = pl.BlockSpec((nb, tm, LANES), lambda i: (0, i, 0))
    return pl.pallas_call(
        body, name="qknorm_bwd", grid=(S // tm,),
        in_specs=[tiles, tiles, tiles, _rows(tm, D3), _resident((1, D)), _resident((1, D))],
        out_specs=[_rows(tm, D3), _acc_spec(8, D)], out_shape=[SDS((S, D3), BF16), SDS((8, D), F32)],
        compiler_params=_params("arbitrary"),
    )(dq, dk, dv, raw, gq, gk)


def _softplus2(z):
    return jnp.maximum(z, jnp.log2(1.0 + jnp.exp2(jnp.minimum(z, 30.0))))


def _attn_tiles(S):
    tq = _tile(S, ATTN_Q_TILE)
    tk = _tile(tq, ATTN_K_TILE)
    return tq, tk


def _attn_consts(tq, tk):
    lane = lax.broadcasted_iota(jnp.int32, (1, LANES), 1)
    first = lane < HEAD_DIM
    r = lax.broadcasted_iota(jnp.int32, (tq, tk), 0)
    c = lax.broadcasted_iota(jnp.int32, (tq, tk), 1)
    kr = lax.broadcasted_iota(jnp.int32, (tk, tk), 0)
    kc = lax.broadcasted_iota(jnp.int32, (tk, tk), 1)
    masks = [(a * tk + c) < r for a in range(tq // tk)]
    later = (kr > kc).astype(BF16)
    upto = (kr <= kc).astype(BF16)
    return first, masks, later, upto


def _attn_specs(S, tq, nb):
    q_spec = pl.BlockSpec((tq, LANES), lambda hp, i: (i, hp))
    k_spec = pl.BlockSpec((S, LANES), lambda hp, i: (0, nb + hp), pipeline_mode=pl.Buffered(1))
    v_spec = pl.BlockSpec((S, LANES), lambda hp, i: (0, 2 * nb + hp), pipeline_mode=pl.Buffered(1))
    return q_spec, k_spec, v_spec


def _attn_fwd(qkv):
    S, D3 = qkv.shape
    D = D3 // 3
    nb = D // LANES
    tq, tk = _attn_tiles(S)
    nsub = tq // tk

    def body(q_ref, k_ref, v_ref, o_ref, tot_ref, start_ref):
        hp = pl.program_id(0)
        i = pl.program_id(1)
        first, masks, later, _ = _attn_consts(tq, tk)
        q = q_ref[...]
        qs = (jnp.where(first, q, jnp.zeros_like(q)), jnp.where(first, jnp.zeros_like(q), q))

        def tiles(t0, n, carry, tile_masks):
            out = []
            for h in range(2):
                o, c = carry[h]
                for a in reversed(range(n)):
                    mask = None if tile_masks is None else tile_masks[a]
                    rows = pl.ds(pl.multiple_of((t0 + a) * tk, tk), tk)
                    kb = k_ref[rows, :]
                    z = _dot_nt(qs[h], kb)
                    sp = _softplus2(z)
                    logsig = z - sp
                    if mask is not None:
                        sp = jnp.where(mask, sp, 0.0)
                    av = jnp.exp2(logsig - _dot(sp.astype(BF16), later) - c)
                    if mask is not None:
                        av = jnp.where(mask, av, 0.0)
                    o = o + _dot(av.astype(BF16), v_ref[rows, :])
                    c = c + jnp.sum(sp, axis=1, keepdims=True)
                out.append((o, c))
            return tuple(out)

        def live(carry):
            return jnp.minimum(jnp.min(carry[0][1]), jnp.min(carry[1][1])) < ATTN_DEAD_BITS

        carry = tuple((jnp.zeros((tq, LANES), F32), jnp.zeros((tq, 1), F32)) for _ in range(2))
        carry = tiles(i * nsub, nsub, carry, masks)
        band_floor = jnp.maximum((i - 1) * nsub, 0)
        t, carry = lax.while_loop(lambda st: (st[0] >= band_floor) & (st[0] >= 0) & live(st[1]),
                                  lambda st: (st[0] - 1, tiles(st[0], 1, st[1], None)), (i * nsub - 1, carry))
        t, carry = lax.while_loop(lambda st: (st[0] >= nsub - 1) & live(st[1]),
                                  lambda st: (st[0] - nsub, tiles(st[0] - (nsub - 1), nsub, st[1], None)), (t, carry))
        (o_a, c_a), (o_b, c_b) = carry
        o_ref[...] = jnp.where(first, o_a, o_b).astype(BF16)
        tot_ref[...] = jnp.where(first, c_a, c_b)
        start_ref[hp, i] = t + 1

    q_spec, k_spec, v_spec = _attn_specs(S, tq, nb)
    return pl.pallas_call(
        body, name="attn_fwd", grid=(nb, S // tq), in_specs=[q_spec, k_spec, v_spec],
        out_specs=[q_spec, q_spec, pl.BlockSpec(memory_space=pltpu.SMEM)],
        out_shape=[SDS((S, D), BF16), SDS((S, D), F32), SDS((nb, S // tq), jnp.int32)],
        compiler_params=_params("arbitrary", "arbitrary"),
    )(qkv, qkv, qkv)


def _attn_bwd(qkv, do, tot, start):
    S, D3 = qkv.shape
    D = D3 // 3
    nb = D // LANES
    tq, tk = _attn_tiles(S)
    nsub = tq // tk
    nkb = S // tk

    def body(start_ref, q_ref, k_ref, v_ref, do_ref, tot_ref, dq_ref, dk_hbm, dv_hbm, dkt_acc, dvt_acc, stage, sem):
        hp = pl.program_id(0)
        i = pl.program_id(1)

        @pl.when(i == 0)
        def _():
            dkt_acc[...] = jnp.zeros_like(dkt_acc)
            dvt_acc[...] = jnp.zeros_like(dvt_acc)

        first, masks, later, upto = _attn_consts(tq, tk)
        q = q_ref[...]
        do = do_ref[...]
        zero = jnp.zeros_like(q)
        qs = (jnp.where(first, q, zero), jnp.where(first, zero, q))
        dos = (jnp.where(first, do, zero), jnp.where(first, zero, do))
        qt = q.astype(F32).T.astype(BF16)
        dot_ = do.astype(F32).T.astype(BF16)
        qts = (qt[:HEAD_DIM], qt[HEAD_DIM:])
        dots = (dot_[:HEAD_DIM], dot_[HEAD_DIM:])
        tots = (tot_ref[:, 0:1], tot_ref[:, HEAD_DIM:HEAD_DIM + 1])

        def tiles(t0, n, carry, tile_masks):
            carry = list(carry)
            for a in range(n):
                mask = None if tile_masks is None else tile_masks[a]
                j = t0 + a
                rows = pl.ds(pl.multiple_of(j * tk, tk), tk)
                kb = k_ref[rows, :]
                vb = v_ref[rows, :]
                dkts, dvts = [], []
                for h in range(2):
                    dq, cum, pre = carry[h]
                    z = _dot_nt(qs[h], kb)
                    sp = _softplus2(z)
                    logsig = z - sp
                    if mask is not None:
                        sp = jnp.where(mask, sp, 0.0)
                    cum = cum + jnp.sum(sp, axis=1, keepdims=True)
                    av = jnp.exp2(logsig - _dot(sp.astype(BF16), later) - (tots[h] - cum))
                    if mask is not None:
                        av = jnp.where(mask, av, 0.0)
                    g = _dot_nt(dos[h], vb) * av
                    dz = g - jnp.exp2(logsig) * (pre + _dot(g.astype(BF16), upto))
                    if mask is not None:
                        dz = jnp.where(mask, dz, 0.0)
                    dz = dz.astype(BF16)
                    dkts.append(_dot(qts[h], dz))
                    dvts.append(_dot(dots[h], av.astype(BF16)))
                    carry[h] = (dq + _dot(dz, kb), cum, pre + jnp.sum(g, axis=1, keepdims=True))
                dkt_acc[j] += jnp.concatenate(dkts, axis=0)
                dvt_acc[j] += jnp.concatenate(dvts, axis=0)
            return tuple(carry)

        carry = tuple((jnp.zeros((tq, LANES), F32), jnp.zeros((tq, 1), F32), jnp.zeros((tq, 1), F32)) for _ in range(2))
        t0 = start_ref[hp, i]
        odd = lax.rem(i * nsub - t0, nsub)
        carry = lax.fori_loop(0, odd, lambda s, cr: tiles(t0 + s, 1, cr, None), carry)
        carry = lax.fori_loop(0, (i * nsub - t0) // nsub, lambda b, cr: tiles(t0 + odd + b * nsub, nsub, cr, None), carry)
        carry = tiles(i * nsub, nsub, carry, masks)
        dq_ref[...] = jnp.where(first, carry[0][0], carry[1][0]) * LN2

        @pl.when(i == pl.num_programs(1) - 1)
        def _():
            def flush(j, _):
                rows = pl.ds(pl.multiple_of(j * tk, tk), tk)
                stage[0] = dkt_acc[j].T * LN2
                stage[1] = dvt_acc[j].T
                ck = pltpu.make_async_copy(stage.at[0], dk_hbm.at[hp, rows], sem.at[0])
                cv = pltpu.make_async_copy(stage.at[1], dv_hbm.at[hp, rows], sem.at[1])
                ck.start()
                cv.start()
                ck.wait()
                cv.wait()
                return 0
            lax.fori_loop(0, nkb, flush, 0)

    q_spec, k_spec, v_spec = _attn_specs(S, tq, nb)
    slab = SDS((nb, S, LANES), F32)
    return pl.pallas_call(
        body, name="attn_bwd", grid=(nb, S // tq),
        in_specs=[pl.BlockSpec(memory_space=pltpu.SMEM), q_spec, k_spec, v_spec, q_spec, q_spec],
        out_specs=[pl.BlockSpec((None, tq, LANES), lambda hp, i: (hp, i, 0)), ANY, ANY], out_shape=[slab, slab, slab],
        scratch_shapes=[pltpu.VMEM((nkb, LANES, tk), F32), pltpu.VMEM((nkb, LANES, tk), F32), pltpu.VMEM((2, tk, LANES), F32),
                        pltpu.SemaphoreType.DMA((2,))],
        compiler_params=_params("arbitrary", "arbitrary"),
    )(start, qkv, qkv, qkv, do, tot)


def _trail_sum(ext, bufs, cols, levels, n):
    def src(lo, size):
        return ext[pl.ds(lo, size), cols]
    for l in range(levels):
        lo = 8 * (l + 1)
        dst = bufs[l % 2]
        dst[lo:, :] = src(lo, n - lo) + src(lo - (1 << l), n - lo)
        def src(lo_, size, d=dst):
            return d[pl.ds(lo_, size), :]
    return src(HALO, n - HALO)


def _lead_sum(ext, bufs, cols, levels, n):
    def src(lo, size):
        return ext[pl.ds(lo, size), cols]
    for l in range(levels):
        hi = n - 8 * (l + 1)
        dst = bufs[l % 2]
        dst[0:hi, :] = src(0, hi) + src(1 << l, hi)
        def src(lo_, size, d=dst):
            return d[pl.ds(lo_, size), :]
    return src(0, n - HALO)


def _pool_diffs(x_ref, xh_ref, vec_ref, ext, bufs, tm, D):
    i = pl.program_id(0)
    gs, shift = vec_ref[0:1, :], vec_ref[1:2, :]
    n, r = _rms(x_ref[...])
    nh, _ = _rms(xh_ref[...])
    ext[0:HALO, :] = jnp.where(i > 0, nh * gs + shift, 0.0)
    ext[HALO:, :] = n * gs + shift
    t = i * tm + lax.broadcasted_iota(jnp.int32, (tm, 1), 0)
    dg = D // len(POOL_LEVELS)
    out = []
    for g, lv in enumerate(POOL_LEVELS):
        cols = slice(g * dg, (g + 1) * dg)
        inv = 1.0 / jnp.minimum(t + 1, 1 << lv).astype(F32)
        out.append((_trail_sum(ext, bufs, cols, lv, HALO + tm) * inv - ext[HALO:, cols], inv))
    return out, n, r


def _pool_fwd(x, vec, pw):
    S, D = x.shape
    ng, dg, _ = pw.shape
    tm = _tile(S, TOKEN_TILE)

    def body(x_ref, xh_ref, vec_ref, pw_ref, xo_ref, ext, buf_a, buf_b):
        diffs, _, _ = _pool_diffs(x_ref, xh_ref, vec_ref, ext, (buf_a, buf_b), tm, D)
        for g, (d, _) in enumerate(diffs):
            cols = slice(g * dg, (g + 1) * dg)
            y = (_dot(d.astype(BF16), pw_ref[g]) + vec_ref[4:5, cols]) * vec_ref[3:4, cols]
            xo_ref[:, cols] = x_ref[:, cols] + vec_ref[2:3, cols] * y

    return pl.pallas_call(
        body, name="pool_fwd", grid=(S // tm,),
        in_specs=[_rows(tm, D), _prev_halo(tm, D), _resident((8, D)), _resident(pw.shape)],
        out_specs=_rows(tm, D), out_shape=SDS((S, D), F32),
        scratch_shapes=[pltpu.VMEM((HALO + tm, D), F32), pltpu.VMEM((HALO + tm, dg), F32), pltpu.VMEM((HALO + tm, dg), F32)],
        compiler_params=_params("parallel"),
    )(x, x, vec, pw)


def _pool_bwd(x, dxo, vec, pw):
    S, D = x.shape
    ng, dg, _ = pw.shape
    tm = _tile(S, TOKEN_TILE)

    def body(x_ref, xh_ref, dxo_ref, dxn_ref, vec_ref, pw_ref, dx_ref, gpw_ref, sums_ref, ext, exte, buf_a, buf_b):
        _zero_at_first_step(gpw_ref, sums_ref)
        i = pl.program_id(0)
        bufs = (buf_a, buf_b)
        diffs, n, r = _pool_diffs(x_ref, xh_ref, vec_ref, ext, bufs, tm, D)
        gate, scale = vec_ref[2:3, :], vec_ref[3:4, :]
        dxo = dxo_ref[...]
        dyp_next = jnp.where(i < pl.num_programs(0) - 1, dxn_ref[...], 0.0) * gate * scale
        t_next = (i + 1) * tm + lax.broadcasted_iota(jnp.int32, (HALO, 1), 0)
        for g, (d, inv) in enumerate(diffs):
            cols = slice(g * dg, (g + 1) * dg)
            w = pw_ref[g]
            db = d.astype(BF16)
            ypre = _dot(db, w) + vec_ref[4:5, cols]
            dy = gate[:, cols] * dxo[:, cols]
            sums_ref[2:3, cols] += jnp.sum(dxo[:, cols] * ypre * scale[:, cols], axis=0, keepdims=True)
            sums_ref[3:4, cols] += jnp.sum(dy * ypre, axis=0, keepdims=True)
            dyp = dy * scale[:, cols]
            sums_ref[4:5, cols] += jnp.sum(dyp, axis=0, keepdims=True)
            dypb = dyp.astype(BF16)
            gpw_ref[g] += _dot_tn(db, dypb)
            dd = _dot_nt(dypb, w)
            dd_next = _dot_nt(dyp_next[:, cols].astype(BF16), w)
            inv_next = 1.0 / jnp.minimum(t_next + 1, 1 << POOL_LEVELS[g]).astype(F32)
            exte[0:tm, cols] = dd * inv
            exte[tm:, cols] = dd_next * inv_next
            ext[HALO:, cols] = _lead_sum(exte, bufs, cols, POOL_LEVELS[g], tm + HALO) - dd
        dx_ref[...] = _modulate_bwd(ext[HALO:, :], n, r, vec_ref[0:1, :], dxo, sums_ref)

    return pl.pallas_call(
        body, name="pool_bwd", grid=(S // tm,),
        in_specs=[_rows(tm, D), _prev_halo(tm, D), _rows(tm, D), _next_halo(tm, D, S), _resident((8, D)), _resident(pw.shape)],
        out_specs=[_rows(tm, D), pl.BlockSpec(pw.shape, lambda i: (0, 0, 0)), _acc_spec(8, D)],
        out_shape=[SDS((S, D), F32), SDS(pw.shape, F32), SDS((8, D), F32)],
        scratch_shapes=[pltpu.VMEM((HALO + tm, D), F32), pltpu.VMEM((tm + HALO, D), F32),
                        pltpu.VMEM((HALO + tm, dg), F32), pltpu.VMEM((HALO + tm, dg), F32)],
        compiler_params=_params("arbitrary"),
    )(x, x, dxo, dxo, vec, pw)


def _loss_head(y, target):
    S, D = y.shape
    tm = _tile(S, TOKEN_TILE)

    def body(y_ref, t_ref, dy_ref, sums_ref):
        _zero_at_first_step(sums_ref)
        err = y_ref[...] - t_ref[...]
        _add_rowsum(sums_ref, 0, err * err)
        dy_ref[...] = err * (1.0 / D)

    return pl.pallas_call(
        body, name="loss_head", grid=(S // tm,), in_specs=[_rows(tm, D), _rows(tm, D)],
        out_specs=[_rows(tm, D), _acc_spec(8, D)], out_shape=[SDS((S, D), F32), SDS((8, D), F32)],
        compiler_params=_params("arbitrary"),
    )(y, target)


def _cond_pre(c_cols, cond_w):
    def body(c_ref, w_ref, o_ref):
        o_ref[...] = _hdot(c_ref[...], w_ref[...])
    return pl.pallas_call(body, name="cond_pre", out_shape=SDS((c_cols.shape[0], cond_w.shape[1]), F32))(c_cols, cond_w)


def _cond_e(parts, cond_b):
    def body(p_ref, b_ref, pre_ref, e_ref):
        pre = p_ref[0] + p_ref[2] + p_ref[4] + p_ref[6] + b_ref[...]
        pre_ref[...] = pre
        e_ref[...] = pre * jax.nn.sigmoid(pre)
    shape = SDS(parts.shape[1:], F32)
    return pl.pallas_call(body, name="cond_e", out_shape=[shape, shape])(parts, cond_b)


def _mod_cols(e, ada_w, ada_b_cols):
    L, D, nc = ada_w.shape
    B = e.shape[0]

    def body(e_ref, w_ref, b_ref, o_ref):
        o_ref[...] = _hdot(e_ref[...], w_ref[...]) + b_ref[...]

    return pl.pallas_call(
        body, name="mod_cols", grid=(L,),
        in_specs=[pl.BlockSpec((B, D), lambda l: (0, 0)), pl.BlockSpec((None, D, nc), lambda l: (l, 0, 0)),
                  pl.BlockSpec((None, 1, nc), lambda l: (l, 0, 0))],
        out_specs=pl.BlockSpec((None, B, nc), lambda l: (l, 0, 0)), out_shape=SDS((L, B, nc), F32),
        compiler_params=_params("parallel"),
    )(e, ada_w, ada_b_cols)


def _mod_bwd(e, dmod_cols, ada_w):
    L, D, nc = ada_w.shape
    B = e.shape[0]

    def body(e_ref, d_ref, w_ref, gw_ref, de_ref):
        _zero_at_first_step(de_ref)
        gw_ref[...] = _hdot(e_ref[...], d_ref[...], (((0,), (0,)), ((), ())))
        de_ref[...] += _hdot(d_ref[...], w_ref[...], (((1,), (1,)), ((), ())))

    return pl.pallas_call(
        body, name="mod_bwd", grid=(L,),
        in_specs=[pl.BlockSpec((B, D), lambda l: (0, 0)), pl.BlockSpec((None, B, nc), lambda l: (l, 0, 0)),
                  pl.BlockSpec((None, D, nc), lambda l: (l, 0, 0))],
        out_specs=[pl.BlockSpec((None, D, nc), lambda l: (l, 0, 0)), pl.BlockSpec((B, D), lambda l: (0, 0))],
        out_shape=[SDS((L, D, nc), F32), SDS((B, D), F32)], compiler_params=_params("arbitrary"),
    )(e, dmod_cols, ada_w)


def _cond_bwd(de_parts, pre, c_cols):
    def body(p_ref, pre_ref, c_ref, gw_ref, gb_ref):
        pre = pre_ref[...]
        dpre = (p_ref[0] + p_ref[2] + p_ref[4] + p_ref[6]) * _silu_grad(pre, jax.nn.sigmoid(pre))
        gb_ref[...] = jnp.sum(dpre, axis=0, keepdims=True)
        gw_ref[...] = _hdot(c_ref[...], dpre, (((0,), (0,)), ((), ())))
    D = pre.shape[1]
    return pl.pallas_call(body, name="cond_bwd", out_shape=[SDS((c_cols.shape[1], D), F32), SDS((1, D), F32)])(de_parts, pre, c_cols)


def _as_rows(a):
    return a.reshape(-1, a.shape[-1])


def _row_tile(rows, width, n_arrays):
    t = max(8, (ADAM_TILE_BYTES // (4 * width)) // 8 * 8)
    while rows % t:
        t -= 8
        if t <= 0:
            return rows
    return t


def _adamw(w, g, m, v):
    shape = w.shape
    w, g, m, v = (_as_rows(a) for a in (w, g, m, v))
    R, C = w.shape
    tr = _row_tile(R, C, 7)

    def body(w_ref, g_ref, m_ref, v_ref, d_ref, nm_ref, nv_ref):
        g = g_ref[...]
        m = ADAM_B1 * m_ref[...] + (1.0 - ADAM_B1) * g
        v = ADAM_B2 * v_ref[...] + (1.0 - ADAM_B2) * (g * g)
        m_hat = m / (1.0 - ADAM_B1 ** ADAM_STEP)
        v_hat = v / (1.0 - ADAM_B2 ** ADAM_STEP)
        d_ref[...] = -ADAM_LR * (m_hat / (jnp.sqrt(v_hat) + ADAM_EPS) + ADAM_WD * w_ref[...])
        nm_ref[...] = m
        nv_ref[...] = v

    spec = _rows(tr, C)
    outs = pl.pallas_call(
        body, name="adamw", grid=(R // tr,), in_specs=[spec] * 4, out_specs=[spec] * 3,
        out_shape=[SDS((R, C), F32)] * 3, compiler_params=_params("parallel"),
    )(w, g, m, v)
    return tuple(o.reshape(shape) for o in outs)


def _sum_lead(a, out_dtype=F32):
    n = a.shape[0]
    shape = a.shape[1:]
    a = a.reshape(n, -1, a.shape[-1])
    _, R, C = a.shape
    tr = _row_tile(R, C, n + 1)

    def body(a_ref, o_ref):
        acc = a_ref[0].astype(F32)
        for k in range(1, n):
            acc = acc + a_ref[k].astype(F32)
        o_ref[...] = acc.astype(out_dtype)

    out = pl.pallas_call(
        body, name="sum_lead", grid=(R // tr,), in_specs=[pl.BlockSpec((n, tr, C), lambda i: (0, i, 0))],
        out_specs=_rows(tr, C), out_shape=SDS((R, C), out_dtype), compiler_params=_params("parallel"),
    )(a)
    return out.reshape(shape)


def _place():
    return lax.axis_index("x"), lax.axis_index("y"), lax.axis_index("c")


def _allgather8(a):
    def body(a_ref, o_ref, send, recv, local):
        mx, my, mc = _place()
        me = 4 * mx + 2 * my + mc
        mine = pltpu.make_async_copy(a_ref, o_ref.at[me], local)
        mine.start()
        copies = []
        for k in range(1, N_DEV):
            peer = (1 - mx if k & 4 else mx, 1 - my if k & 2 else my, 1 - mc if k & 1 else mc)
            cp = pltpu.make_async_remote_copy(a_ref, o_ref.at[me], send.at[k - 1], recv.at[k - 1], device_id=peer, device_id_type=MESH)
            cp.start()
            copies.append(cp)
        for cp in copies:
            cp.wait()
        mine.wait()

    return pl.pallas_call(
        body, name="allgather8", in_specs=[ANY], out_specs=ANY, out_shape=SDS((N_DEV,) + a.shape, a.dtype),
        scratch_shapes=[pltpu.SemaphoreType.DMA((N_DEV - 1,)), pltpu.SemaphoreType.DMA((N_DEV - 1,)), pltpu.SemaphoreType.DMA],
    )(a)


def _chip_exchange(arrs, scatter, name):
    n = len(arrs)
    out_shape = [SDS((N_CHIPS,) + ((a.shape[0],) + a.shape[2:] if scatter else a.shape), a.dtype) for a in arrs]

    def body(*refs):
        ins, outs, (send, recv, local) = refs[:n], refs[n:2 * n], refs[2 * n:]
        mx, my, mc = _place()
        chip = 2 * mx + my
        copies = []
        for a in range(n):
            cp = pltpu.make_async_copy(ins[a].at[:, chip] if scatter else ins[a], outs[a].at[chip], local.at[a])
            cp.start()
            copies.append(cp)
        for k in range(1, N_CHIPS):
            px, py = (1 - mx if k & 2 else mx), (1 - my if k & 1 else my)
            for a in range(n):
                s = (k - 1) * n + a
                cp = pltpu.make_async_remote_copy(ins[a].at[:, 2 * px + py] if scatter else ins[a], outs[a].at[chip],
                                                  send.at[s], recv.at[s], device_id=(px, py, mc), device_id_type=MESH)
                cp.start()
                copies.append(cp)
        for cp in copies:
            cp.wait()

    return pl.pallas_call(
        body, name=name, in_specs=[ANY] * n, out_specs=[ANY] * n, out_shape=out_shape,
        scratch_shapes=[pltpu.SemaphoreType.DMA((3 * n,)), pltpu.SemaphoreType.DMA((3 * n,)), pltpu.SemaphoreType.DMA((n,))],
    )(*arrs)


def _sibling_send(arrs, halves, name):
    n = len(arrs)
    out_shape = [SDS(a.shape[:2] + a.shape[3:] if halves else a.shape, a.dtype) for a in arrs]

    def body(*refs):
        ins, outs, (send, recv) = refs[:n], refs[n:2 * n], refs[2 * n:]
        mx, my, mc = _place()
        copies = []
        for a in range(n):
            cp = pltpu.make_async_remote_copy(ins[a].at[:, :, 1 - mc] if halves else ins[a], outs[a], send.at[a], recv.at[a],
                                              device_id=(mx, my, 1 - mc), device_id_type=MESH)
            cp.start()
            copies.append(cp)
        for cp in copies:
            cp.wait()

    return pl.pallas_call(
        body, name=name, in_specs=[ANY] * n, out_specs=[ANY] * n, out_shape=out_shape,
        scratch_shapes=[pltpu.SemaphoreType.DMA((n,)), pltpu.SemaphoreType.DMA((n,))],
    )(*arrs)


def _add_cast(a, b, dtype):
    shape = a.shape
    a, b = _as_rows(a), _as_rows(b)
    R, C = a.shape
    tr = _row_tile(R, C, 3)

    def body(a_ref, b_ref, o_ref):
        o_ref[...] = (a_ref[...] + b_ref[...]).astype(dtype)

    out = pl.pallas_call(body, name="pair_sum", grid=(R // tr,), in_specs=[_rows(tr, C)] * 2, out_specs=_rows(tr, C),
                         out_shape=SDS((R, C), dtype), compiler_params=_params("parallel"))(a, b)
    return out.reshape(shape)


def _pack(arrs):
    flat = jnp.concatenate([a.reshape(-1).astype(F32) for a in arrs])
    pad = (-flat.shape[0]) % (8 * LANES)
    return jnp.pad(flat, (0, pad)).reshape(-1, LANES)


def _unpack(buf, shapes):
    flat = buf.reshape(buf.shape[:-2] + (-1,))
    out, off = [], 0
    for s in shapes:
        size = 1
        for d in s:
            size *= d
        out.append(flat[..., off:off + size].reshape(flat.shape[:-1] + tuple(s)))
        off += size
    return out


def _unshard(stacked, axis):
    moved = jnp.moveaxis(stacked, 0, axis)
    return moved.reshape(moved.shape[:axis] + (N_CHIPS * moved.shape[axis + 1],) + moved.shape[axis + 2:])


def _my_shard(full, axis, chip):
    size = full.shape[axis] // N_CHIPS
    return lax.dynamic_slice_in_dim(full, chip * size, size, axis)


def kernel(x, c, cond_w, cond_b, ada_w, ada_b, norm_g, ffn_w1, ffn_w3, ffn_w2, a_w_in, a_b_in, a_dw, a_dw_b, a_ln_g, a_ln_b, a_w_out, a_b_out, b_w_qkv, b_q_g, b_k_g, b_w_o, p_w, p_b, p_scale, loss_target, m_cond_w, m_cond_b, m_ada_w, m_ada_b, m_norm_g, m_ffn_w1, m_ffn_w3, m_ffn_w2, m_a_w_in, m_a_b_in, m_a_dw, m_a_dw_b, m_a_ln_g, m_a_ln_b, m_a_w_out, m_a_b_out, m_b_w_qkv, m_b_q_g, m_b_k_g, m_b_w_o, m_p_w, m_p_b, m_p_scale, v_cond_w, v_cond_b, v_ada_w, v_ada_b, v_norm_g, v_ffn_w1, v_ffn_w3, v_ffn_w2, v_a_w_in, v_a_b_in, v_a_dw, v_a_dw_b, v_a_ln_g, v_a_ln_b, v_a_w_out, v_a_b_out, v_b_w_qkv, v_b_q_g, v_b_k_g, v_b_w_o, v_p_w, v_p_b, v_p_scale):
    w_in = dict(cond_w=cond_w, cond_b=cond_b, ada_w=ada_w, ada_b=ada_b, norm_g=norm_g, ffn_w1=ffn_w1, ffn_w3=ffn_w3, ffn_w2=ffn_w2,
                a_w_in=a_w_in, a_b_in=a_b_in, a_dw=a_dw, a_dw_b=a_dw_b, a_ln_g=a_ln_g, a_ln_b=a_ln_b, a_w_out=a_w_out, a_b_out=a_b_out,
                b_w_qkv=b_w_qkv, b_q_g=b_q_g, b_k_g=b_k_g, b_w_o=b_w_o, p_w=p_w, p_b=p_b, p_scale=p_scale)
    m_in = dict(zip(WEIGHTS, (m_cond_w, m_cond_b, m_ada_w, m_ada_b, m_norm_g, m_ffn_w1, m_ffn_w3, m_ffn_w2, m_a_w_in, m_a_b_in, m_a_dw,
                              m_a_dw_b, m_a_ln_g, m_a_ln_b, m_a_w_out, m_a_b_out, m_b_w_qkv, m_b_q_g, m_b_k_g, m_b_w_o, m_p_w, m_p_b, m_p_scale)))
    v_in = dict(zip(WEIGHTS, (v_cond_w, v_cond_b, v_ada_w, v_ada_b, v_norm_g, v_ffn_w1, v_ffn_w3, v_ffn_w2, v_a_w_in, v_a_b_in, v_a_dw,
                              v_a_dw_b, v_a_ln_g, v_a_ln_b, v_a_w_out, v_a_b_out, v_b_w_qkv, v_b_q_g, v_b_k_g, v_b_w_o, v_p_w, v_p_b, v_p_scale)))
    x = x[0]
    target = loss_target[0]
    S, D = x.shape
    L = ada_w.shape[0]
    assert b_q_g.shape[-1] == HEAD_DIM and D % LANES == 0 and S % HALO == 0
    mx, my, mc = _place()
    chip = 2 * mx + my
    me = 2 * chip + mc

    big = [ffn_w1.astype(BF16), ffn_w3.astype(BF16), ffn_w2.astype(BF16), a_w_in.astype(BF16), a_w_out.astype(BF16),
           b_w_qkv.astype(BF16), b_w_o.astype(BF16)]
    sharded = [k for k, ax in SMALL.items() if ax is not None]
    gathered = _chip_exchange(big + [_pack([w_in[k] for k in sharded])], scatter=False, name="weight_gather")
    g_w1, g_w3, g_w2, g_ain, g_aout, g_qkv, g_wo, g_small = gathered
    full = {k: _unshard(a, SMALL[k]) for k, a in zip(sharded, _unpack(g_small, [w_in[k].shape for k in sharded]))}
    n_conv, n_pool = a_w_in.shape[0], p_w.shape[0]
    conv_dw = [jnp.pad(full['a_dw'][ia], ((0, 1), (0, 0))) for ia in range(n_conv)]
    pool_w = [full['p_w'][ic].astype(BF16) for ic in range(n_pool)]
    qk_scale = LOG2E * HEAD_DIM ** -0.5
    gq = jnp.tile(b_q_g[0], D // HEAD_DIM).reshape(1, D) * qk_scale
    gk = jnp.tile(b_k_g[0], D // HEAD_DIM).reshape(1, D)

    c_all = _allgather8(c)[:, 0, :]
    c_cols = _my_shard(c_all, 1, chip)
    pre, e = _cond_e(_allgather8(_cond_pre(c_cols, cond_w)), cond_b.reshape(1, D))
    nc = ada_w.shape[2]
    mod_c = _mod_cols(e, ada_w, _my_shard(ada_b, 1, chip).reshape(L, 1, nc))
    mod_all = _allgather8(mod_c.reshape(L * N_DEV, nc)).reshape(N_CHIPS, 2, L, N_DEV, nc)[:, 0]
    mod = jnp.moveaxis(lax.dynamic_index_in_dim(mod_all, me, axis=2, keepdims=False), 0, 1).reshape(L, 3, 3, D)
    shift, scale, gate = mod[:, :, 0], mod[:, :, 1], 1.0 + mod[:, :, 2]
    gains = full['norm_g']

    def mod_vec(i, k, gate_factor=1.0):
        return _vec(gains[i, k] * (1.0 + scale[i, k]), shift[i, k], gate_factor * gate[i, k])

    saved = []
    ia = ib = ic = 0
    for i in range(L):
        for k, half in ((0, 0), (1, None), (2, 1)):
            if half is not None:
                vec = mod_vec(i, k, 0.5)
                w1, w3, w2 = g_w1[:, i, half], g_w3[:, i, half], g_w2[:, i, half]
                xo, a1, a3, y = _ffn_fwd(x, vec, w1, w3, w2)
                saved.append(('ffn', i, k, half, x, vec, (a1, a3, y, w1, w3, w2)))
            elif i % 3 == 0:
                vec = mod_vec(i, k)
                w_a = g_ain[:, ia]
                w_o = g_aout[:, ia].reshape(D, D)
                p, u = _in_fwd(x, vec, w_a, full['a_b_in'][ia].reshape(1, 2 * D), True, "conv_in_fwd")
                cvec = _vec(gate[i, k], full['a_dw_b'][ia], full['a_ln_g'][ia], full['a_ln_b'][ia], full['a_b_out'][ia])
                xo, v, y = _conv_mid_fwd(u, x, cvec, conv_dw[ia], w_o)
                saved.append(('conv', i, k, ia, x, vec, (p, u, v, y, cvec, w_a, w_o)))
                ia += 1
            elif i % 3 == 1:
                vec = mod_vec(i, k)
                w_q = g_qkv[:, ib]
                w_o = g_wo[:, ib].reshape(D, D)
                raw, = _in_fwd(x, vec, w_q, jnp.zeros((1, 3 * D), F32), False, "attn_in_fwd")
                qkv = _qknorm_fwd(raw, gq, gk)
                o, tot, start = _attn_fwd(qkv)
                xo, y = _out_fwd(x, o, vec, w_o)
                saved.append(('attn', i, k, ib, x, vec, (raw, qkv, o, (tot, start), y, w_q, w_o)))
                ib += 1
            else:
                vec = _vec(gains[i, k] * (1.0 + scale[i, k]), shift[i, k], gate[i, k], full['p_scale'][ic], full['p_b'][ic].reshape(D))
                xo = _pool_fwd(x, vec, pool_w[ic])
                saved.append(('pool', i, k, ic, x, vec, ()))
                ic += 1
            x = xo

    dx, sq = _loss_head(x, target)
    loss = lax.psum(0.5 / D * jnp.sum(sq[0]), ("x", "y", "c"))

    zeros_like_full = lambda k: jnp.zeros(full[k].shape, F32)
    g_full = {k: zeros_like_full(k) for k in sharded}
    g_full['b_q_g'] = jnp.zeros_like(b_q_g)
    g_full['b_k_g'] = jnp.zeros_like(b_k_g)
    dmod = jnp.zeros((L, 3, 3, D), F32)
    fc = ffn_w1.shape[-1]
    n_attn = b_w_qkv.shape[0]
    big_grads = {'ffn_w1': jnp.zeros((2 * L, N_CHIPS, D, fc), F32), 'ffn_w3': jnp.zeros((2 * L, N_CHIPS, D, fc), F32),
                 'ffn_w2': jnp.zeros((2 * L, N_CHIPS, fc, D), F32), 'a_w_in': jnp.zeros((n_conv, N_CHIPS, D, 2 * D // N_CHIPS), F32),
                 'a_w_out': jnp.zeros((n_conv, 1, D, D), F32), 'b_w_qkv': jnp.zeros((n_attn, N_CHIPS, D, 3 * D // N_CHIPS), F32),
                 'b_w_o': jnp.zeros((n_attn, 1, D, D), F32)}

    def wgrad(name, slot, a, b, a_mode, b_mode, nch):
        big_grads[name] = _wgrad(a, b, a_mode, b_mode, nch, name + "_grad", big_grads[name], slot)

    def put(name, idx, val):
        g_full[name] = g_full[name].at[idx].set(val.reshape(g_full[name][idx].shape))

    for kind, i, k, idx, xin, vec, res in reversed(saved):
        if kind == 'ffn':
            a1, a3, y, w1, w3, w2 = res
            dx, h, dy, u, da1, da3, sums = _ffn_bwd(xin, dx, a1, a3, y, vec, w1, w3, w2)
            wgrad('ffn_w1', 2 * i + idx, h, da1, 'full', 'lead', N_CHIPS)
            wgrad('ffn_w3', 2 * i + idx, h, da3, 'full', 'lead', N_CHIPS)
            wgrad('ffn_w2', 2 * i + idx, u, dy, 'lead', 'full', N_CHIPS)
            dgate = 0.5 * sums[2]
        elif kind == 'conv':
            p, u, v, y, cvec, w_a, w_o = res
            dv, q, dout, csums = _conv_mid_bwd(dx, y, v, cvec, w_o)
            wgrad('a_w_out', idx, q, dout, 'full', 'full', 1)
            du, gdw = _conv_transpose(dv, u, conv_dw[idx])
            dp, psums = _glu_bwd(du, p)
            dx, h, sums = _in_bwd(xin, dx, dp, vec, w_a, "conv_in_bwd")
            wgrad('a_w_in', idx, h, dp, 'full', 'col', N_CHIPS)
            dgate = csums[0]
            put('a_b_out', idx, csums[1])
            put('a_ln_g', idx, csums[2])
            put('a_ln_b', idx, csums[3])
            put('a_dw_b', idx, csums[4])
            put('a_dw', idx, gdw[:-1])
            put('a_b_in', idx, psums[0])
        elif kind == 'attn':
            raw, qkv, o, tot, y, w_q, w_o = res
            do, dout, osums = _out_bwd(dx, y, vec, w_o)
            wgrad('b_w_o', idx, o, dout, 'full', 'full', 1)
            dq, dk, dvv = _attn_bwd(qkv, do, *tot)
            draw, qsums = _qknorm_bwd(dq, dk, dvv, raw, gq, gk)
            dx, h, sums = _in_bwd(xin, dx, draw, vec, w_q, "attn_in_bwd")
            wgrad('b_w_qkv', idx, h, draw, 'full', 'col', N_CHIPS)
            dgate = osums[2]
            put('b_q_g', idx, qk_scale * jnp.sum(qsums[0].reshape(-1, HEAD_DIM), axis=0))
            put('b_k_g', idx, jnp.sum(qsums[1].reshape(-1, HEAD_DIM), axis=0))
        else:
            dx, gpw, sums = _pool_bwd(xin, dx, vec, pool_w[idx])
            dgate = sums[2]
            put('p_w', idx, gpw)
            put('p_scale', idx, sums[3])
            put('p_b', idx, sums[4])
        put('norm_g', (i, k), sums[0] * (1.0 + scale[i, k]))
        dmod = dmod.at[i, k].set(jnp.stack([sums[1], sums[0] * gains[i, k], dgate]))
    grad_x = dx[None]

    dmod_all = _allgather8(dmod.reshape(L, 9 * D))
    g_ada_b = _sum_lead(dmod_all)
    dmod_cols = jnp.moveaxis(_my_shard(dmod_all, 2, chip), 0, 1)
    g_ada_w, de_part = _mod_bwd(e, dmod_cols, ada_w)
    g_cond_w, g_cond_b = _cond_bwd(_allgather8(de_part), pre, c_cols)

    small_names = list(SMALL)
    g_full['cond_b'] = g_cond_b.reshape(D)
    g_full['ada_b'] = g_ada_b
    reduced = [k for k in small_names if k not in ('cond_b', 'ada_b')]
    red = _unpack(_sum_lead(_allgather8(_pack([g_full[k] for k in reduced]))), [g_full[k].shape for k in reduced])
    for k, a in zip(reduced, red):
        g_full[k] = a
    grads = {k: (g_full[k] if SMALL[k] is None else _my_shard(g_full[k], SMALL[k], chip)) for k in small_names}
    grads['cond_w'] = g_cond_w
    grads['ada_w'] = g_ada_w

    big_names = list(big_grads)
    by_half = []
    for name in big_names:
        g = big_grads[name]
        rows = g.shape[1] * g.shape[2] // N_CHIPS
        by_half.append(g.reshape(g.shape[0], N_CHIPS, 2, rows // 2, g.shape[3]))
    got = _sibling_send(by_half, True, "pair_send")
    pair = [_add_cast(lax.dynamic_index_in_dim(a, mc, axis=2, keepdims=False), b, BF16) for a, b in zip(by_half, got)]
    mine = [_sum_lead(a) for a in _chip_exchange(pair, scatter=True, name="grad_scatter")]
    theirs = _sibling_send(mine, False, "pair_return")
    for name, a, b in zip(big_names, mine, theirs):
        both = jnp.where(mc == 0, jnp.stack([a, b], axis=1), jnp.stack([b, a], axis=1))
        grads[name] = both.reshape(w_in[name].shape)

    delta, new_m, new_v = {}, {}, {}
    packed = [_pack([d[k] for k in small_names]) for d in (w_in, grads, m_in, v_in)]
    shapes = [w_in[k].shape for k in small_names]
    for out, buf in zip((delta, new_m, new_v), _adamw(*packed)):
        out.update(zip(small_names, _unpack(buf, shapes)))
    for k in WEIGHTS:
        if k not in SMALL:
            delta[k], new_m[k], new_v[k] = _adamw(w_in[k], grads[k], m_in[k], v_in[k])
    return (loss, grad_x, *[grads[k] for k in WEIGHTS], *[delta[k] for k in WEIGHTS], *[new_m[k] for k in WEIGHTS],
            *[new_v[k] for k in WEIGHTS])
```

```python
import functools

import jax
import jax.numpy as jnp
from jax import lax
from jax.experimental import pallas as pl
from jax.experimental.pallas import tpu as pltpu

F32 = jnp.float32
BF16 = jnp.bfloat16
SDS = jax.ShapeDtypeStruct
MESH = pl.DeviceIdType.MESH
ANY = pl.BlockSpec(memory_space=pl.ANY)

EPS = 1e-6
N_CHIPS = 4
N_DEV = 8
LANES = 128
HEAD_DIM = 64
VMEM_LIMIT_BYTES = 56 * 2**20
TOKEN_TILE = 512
WGRAD_TILE = 1024
ATTN_Q_TILE = 1024
ATTN_K_TILE = 256
ATTN_DEAD_BITS = 160.0
LOG2E = 1.4426950408889634
LN2 = 0.6931471805599453
HALO = 32
CONV_ROWS, CONV_COLS = 32, 256
ADAM_TILE_BYTES = 1 << 20
POOL_LEVELS = (1, 2, 3, 4)

ADAM_LR, ADAM_B1, ADAM_B2, ADAM_EPS, ADAM_WD, ADAM_STEP = 0.001, 0.9, 0.999, 1e-08, 0.01, 10

WEIGHTS = ['cond_w', 'cond_b', 'ada_w', 'ada_b', 'norm_g', 'ffn_w1', 'ffn_w3', 'ffn_w2', 'a_w_in', 'a_b_in', 'a_dw',
           'a_dw_b', 'a_ln_g', 'a_ln_b', 'a_w_out', 'a_b_out', 'b_w_qkv', 'b_q_g', 'b_k_g', 'b_w_o', 'p_w', 'p_b', 'p_scale']
SMALL = {'norm_g': 2, 'a_b_in': 1, 'a_dw': 2, 'a_dw_b': 1, 'a_ln_g': 1, 'a_ln_b': 1, 'a_b_out': 1, 'p_w': 2, 'p_b': 2,
         'p_scale': 1, 'cond_b': None, 'ada_b': None, 'b_q_g': None, 'b_k_g': None}


def _tile(n, pref):
    return pref if n % pref == 0 else n


def _params(*sem):
    return pltpu.CompilerParams(dimension_semantics=sem, vmem_limit_bytes=VMEM_LIMIT_BYTES)


def _resident(shape):
    nd = len(shape)
    return pl.BlockSpec(shape, lambda *_: (0,) * nd, pipeline_mode=pl.Buffered(1))


def _rows(tm, width):
    return pl.BlockSpec((tm, width), lambda i: (i, 0))


def _acc_spec(rows, width):
    return pl.BlockSpec((rows, width), lambda i: (0, 0))


def _dot(a, b):
    return jnp.dot(a, b, preferred_element_type=F32)


def _dot_nt(a, b):
    return lax.dot_general(a, b, (((1,), (1,)), ((), ())), preferred_element_type=F32)


def _dot_tn(a, b):
    return lax.dot_general(a, b, (((0,), (0,)), ((), ())), preferred_element_type=F32)


def _hdot(a, b, dims=(((1,), (0,)), ((), ()))):
    return lax.dot_general(a, b, dims, preferred_element_type=F32, precision=lax.Precision.HIGHEST)


def _zero_at_first_step(*refs):
    @pl.when(pl.program_id(0) == 0)
    def _():
        for r in refs:
            r[...] = jnp.zeros_like(r)


def _add_rowsum(ref, row, t):
    ref[row:row + 1, :] += jnp.sum(t, axis=0, keepdims=True)


def _rms(x):
    r = lax.rsqrt(jnp.mean(x * x, axis=-1, keepdims=True) + EPS)
    return x * r, r


def _modulate_bwd(dh, n, r, gs, dxo, sums_ref):
    _add_rowsum(sums_ref, 0, dh * n)
    _add_rowsum(sums_ref, 1, dh)
    dn = dh * gs
    return dxo + r * (dn - n * jnp.mean(dn * n, axis=-1, keepdims=True))


def _silu_grad(a, sg):
    return sg * (1.0 + a * (1.0 - sg))


def _vec(*rows):
    d = rows[0].shape[-1]
    rows = [r.reshape(1, d).astype(F32) for r in rows]
    return jnp.concatenate(rows + [jnp.zeros((8 - len(rows), d), F32)], axis=0)


def _ffn_fwd(x, vec, w1, w3, w2):
    S, D = x.shape
    nch, _, fc = w1.shape
    tm = _tile(S, TOKEN_TILE)

    def body(x_ref, vec_ref, w1_ref, w3_ref, w2_ref, xo_ref, a1_ref, a3_ref, y_ref):
        x = x_ref[...]
        n, _ = _rms(x)
        h = (n * vec_ref[0:1, :] + vec_ref[1:2, :]).astype(BF16)
        acc = jnp.zeros((tm, D), F32)
        for j in range(nch):
            a1 = _dot(h, w1_ref[j]).astype(BF16)
            a3 = _dot(h, w3_ref[j]).astype(BF16)
            a1_ref[j] = a1
            a3_ref[j] = a3
            a1 = a1.astype(F32)
            u = a1 * jax.nn.sigmoid(a1) * a3.astype(F32)
            acc = acc + _dot(u.astype(BF16), w2_ref[j])
        y_ref[...] = acc.astype(BF16)
        xo_ref[...] = x + vec_ref[2:3, :] * acc

    chunked = pl.BlockSpec((nch, tm, fc), lambda i: (0, i, 0))
    return pl.pallas_call(
        body, name="ffn_fwd", grid=(S // tm,),
        in_specs=[_rows(tm, D), _resident((8, D)), _resident(w1.shape), _resident(w3.shape), _resident(w2.shape)],
        out_specs=[_rows(tm, D), chunked, chunked, _rows(tm, D)],
        out_shape=[SDS((S, D), F32), SDS((nch, S, fc), BF16), SDS((nch, S, fc), BF16), SDS((S, D), BF16)],
        compiler_params=_params("parallel"),
    )(x, vec, w1, w3, w2)


def _ffn_bwd(x, dxo, a1, a3, y, vec, w1, w3, w2):
    S, D = x.shape
    nch, _, fc = w1.shape
    tm = _tile(S, TOKEN_TILE // 2)

    def body(x_ref, dxo_ref, a1_ref, a3_ref, y_ref, vec_ref, w1_ref, w3_ref, w2_ref,
             dx_ref, h_ref, dy_ref, u_ref, da1_ref, da3_ref, sums_ref):
        _zero_at_first_step(sums_ref)
        x = x_ref[...]
        dxo = dxo_ref[...]
        gs = vec_ref[0:1, :]
        n, r = _rms(x)
        h_ref[...] = (n * gs + vec_ref[1:2, :]).astype(BF16)
        _add_rowsum(sums_ref, 2, dxo * y_ref[...].astype(F32))
        dy = (vec_ref[2:3, :] * dxo).astype(BF16)
        dy_ref[...] = dy
        dh = jnp.zeros((tm, D), F32)
        for j in range(nch):
            a1 = a1_ref[j].astype(F32)
            a3 = a3_ref[j].astype(F32)
            sg = jax.nn.sigmoid(a1)
            s = a1 * sg
            du = _dot_nt(dy, w2_ref[j])
            da1 = (du * a3 * _silu_grad(a1, sg)).astype(BF16)
            da3 = (du * s).astype(BF16)
            u_ref[j] = (s * a3).astype(BF16)
            da1_ref[j] = da1
            da3_ref[j] = da3
            dh = dh + _dot_nt(da1, w1_ref[j]) + _dot_nt(da3, w3_ref[j])
        dx_ref[...] = _modulate_bwd(dh, n, r, gs, dxo, sums_ref)

    chunked = pl.BlockSpec((nch, tm, fc), lambda i: (0, i, 0))
    return pl.pallas_call(
        body, name="ffn_bwd", grid=(S // tm,),
        in_specs=[_rows(tm, D), _rows(tm, D), chunked, chunked, _rows(tm, D), _resident((8, D)),
                  _resident(w1.shape), _resident(w3.shape), _resident(w2.shape)],
        out_specs=[_rows(tm, D), _rows(tm, D), _rows(tm, D), chunked, chunked, chunked, _acc_spec(8, D)],
        out_shape=[SDS((S, D), F32), SDS((S, D), BF16), SDS((S, D), BF16), SDS((nch, S, fc), BF16),
                   SDS((nch, S, fc), BF16), SDS((nch, S, fc), BF16), SDS((8, D), F32)],
        compiler_params=_params("arbitrary"),
    )(x, dxo, a1, a3, y, vec, w1, w3, w2)


def _wgrad(a, b, a_mode, b_mode, nch, name, acc, slot):
    S = a.shape[-2]
    M = a.shape[-1]
    N = b.shape[-1] // (nch if b_mode == 'col' else 1)
    assert acc.shape[1:] == (nch, M, N)
    ts = _tile(S, WGRAD_TILE)

    def spec(mode, width):
        if mode == 'full':
            return pl.BlockSpec((ts, width), lambda j, s: (s, 0))
        if mode == 'lead':
            return pl.BlockSpec((None, ts, width), lambda j, s: (j, s, 0))
        return pl.BlockSpec((ts, width), lambda j, s: (s, j))

    def body(a_ref, b_ref, acc_ref, o_ref):
        @pl.when(pl.program_id(1) == 0)
        def _():
            o_ref[...] = jnp.zeros_like(o_ref)
        o_ref[...] += _dot_tn(a_ref[...], b_ref[...])

    return pl.pallas_call(
        body, name=name, grid=(nch, S // ts), in_specs=[spec(a_mode, M), spec(b_mode, N), ANY],
        out_specs=pl.BlockSpec((None, None, M, N), lambda j, s: (slot, j, 0, 0)), out_shape=SDS(acc.shape, F32),
        input_output_aliases={2: 0}, compiler_params=_params("parallel", "arbitrary"),
    )(a, b, acc)


def _in_fwd(x, vec, w, bias, glu, name):
    S, D = x.shape
    nch, _, nc = w.shape
    N = nch * nc
    tm = _tile(S, TOKEN_TILE)

    def body(x_ref, vec_ref, w_ref, b_ref, p_ref, *u_ref):
        n, _ = _rms(x_ref[...])
        h = (n * vec_ref[0:1, :] + vec_ref[1:2, :]).astype(BF16)
        for j in range(nch):
            cols = slice(j * nc, (j + 1) * nc)
            p_ref[:, cols] = (_dot(h, w_ref[j]) + b_ref[:, cols]).astype(p_ref.dtype)
        if glu:
            half = N // 2
            u_ref[0][...] = p_ref[:, :half].astype(F32) * jax.nn.sigmoid(p_ref[:, half:].astype(F32))

    out_specs = [_rows(tm, N)] + ([_rows(tm, N // 2)] if glu else [])
    out_shape = [SDS((S, N), BF16 if glu else F32)] + ([SDS((S, N // 2), F32)] if glu else [])
    return pl.pallas_call(
        body, name=name, grid=(S // tm,),
        in_specs=[_rows(tm, D), _resident((8, D)), _resident(w.shape), _resident((1, N))],
        out_specs=out_specs, out_shape=out_shape, compiler_params=_params("parallel"),
    )(x, vec, w, bias)


def _in_bwd(x, dxo, dp, vec, w, name):
    S, D = x.shape
    nch, _, nc = w.shape
    tm = _tile(S, TOKEN_TILE)

    def body(x_ref, dxo_ref, dp_ref, vec_ref, w_ref, dx_ref, h_ref, sums_ref):
        _zero_at_first_step(sums_ref)
        gs = vec_ref[0:1, :]
        n, r = _rms(x_ref[...])
        h_ref[...] = (n * gs + vec_ref[1:2, :]).astype(BF16)
        dh = jnp.zeros((tm, D), F32)
        for j in range(nch):
            dh = dh + _dot_nt(dp_ref[:, j * nc:(j + 1) * nc], w_ref[j])
        dx_ref[...] = _modulate_bwd(dh, n, r, gs, dxo_ref[...], sums_ref)

    return pl.pallas_call(
        body, name=name, grid=(S // tm,),
        in_specs=[_rows(tm, D), _rows(tm, D), _rows(tm, nch * nc), _resident((8, D)), _resident(w.shape)],
        out_specs=[_rows(tm, D), _rows(tm, D), _acc_spec(8, D)],
        out_shape=[SDS((S, D), F32), SDS((S, D), BF16), SDS((8, D), F32)],
        compiler_params=_params("arbitrary"),
    )(x, dxo, dp, vec, w)


def _out_fwd(x, t, vec, w):
    S, D = x.shape
    tm = _tile(S, TOKEN_TILE)

    def body(x_ref, t_ref, vec_ref, w_ref, xo_ref, y_ref):
        y = _dot(t_ref[...], w_ref[...])
        y_ref[...] = y.astype(BF16)
        xo_ref[...] = x_ref[...] + vec_ref[2:3, :] * y

    return pl.pallas_call(
        body, name="attn_out_fwd", grid=(S // tm,),
        in_specs=[_rows(tm, D), _rows(tm, t.shape[1]), _resident((8, D)), _resident(w.shape)],
        out_specs=[_rows(tm, D), _rows(tm, D)], out_shape=[SDS((S, D), F32), SDS((S, D), BF16)],
        compiler_params=_params("parallel"),
    )(x, t, vec, w)


def _out_bwd(dxo, y, vec, w):
    S, D = dxo.shape
    K = w.shape[0]
    tm = _tile(S, TOKEN_TILE)

    def body(dxo_ref, y_ref, vec_ref, w_ref, dt_ref, dout_ref, sums_ref):
        _zero_at_first_step(sums_ref)
        dxo = dxo_ref[...]
        _add_rowsum(sums_ref, 2, dxo * y_ref[...].astype(F32))
        dout = (vec_ref[2:3, :] * dxo).astype(BF16)
        dout_ref[...] = dout
        dt_ref[...] = _dot_nt(dout, w_ref[...]).astype(BF16)

    return pl.pallas_call(
        body, name="attn_out_bwd", grid=(S // tm,),
        in_specs=[_rows(tm, D), _rows(tm, D), _resident((8, D)), _resident(w.shape)],
        out_specs=[_rows(tm, K), _rows(tm, D), _acc_spec(8, D)],
        out_shape=[SDS((S, K), BF16), SDS((S, D), BF16), SDS((8, D), F32)],
        compiler_params=_params("arbitrary"),
    )(dxo, y, vec, w)


def _prev_halo(tm, width):
    return pl.BlockSpec((HALO, width), lambda i: (jnp.maximum(i * (tm // HALO) - 1, 0), 0))


def _next_halo(tm, width, n_rows):
    last = n_rows // HALO - 1
    return pl.BlockSpec((HALO, width), lambda i: (jnp.minimum((i + 1) * (tm // HALO), last), 0))


def _layer_norm(v, g, b):
    mu = jnp.mean(v, axis=-1, keepdims=True)
    vc = v - mu
    rstd = lax.rsqrt(jnp.mean(vc * vc, axis=-1, keepdims=True) + EPS)
    vh = vc * rstd
    return vh * g + b, vh, rstd


def _fill_shifts(ext, sh, n):
    ext[n:, :] = jnp.zeros((8, ext.shape[1]), F32)
    for b in range(1, 8):
        sh[b - 1] = ext[pl.ds(b, n), :]


def _shifted(ext, sh, off, r0, rows, cols):
    b = off % 8
    base = r0 + off - b
    return ext[pl.ds(base, rows), cols] if b == 0 else sh[b - 1, pl.ds(base, rows), cols]


def _conv_mid_fwd(u, x, vec, dw, w_out):
    S, D = x.shape
    taps = dw.shape[0] - 1
    tm = _tile(S, TOKEN_TILE // 2)

    def body(u_ref, uh_ref, x_ref, vec_ref, dw_ref, w_ref, xo_ref, v_ref, y_ref, ext, sh):
        n = HALO + tm
        ext[0:HALO, :] = jnp.where(pl.program_id(0) > 0, uh_ref[...], 0.0)
        ext[HALO:n, :] = u_ref[...]
        _fill_shifts(ext, sh, n)
        cb = min(CONV_COLS, D)
        for c0 in range(0, D, cb):
            cols = slice(c0, c0 + cb)
            w = [dw_ref[k:k + 1, cols] for k in range(taps)]
            for r0 in range(0, tm, CONV_ROWS):
                acc = jnp.zeros((CONV_ROWS, cb), F32) + vec_ref[1:2, cols]
                for k in range(taps):
                    acc = acc + w[k] * _shifted(ext, sh, HALO - (taps - 1) + k, r0, CONV_ROWS, cols)
                v_ref[r0:r0 + CONV_ROWS, cols] = acc
        v = v_ref[...]
        l, _, _ = _layer_norm(v, vec_ref[2:3, :], vec_ref[3:4, :])
        q = (l * jax.nn.sigmoid(l)).astype(BF16)
        y = _dot(q, w_ref[...]) + vec_ref[4:5, :]
        y_ref[...] = y.astype(BF16)
        xo_ref[...] = x_ref[...] + vec_ref[0:1, :] * y

    return pl.pallas_call(
        body, name="conv_mid_fwd", grid=(S // tm,),
        in_specs=[_rows(tm, D), _prev_halo(tm, D), _rows(tm, D), _resident((8, D)), _resident(dw.shape), _resident(w_out.shape)],
        out_specs=[_rows(tm, D), _rows(tm, D), _rows(tm, D)],
        out_shape=[SDS((S, D), F32), SDS((S, D), F32), SDS((S, D), BF16)],
        scratch_shapes=[pltpu.VMEM((HALO + tm + 8, D), F32), pltpu.VMEM((7, HALO + tm, D), F32)],
        compiler_params=_params("parallel"),
    )(u, u, x, vec, dw, w_out)


def _conv_mid_bwd(dxo, y, v, vec, w_out):
    S, D = dxo.shape
    tm = _tile(S, TOKEN_TILE)

    def body(dxo_ref, y_ref, v_ref, vec_ref, w_ref, dv_ref, q_ref, dout_ref, sums_ref):
        _zero_at_first_step(sums_ref)
        dxo = dxo_ref[...]
        _add_rowsum(sums_ref, 0, dxo * y_ref[...].astype(F32))
        dout = vec_ref[0:1, :] * dxo
        _add_rowsum(sums_ref, 1, dout)
        dout = dout.astype(BF16)
        dout_ref[...] = dout
        ln_g = vec_ref[2:3, :]
        l, vh, rstd = _layer_norm(v_ref[...], ln_g, vec_ref[3:4, :])
        sg = jax.nn.sigmoid(l)
        q_ref[...] = (l * sg).astype(BF16)
        dl = _dot_nt(dout, w_ref[...]) * _silu_grad(l, sg)
        _add_rowsum(sums_ref, 2, dl * vh)
        _add_rowsum(sums_ref, 3, dl)
        dvh = dl * ln_g
        dv = rstd * (dvh - jnp.mean(dvh, axis=-1, keepdims=True) - vh * jnp.mean(dvh * vh, axis=-1, keepdims=True))
        _add_rowsum(sums_ref, 4, dv)
        dv_ref[...] = dv

    return pl.pallas_call(
        body, name="conv_mid_bwd", grid=(S // tm,),
        in_specs=[_rows(tm, D), _rows(tm, D), _rows(tm, D), _resident((8, D)), _resident(w_out.shape)],
        out_specs=[_rows(tm, D), _rows(tm, D), _rows(tm, D), _acc_spec(8, D)],
        out_shape=[SDS((S, D), F32), SDS((S, D), BF16), SDS((S, D), BF16), SDS((8, D), F32)],
        compiler_params=_params("arbitrary"),
    )(dxo, y, v, vec, w_out)


def _conv_transpose(dv, u, dw):
    S, D = dv.shape
    taps = dw.shape[0] - 1
    tm = _tile(S, TOKEN_TILE // 2)

    def body(dv_ref, dvn_ref, u_ref, uh_ref, dw_ref, du_ref, gdw_ref, extv, shv, extu, shu):
        _zero_at_first_step(gdw_ref)
        i = pl.program_id(0)
        n = HALO + tm
        extv[0:tm, :] = dv_ref[...]
        extv[tm:n, :] = jnp.where(i < pl.num_programs(0) - 1, dvn_ref[...], 0.0)
        _fill_shifts(extv, shv, n)
        extu[0:HALO, :] = jnp.where(i > 0, uh_ref[...], 0.0)
        extu[HALO:n, :] = u_ref[...]
        _fill_shifts(extu, shu, n)
        cb = min(CONV_COLS, D)
        for c0 in range(0, D, cb):
            cols = slice(c0, c0 + cb)
            w = [dw_ref[k:k + 1, cols] for k in range(taps)]
            for r0 in range(0, tm, CONV_ROWS):
                acc = jnp.zeros((CONV_ROWS, cb), F32)
                for k in range(taps):
                    acc = acc + w[k] * _shifted(extv, shv, taps - 1 - k, r0, CONV_ROWS, cols)
                du_ref[r0:r0 + CONV_ROWS, cols] = acc
        for c0 in range(0, D, LANES):
            cols = slice(c0, c0 + LANES)
            accs = [jnp.zeros((8, LANES), F32) for _ in range(taps)]
            for r0 in range(0, tm, CONV_ROWS):
                dvb = extv[r0:r0 + CONV_ROWS, cols]
                for k in range(taps):
                    p = dvb * _shifted(extu, shu, HALO - (taps - 1) + k, r0, CONV_ROWS, cols)
                    for s in range(0, CONV_ROWS, 8):
                        accs[k] = accs[k] + p[s:s + 8]
            for k in range(taps):
                gdw_ref[k:k + 1, cols] += jnp.sum(accs[k], axis=0, keepdims=True)

    return pl.pallas_call(
        body, name="conv_transpose", grid=(S // tm,),
        in_specs=[_rows(tm, D), _next_halo(tm, D, S), _rows(tm, D), _prev_halo(tm, D), _resident(dw.shape)],
        out_specs=[_rows(tm, D), _acc_spec(dw.shape[0], D)],
        out_shape=[SDS((S, D), F32), SDS(dw.shape, F32)],
        scratch_shapes=[pltpu.VMEM((HALO + tm + 8, D), F32), pltpu.VMEM((7, HALO + tm, D), F32),
                        pltpu.VMEM((HALO + tm + 8, D), F32), pltpu.VMEM((7, HALO + tm, D), F32)],
        compiler_params=_params("arbitrary"),
    )(dv, dv, u, u, dw)


def _glu_bwd(du, p):
    S, D = du.shape
    tm = _tile(S, TOKEN_TILE)

    def body(du_ref, p_ref, dp_ref, sums_ref):
        _zero_at_first_step(sums_ref)
        du = du_ref[...]
        a = p_ref[:, :D].astype(F32)
        sb = jax.nn.sigmoid(p_ref[:, D:].astype(F32))
        da = du * sb
        db = du * a * sb * (1.0 - sb)
        dp_ref[:, :D] = da.astype(BF16)
        dp_ref[:, D:] = db.astype(BF16)
        sums_ref[0:1, :D] += jnp.sum(da, axis=0, keepdims=True)
        sums_ref[0:1, D:] += jnp.sum(db, axis=0, keepdims=True)

    return pl.pallas_call(
        body, name="glu_bwd", grid=(S // tm,), in_specs=[_rows(tm, D), _rows(tm, 2 * D)],
        out_specs=[_rows(tm, 2 * D), _acc_spec(8, 2 * D)], out_shape=[SDS((S, 2 * D), BF16), SDS((8, 2 * D), F32)],
        compiler_params=_params("arbitrary"),
    )(du, p)


def _head_mean(t, bd):
    hi = t.astype(BF16)
    lo = (t - hi.astype(F32)).astype(BF16)
    return (_dot(hi, bd) + _dot(lo, bd)) * (1.0 / HEAD_DIM)


def _head_blocks():
    r = lax.broadcasted_iota(jnp.int32, (LANES, LANES), 0) // HEAD_DIM
    c = lax.broadcasted_iota(jnp.int32, (LANES, LANES), 1) // HEAD_DIM
    return (r == c).astype(BF16)


def _qknorm_fwd(raw, gq, gk):
    S, D3 = raw.shape
    D = D3 // 3
    tm = _tile(S, TOKEN_TILE)

    def body(raw_ref, gq_ref, gk_ref, o_ref):
        bd = _head_blocks()
        for off, g_ref in ((0, gq_ref), (D, gk_ref)):
            for c in range(D // LANES):
                cols = slice(off + c * LANES, off + (c + 1) * LANES)
                xs = raw_ref[:, cols]
                r = lax.rsqrt(_head_mean(xs * xs, bd) + EPS)
                o_ref[:, cols] = (xs * r * g_ref[:, c * LANES:(c + 1) * LANES]).astype(BF16)
        o_ref[:, 2 * D:] = raw_ref[:, 2 * D:].astype(BF16)

    return pl.pallas_call(
        body, name="qknorm_fwd", grid=(S // tm,), in_specs=[_rows(tm, D3), _resident((1, D)), _resident((1, D))],
        out_specs=_rows(tm, D3), out_shape=SDS((S, D3), BF16), compiler_params=_params("parallel"),
    )(raw, gq, gk)


def _qknorm_bwd(dq, dk, dv, raw, gq, gk):
    S, D3 = raw.shape
    D = D3 // 3
    nb = D // LANES
    tm = _tile(S, TOKEN_TILE)

    def body(dq_ref, dk_ref, dv_ref, raw_ref, gq_ref, gk_ref, o_ref, sums_ref):
        _zero_at_first_step(sums_ref)
        bd = _head_blocks()
        for row, (off, g_ref, d_ref) in enumerate(((0, gq_ref, dq_ref), (D, gk_ref, dk_ref))):
            for c in range(nb):
                lanes = slice(c * LANES, (c + 1) * LANES)
                cols = slice(off + c * LANES, off + (c + 1) * LANES)
                xs = raw_ref[:, cols]
                r = lax.rsqrt(_head_mean(xs * xs, bd) + EPS)
                n = xs * r
                dhat = d_ref[c]
                sums_ref[row:row + 1, lanes] += jnp.sum(dhat * n, axis=0, keepdims=True)
                dn = dhat * g_ref[:, lanes]
                o_ref[:, cols] = (r * (dn - n * _head_mean(dn * n, bd))).astype(BF16)
        for c in range(nb):
            o_ref[:, 2 * D + c * LANES:2 * D + (c + 1) * LANES] = dv_ref[c].astype(BF16)

    tiles = pl.BlockSpec((nb, tm, LANES), lambda i: (0, i, 0))
    return pl.pallas_call(
        body, name="qknorm_bwd", grid=(S // tm,),
        in_specs=[tiles, tiles, tiles, _rows(tm, D3), _resident((1, D)), _resident((1, D))],
        out_specs=[_rows(tm, D3), _acc_spec(8, D)], out_shape=[SDS((S, D3), BF16), SDS((8, D), F32)],
        compiler_params=_params("arbitrary"),
    )(dq, dk, dv, raw, gq, gk)


def _softplus2(z):
    return jnp.maximum(z, jnp.log2(1.0 + jnp.exp2(jnp.minimum(z, 30.0))))


def _attn_tiles(S):
    tq = _tile(S, ATTN_Q_TILE)
    tk = _tile(tq, ATTN_K_TILE)
    return tq, tk


def _attn_consts(tq, tk):
    lane = lax.broadcasted_iota(jnp.int32, (1, LANES), 1)
    first = lane < HEAD_DIM
    r = lax.broadcasted_iota(jnp.int32, (tq, tk), 0)
    c = lax.broadcasted_iota(jnp.int32, (tq, tk), 1)
    kr = lax.broadcasted_iota(jnp.int32, (tk, tk), 0)
    kc = lax.broadcasted_iota(jnp.int32, (tk, tk), 1)
    masks = [(a * tk + c) < r for a in range(tq // tk)]
    later = (kr > kc).astype(BF16)
    upto = (kr <= kc).astype(BF16)
    return first, masks, later, upto


def _attn_specs(S, tq, nb):
    q_spec = pl.BlockSpec((tq, LANES), lambda hp, i: (i, hp))
    k_spec = pl.BlockSpec((S, LANES), lambda hp, i: (0, nb + hp), pipeline_mode=pl.Buffered(1))
    v_spec = pl.BlockSpec((S, LANES), lambda hp, i: (0, 2 * nb + hp), pipeline_mode=pl.Buffered(1))
    return q_spec, k_spec, v_spec


def _attn_fwd(qkv):
    S, D3 = qkv.shape
    D = D3 // 3
    nb = D // LANES
    tq, tk = _attn_tiles(S)
    nsub = tq // tk

    def body(q_ref, k_ref, v_ref, o_ref, tot_ref, start_ref):
        hp = pl.program_id(0)
        i = pl.program_id(1)
        first, masks, later, _ = _attn_consts(tq, tk)
        q = q_ref[...]
        qs = (jnp.where(first, q, jnp.zeros_like(q)), jnp.where(first, jnp.zeros_like(q), q))

        def tiles(t0, n, carry, tile_masks):
            out = []
            for h in range(2):
                o, c = carry[h]
                for a in reversed(range(n)):
                    mask = None if tile_masks is None else tile_masks[a]
                    rows = pl.ds(pl.multiple_of((t0 + a) * tk, tk), tk)
                    kb = k_ref[rows, :]
                    z = _dot_nt(qs[h], kb)
                    sp = _softplus2(z)
                    logsig = z - sp
                    if mask is not None:
                        sp = jnp.where(mask, sp, 0.0)
                    av = jnp.exp2(logsig - _dot(sp.astype(BF16), later) - c)
                    if mask is not None:
                        av = jnp.where(mask, av, 0.0)
                    o = o + _dot(av.astype(BF16), v_ref[rows, :])
                    c = c + jnp.sum(sp, axis=1, keepdims=True)
                out.append((o, c))
            return tuple(out)

        def live(carry):
            return jnp.minimum(jnp.min(carry[0][1]), jnp.min(carry[1][1])) < ATTN_DEAD_BITS

        carry = tuple((jnp.zeros((tq, LANES), F32), jnp.zeros((tq, 1), F32)) for _ in range(2))
        carry = tiles(i * nsub, nsub, carry, masks)
        band_floor = jnp.maximum((i - 1) * nsub, 0)
        t, carry = lax.while_loop(lambda st: (st[0] >= band_floor) & (st[0] >= 0) & live(st[1]),
                                  lambda st: (st[0] - 1, tiles(st[0], 1, st[1], None)), (i * nsub - 1, carry))
        t, carry = lax.while_loop(lambda st: (st[0] >= nsub - 1) & live(st[1]),
                                  lambda st: (st[0] - nsub, tiles(st[0] - (nsub - 1), nsub, st[1], None)), (t, carry))
        (o_a, c_a), (o_b, c_b) = carry
        o_ref[...] = jnp.where(first, o_a, o_b).astype(BF16)
        tot_ref[...] = jnp.where(first, c_a, c_b)
        start_ref[hp, i] = t + 1

    q_spec, k_spec, v_spec = _attn_specs(S, tq, nb)
    return pl.pallas_call(
        body, name="attn_fwd", grid=(nb, S // tq), in_specs=[q_spec, k_spec, v_spec],
        out_specs=[q_spec, q_spec, pl.BlockSpec(memory_space=pltpu.SMEM)],
        out_shape=[SDS((S, D), BF16), SDS((S, D), F32), SDS((nb, S // tq), jnp.int32)],
        compiler_params=_params("arbitrary", "arbitrary"),
    )(qkv, qkv, qkv)


def _attn_bwd(qkv, do, tot, start):
    S, D3 = qkv.shape
    D = D3 // 3
    nb = D // LANES
    tq, tk = _attn_tiles(S)
    nsub = tq // tk
    nkb = S // tk

    def body(start_ref, q_ref, k_ref, v_ref, do_ref, tot_ref, dq_ref, dk_hbm, dv_hbm, dkt_acc, dvt_acc, stage, sem):
        hp = pl.program_id(0)
        i = pl.program_id(1)

        @pl.when(i == 0)
        def _():
            dkt_acc[...] = jnp.zeros_like(dkt_acc)
            dvt_acc[...] = jnp.zeros_like(dvt_acc)

        first, masks, later, upto = _attn_consts(tq, tk)
        q = q_ref[...]
        do = do_ref[...]
        zero = jnp.zeros_like(q)
        qs = (jnp.where(first, q, zero), jnp.where(first, zero, q))
        dos = (jnp.where(first, do, zero), jnp.where(first, zero, do))
        qt = q.astype(F32).T.astype(BF16)
        dot_ = do.astype(F32).T.astype(BF16)
        qts = (qt[:HEAD_DIM], qt[HEAD_DIM:])
        dots = (dot_[:HEAD_DIM], dot_[HEAD_DIM:])
        tots = (tot_ref[:, 0:1], tot_ref[:, HEAD_DIM:HEAD_DIM + 1])

        def tiles(t0, n, carry, tile_masks):
            carry = list(carry)
            for a in range(n):
                mask = None if tile_masks is None else tile_masks[a]
                j = t0 + a
                rows = pl.ds(pl.multiple_of(j * tk, tk), tk)
                kb = k_ref[rows, :]
                vb = v_ref[rows, :]
                dkts, dvts = [], []
                for h in range(2):
                    dq, cum, pre = carry[h]
                    z = _dot_nt(qs[h], kb)
                    sp = _softplus2(z)
                    logsig = z - sp
                    if mask is not None:
                        sp = jnp.where(mask, sp, 0.0)
                    cum = cum + jnp.sum(sp, axis=1, keepdims=True)
                    av = jnp.exp2(logsig - _dot(sp.astype(BF16), later) - (tots[h] - cum))
                    if mask is not None:
                        av = jnp.where(mask, av, 0.0)
                    g = _dot_nt(dos[h], vb) * av
                    dz = g - jnp.exp2(logsig) * (pre + _dot(g.astype(BF16), upto))
                    if mask is not None:
                        dz = jnp.where(mask, dz, 0.0)
                    dz = dz.astype(BF16)
                    dkts.append(_dot(qts[h], dz))
                    dvts.append(_dot(dots[h], av.astype(BF16)))
                    carry[h] = (dq + _dot(dz, kb), cum, pre + jnp.sum(g, axis=1, keepdims=True))
                dkt_acc[j] += jnp.concatenate(dkts, axis=0)
                dvt_acc[j] += jnp.concatenate(dvts, axis=0)
            return tuple(carry)

        carry = tuple((jnp.zeros((tq, LANES), F32), jnp.zeros((tq, 1), F32), jnp.zeros((tq, 1), F32)) for _ in range(2))
        t0 = start_ref[hp, i]
        odd = lax.rem(i * nsub - t0, nsub)
        carry = lax.fori_loop(0, odd, lambda s, cr: tiles(t0 + s, 1, cr, None), carry)
        carry = lax.fori_loop(0, (i * nsub - t0) // nsub, lambda b, cr: tiles(t0 + odd + b * nsub, nsub, cr, None), carry)
        carry = tiles(i * nsub, nsub, carry, masks)
        dq_ref[...] = jnp.where(first, carry[0][0], carry[1][0]) * LN2

        @pl.when(i == pl.num_programs(1) - 1)
        def _():
            def flush(j, _):
                rows = pl.ds(pl.multiple_of(j * tk, tk), tk)
                stage[0] = dkt_acc[j].T * LN2
                stage[1] = dvt_acc[j].T
                ck = pltpu.make_async_copy(stage.at[0], dk_hbm.at[hp, rows], sem.at[0])
                cv = pltpu.make_async_copy(stage.at[1], dv_hbm.at[hp, rows], sem.at[1])
                ck.start()
                cv.start()
                ck.wait()
                cv.wait()
                return 0
            lax.fori_loop(0, nkb, flush, 0)

    q_spec, k_spec, v_spec = _attn_specs(S, tq, nb)
    slab = SDS((nb, S, LANES), F32)
    return pl.pallas_call(
        body, name="attn_bwd", grid=(nb, S // tq),
        in_specs=[pl.BlockSpec(memory_space=pltpu.SMEM), q_spec, k_spec, v_spec, q_spec, q_spec],
        out_specs=[pl.BlockSpec((None, tq, LANES), lambda hp, i: (hp, i, 0)), ANY, ANY], out_shape=[slab, slab, slab],
        scratch_shapes=[pltpu.VMEM((nkb, LANES, tk), F32), pltpu.VMEM((nkb, LANES, tk), F32), pltpu.VMEM((2, tk, LANES), F32),
                        pltpu.SemaphoreType.DMA((2,))],
        compiler_params=_params("arbitrary", "arbitrary"),
    )(start, qkv, qkv, qkv, do, tot)


def _trail_sum(ext, bufs, cols, levels, n):
    def src(lo, size):
        return ext[pl.ds(lo, size), cols]
    for l in range(levels):
        lo = 8 * (l + 1)
        dst = bufs[l % 2]
        dst[lo:, :] = src(lo, n - lo) + src(lo - (1 << l), n - lo)
        def src(lo_, size, d=dst):
            return d[pl.ds(lo_, size), :]
    return src(HALO, n - HALO)


def _lead_sum(ext, bufs, cols, levels, n):
    def src(lo, size):
        return ext[pl.ds(lo, size), cols]
    for l in range(levels):
        hi = n - 8 * (l + 1)
        dst = bufs[l % 2]
        dst[0:hi, :] = src(0, hi) + src(1 << l, hi)
        def src(lo_, size, d=dst):
            return d[pl.ds(lo_, size), :]
    return src(0, n - HALO)


def _pool_diffs(x_ref, xh_ref, vec_ref, ext, bufs, tm, D):
    i = pl.program_id(0)
    gs, shift = vec_ref[0:1, :], vec_ref[1:2, :]
    n, r = _rms(x_ref[...])
    nh, _ = _rms(xh_ref[...])
    ext[0:HALO, :] = jnp.where(i > 0, nh * gs + shift, 0.0)
    ext[HALO:, :] = n * gs + shift
    t = i * tm + lax.broadcasted_iota(jnp.int32, (tm, 1), 0)
    dg = D // len(POOL_LEVELS)
    out = []
    for g, lv in enumerate(POOL_LEVELS):
        cols = slice(g * dg, (g + 1) * dg)
        inv = 1.0 / jnp.minimum(t + 1, 1 << lv).astype(F32)
        out.append((_trail_sum(ext, bufs, cols, lv, HALO + tm) * inv - ext[HALO:, cols], inv))
    return out, n, r


def _pool_fwd(x, vec, pw):
    S, D = x.shape
    ng, dg, _ = pw.shape
    tm = _tile(S, TOKEN_TILE)

    def body(x_ref, xh_ref, vec_ref, pw_ref, xo_ref, ext, buf_a, buf_b):
        diffs, _, _ = _pool_diffs(x_ref, xh_ref, vec_ref, ext, (buf_a, buf_b), tm, D)
        for g, (d, _) in enumerate(diffs):
            cols = slice(g * dg, (g + 1) * dg)
            y = (_dot(d.astype(BF16), pw_ref[g]) + vec_ref[4:5, cols]) * vec_ref[3:4, cols]
            xo_ref[:, cols] = x_ref[:, cols] + vec_ref[2:3, cols] * y

    return pl.pallas_call(
        body, name="pool_fwd", grid=(S // tm,),
        in_specs=[_rows(tm, D), _prev_halo(tm, D), _resident((8, D)), _resident(pw.shape)],
        out_specs=_rows(tm, D), out_shape=SDS((S, D), F32),
        scratch_shapes=[pltpu.VMEM((HALO + tm, D), F32), pltpu.VMEM((HALO + tm, dg), F32), pltpu.VMEM((HALO + tm, dg), F32)],
        compiler_params=_params("parallel"),
    )(x, x, vec, pw)


def _pool_bwd(x, dxo, vec, pw):
    S, D = x.shape
    ng, dg, _ = pw.shape
    tm = _tile(S, TOKEN_TILE)

    def body(x_ref, xh_ref, dxo_ref, dxn_ref, vec_ref, pw_ref, dx_ref, gpw_ref, sums_ref, ext, exte, buf_a, buf_b):
        _zero_at_first_step(gpw_ref, sums_ref)
        i = pl.program_id(0)
        bufs = (buf_a, buf_b)
        diffs, n, r = _pool_diffs(x_ref, xh_ref, vec_ref, ext, bufs, tm, D)
        gate, scale = vec_ref[2:3, :], vec_ref[3:4, :]
        dxo = dxo_ref[...]
        dyp_next = jnp.where(i < pl.num_programs(0) - 1, dxn_ref[...], 0.0) * gate * scale
        t_next = (i + 1) * tm + lax.broadcasted_iota(jnp.int32, (HALO, 1), 0)
        for g, (d, inv) in enumerate(diffs):
            cols = slice(g * dg, (g + 1) * dg)
            w = pw_ref[g]
            db = d.astype(BF16)
            ypre = _dot(db, w) + vec_ref[4:5, cols]
            dy = gate[:, cols] * dxo[:, cols]
            sums_ref[2:3, cols] += jnp.sum(dxo[:, cols] * ypre * scale[:, cols], axis=0, keepdims=True)
            sums_ref[3:4, cols] += jnp.sum(dy * ypre, axis=0, keepdims=True)
            dyp = dy * scale[:, cols]
            sums_ref[4:5, cols] += jnp.sum(dyp, axis=0, keepdims=True)
            dypb = dyp.astype(BF16)
            gpw_ref[g] += _dot_tn(db, dypb)
            dd = _dot_nt(dypb, w)
            dd_next = _dot_nt(dyp_next[:, cols].astype(BF16), w)
            inv_next = 1.0 / jnp.minimum(t_next + 1, 1 << POOL_LEVELS[g]).astype(F32)
            exte[0:tm, cols] = dd * inv
            exte[tm:, cols] = dd_next * inv_next
            ext[HALO:, cols] = _lead_sum(exte, bufs, cols, POOL_LEVELS[g], tm + HALO) - dd
        dx_ref[...] = _modulate_bwd(ext[HALO:, :], n, r, vec_ref[0:1, :], dxo, sums_ref)

    return pl.pallas_call(
        body, name="pool_bwd", grid=(S // tm,),
        in_specs=[_rows(tm, D), _prev_halo(tm, D), _rows(tm, D), _next_halo(tm, D, S), _resident((8, D)), _resident(pw.shape)],
        out_specs=[_rows(tm, D), pl.BlockSpec(pw.shape, lambda i: (0, 0, 0)), _acc_spec(8, D)],
        out_shape=[SDS((S, D), F32), SDS(pw.shape, F32), SDS((8, D), F32)],
        scratch_shapes=[pltpu.VMEM((HALO + tm, D), F32), pltpu.VMEM((tm + HALO, D), F32),
                        pltpu.VMEM((HALO + tm, dg), F32), pltpu.VMEM((HALO + tm, dg), F32)],
        compiler_params=_params("arbitrary"),
    )(x, x, dxo, dxo, vec, pw)


def _loss_head(y, target):
    S, D = y.shape
    tm = _tile(S, TOKEN_TILE)

    def body(y_ref, t_ref, dy_ref, sums_ref):
        _zero_at_first_step(sums_ref)
        err = y_ref[...] - t_ref[...]
        _add_rowsum(sums_ref, 0, err * err)
        dy_ref[...] = err * (1.0 / D)

    return pl.pallas_call(
        body, name="loss_head", grid=(S // tm,), in_specs=[_rows(tm, D), _rows(tm, D)],
        out_specs=[_rows(tm, D), _acc_spec(8, D)], out_shape=[SDS((S, D), F32), SDS((8, D), F32)],
        compiler_params=_params("arbitrary"),
    )(y, target)


def _cond_pre(c_cols, cond_w):
    def body(c_ref, w_ref, o_ref):
        o_ref[...] = _hdot(c_ref[...], w_ref[...])
    return pl.pallas_call(body, name="cond_pre", out_shape=SDS((c_cols.shape[0], cond_w.shape[1]), F32))(c_cols, cond_w)


def _cond_e(parts, cond_b):
    def body(p_ref, b_ref, pre_ref, e_ref):
        pre = p_ref[0] + p_ref[2] + p_ref[4] + p_ref[6] + b_ref[...]
        pre_ref[...] = pre
        e_ref[...] = pre * jax.nn.sigmoid(pre)
    shape = SDS(parts.shape[1:], F32)
    return pl.pallas_call(body, name="cond_e", out_shape=[shape, shape])(parts, cond_b)


def _mod_cols(e, ada_w, ada_b_cols):
    L, D, nc = ada_w.shape
    B = e.shape[0]

    def body(e_ref, w_ref, b_ref, o_ref):
        o_ref[...] = _hdot(e_ref[...], w_ref[...]) + b_ref[...]

    return pl.pallas_call(
        body, name="mod_cols", grid=(L,),
        in_specs=[pl.BlockSpec((B, D), lambda l: (0, 0)), pl.BlockSpec((None, D, nc), lambda l: (l, 0, 0)),
                  pl.BlockSpec((None, 1, nc), lambda l: (l, 0, 0))],
        out_specs=pl.BlockSpec((None, B, nc), lambda l: (l, 0, 0)), out_shape=SDS((L, B, nc), F32),
        compiler_params=_params("parallel"),
    )(e, ada_w, ada_b_cols)


def _mod_bwd(e, dmod_cols, ada_w):
    L, D, nc = ada_w.shape
    B = e.shape[0]

    def body(e_ref, d_ref, w_ref, gw_ref, de_ref):
        _zero_at_first_step(de_ref)
        gw_ref[...] = _hdot(e_ref[...], d_ref[...], (((0,), (0,)), ((), ())))
        de_ref[...] += _hdot(d_ref[...], w_ref[...], (((1,), (1,)), ((), ())))

    return pl.pallas_call(
        body, name="mod_bwd", grid=(L,),
        in_specs=[pl.BlockSpec((B, D), lambda l: (0, 0)), pl.BlockSpec((None, B, nc), lambda l: (l, 0, 0)),
                  pl.BlockSpec((None, D, nc), lambda l: (l, 0, 0))],
        out_specs=[pl.BlockSpec((None, D, nc), lambda l: (l, 0, 0)), pl.BlockSpec((B, D), lambda l: (0, 0))],
        out_shape=[SDS((L, D, nc), F32), SDS((B, D), F32)], compiler_params=_params("arbitrary"),
    )(e, dmod_cols, ada_w)


def _cond_bwd(de_parts, pre, c_cols):
    def body(p_ref, pre_ref, c_ref, gw_ref, gb_ref):
        pre = pre_ref[...]
        dpre = (p_ref[0] + p_ref[2] + p_ref[4] + p_ref[6]) * _silu_grad(pre, jax.nn.sigmoid(pre))
        gb_ref[...] = jnp.sum(dpre, axis=0, keepdims=True)
        gw_ref[...] = _hdot(c_ref[...], dpre, (((0,), (0,)), ((), ())))
    D = pre.shape[1]
    return pl.pallas_call(body, name="cond_bwd", out_shape=[SDS((c_cols.shape[1], D), F32), SDS((1, D), F32)])(de_parts, pre, c_cols)


def _as_rows(a):
    return a.reshape(-1, a.shape[-1])


def _row_tile(rows, width, n_arrays):
    t = max(8, (ADAM_TILE_BYTES // (4 * width)) // 8 * 8)
    while rows % t:
        t -= 8
        if t <= 0:
            return rows
    return t


def _adamw(w, g, m, v):
    shape = w.shape
    w, g, m, v = (_as_rows(a) for a in (w, g, m, v))
    R, C = w.shape
    tr = _row_tile(R, C, 7)

    def body(w_ref, g_ref, m_ref, v_ref, d_ref, nm_ref, nv_ref):
        g = g_ref[...]
        m = ADAM_B1 * m_ref[...] + (1.0 - ADAM_B1) * g
        v = ADAM_B2 * v_ref[...] + (1.0 - ADAM_B2) * (g * g)
        m_hat = m / (1.0 - ADAM_B1 ** ADAM_STEP)
        v_hat = v / (1.0 - ADAM_B2 ** ADAM_STEP)
        d_ref[...] = -ADAM_LR * (m_hat / (jnp.sqrt(v_hat) + ADAM_EPS) + ADAM_WD * w_ref[...])
        nm_ref[...] = m
        nv_ref[...] = v

    spec = _rows(tr, C)
    outs = pl.pallas_call(
        body, name="adamw", grid=(R // tr,), in_specs=[spec] * 4, out_specs=[spec] * 3,
        out_shape=[SDS((R, C), F32)] * 3, compiler_params=_params("parallel"),
    )(w, g, m, v)
    return tuple(o.reshape(shape) for o in outs)


def _sum_lead(a, out_dtype=F32):
    n = a.shape[0]
    shape = a.shape[1:]
    a = a.reshape(n, -1, a.shape[-1])
    _, R, C = a.shape
    tr = _row_tile(R, C, n + 1)

    def body(a_ref, o_ref):
        acc = a_ref[0].astype(F32)
        for k in range(1, n):
            acc = acc + a_ref[k].astype(F32)
        o_ref[...] = acc.astype(out_dtype)

    out = pl.pallas_call(
        body, name="sum_lead", grid=(R // tr,), in_specs=[pl.BlockSpec((n, tr, C), lambda i: (0, i, 0))],
        out_specs=_rows(tr, C), out_shape=SDS((R, C), out_dtype), compiler_params=_params("parallel"),
    )(a)
    return out.reshape(shape)


def _place():
    return lax.axis_index("x"), lax.axis_index("y"), lax.axis_index("c")


def _allgather8(a):
    def body(a_ref, o_ref, send, recv, local):
        mx, my, mc = _place()
        me = 4 * mx + 2 * my + mc
        mine = pltpu.make_async_copy(a_ref, o_ref.at[me], local)
        mine.start()
        copies = []
        for k in range(1, N_DEV):
            peer = (1 - mx if k & 4 else mx, 1 - my if k & 2 else my, 1 - mc if k & 1 else mc)
            cp = pltpu.make_async_remote_copy(a_ref, o_ref.at[me], send.at[k - 1], recv.at[k - 1], device_id=peer, device_id_type=MESH)
            cp.start()
            copies.append(cp)
        for cp in copies:
            cp.wait()
        mine.wait()

    return pl.pallas_call(
        body, name="allgather8", in_specs=[ANY], out_specs=ANY, out_shape=SDS((N_DEV,) + a.shape, a.dtype),
        scratch_shapes=[pltpu.SemaphoreType.DMA((N_DEV - 1,)), pltpu.SemaphoreType.DMA((N_DEV - 1,)), pltpu.SemaphoreType.DMA],
    )(a)


def _grad_scatter(arrs):
    n = len(arrs)
    out_shape = [SDS((N_CHIPS, a.shape[0]) + a.shape[2:], a.dtype) for a in arrs]

    def body(*refs):
        ins, outs, (send, recv, local) = refs[:n], refs[n:2 * n], refs[2 * n:]
        mx, my, mc = _place()
        chip = 2 * mx + my
        copies = []
        for a in range(n):
            cp = pltpu.make_async_copy(ins[a].at[:, chip], outs[a].at[chip], local.at[a])
            cp.start()
            copies.append(cp)
        for k in range(1, N_CHIPS):
            px, py = (1 - mx if k & 2 else mx), (1 - my if k & 1 else my)
            for a in range(n):
                s = (k - 1) * n + a
                cp = pltpu.make_async_remote_copy(ins[a].at[:, 2 * px + py], outs[a].at[chip], send.at[s], recv.at[s],
                                                  device_id=(px, py, mc), device_id_type=MESH)
                cp.start()
                copies.append(cp)
        for cp in copies:
            cp.wait()

    return pl.pallas_call(
        body, name="grad_scatter", in_specs=[ANY] * n, out_specs=[ANY] * n, out_shape=out_shape,
        scratch_shapes=[pltpu.SemaphoreType.DMA((3 * n,)), pltpu.SemaphoreType.DMA((3 * n,)), pltpu.SemaphoreType.DMA((n,))],
    )(*arrs)


def _weight_gather(arrs):
    n = len(arrs)

    def body(*refs):
        ins, outs, (ici_send, ici_recv, d2d_send, d2d_recv, local) = refs[:n], refs[n:2 * n], refs[2 * n:]
        mx, my, mc = _place()
        chip = 2 * mx + my
        own = []
        for a in range(n):
            cp = pltpu.make_async_copy(ins[a], outs[a].at[chip], local.at[a])
            cp.start()
            own.append(cp)
        peers = [(1 - mx if k & 2 else mx, 1 - my if k & 1 else my) for k in range(1, N_CHIPS)]
        fetched = []
        for k, (px, py) in enumerate(peers):
            for a in range(n):
                s = k * n + a
                cp = pltpu.make_async_remote_copy(ins[a].at[:, mc], outs[a].at[chip, :, mc], ici_send.at[s], ici_recv.at[s],
                                                  device_id=(px, py, mc), device_id_type=MESH)
                cp.start()
                fetched.append(cp)
        passed = []
        for k, (px, py) in enumerate(peers):
            for a in range(n):
                s = k * n + a
                fetched[s].wait_recv()
                half = outs[a].at[2 * px + py, :, mc]
                cp = pltpu.make_async_remote_copy(half, half, d2d_send.at[s], d2d_recv.at[s],
                                                  device_id=(mx, my, 1 - mc), device_id_type=MESH)
                cp.start()
                passed.append(cp)
        for cp in fetched:
            cp.wait_send()
        for cp in passed + own:
            cp.wait()

    sems = pltpu.SemaphoreType.DMA((3 * n,))
    return pl.pallas_call(
        body, name="weight_gather", in_specs=[ANY] * n, out_specs=[ANY] * n,
        out_shape=[SDS((N_CHIPS,) + a.shape, a.dtype) for a in arrs],
        scratch_shapes=[sems, sems, sems, sems, pltpu.SemaphoreType.DMA((n,))],
    )(*arrs)


def _sibling_send(arrs, halves, name):
    n = len(arrs)
    out_shape = [SDS(a.shape[:2] + a.shape[3:] if halves else a.shape, a.dtype) for a in arrs]

    def body(*refs):
        ins, outs, (send, recv) = refs[:n], refs[n:2 * n], refs[2 * n:]
        mx, my, mc = _place()
        copies = []
        for a in range(n):
            cp = pltpu.make_async_remote_copy(ins[a].at[:, :, 1 - mc] if halves else ins[a], outs[a], send.at[a], recv.at[a],
                                              device_id=(mx, my, 1 - mc), device_id_type=MESH)
            cp.start()
            copies.append(cp)
        for cp in copies:
            cp.wait()

    return pl.pallas_call(
        body, name=name, in_specs=[ANY] * n, out_specs=[ANY] * n, out_shape=out_shape,
        scratch_shapes=[pltpu.SemaphoreType.DMA((n,)), pltpu.SemaphoreType.DMA((n,))],
    )(*arrs)


def _add_cast(a, b, dtype):
    shape = a.shape
    a, b = _as_rows(a), _as_rows(b)
    R, C = a.shape
    tr = _row_tile(R, C, 3)

    def body(a_ref, b_ref, o_ref):
        o_ref[...] = (a_ref[...] + b_ref[...]).astype(dtype)

    out = pl.pallas_call(body, name="pair_sum", grid=(R // tr,), in_specs=[_rows(tr, C)] * 2, out_specs=_rows(tr, C),
                         out_shape=SDS((R, C), dtype), compiler_params=_params("parallel"))(a, b)
    return out.reshape(shape)


def _pack(arrs):
    flat = jnp.concatenate([a.reshape(-1).astype(F32) for a in arrs])
    pad = (-flat.shape[0]) % (16 * LANES)
    return jnp.pad(flat, (0, pad)).reshape(-1, LANES)


def _unpack(buf, shapes):
    flat = buf.reshape(buf.shape[:-2] + (-1,))
    out, off = [], 0
    for s in shapes:
        size = 1
        for d in s:
            size *= d
        out.append(flat[..., off:off + size].reshape(flat.shape[:-1] + tuple(s)))
        off += size
    return out


def _unshard(stacked, axis):
    moved = jnp.moveaxis(stacked, 0, axis)
    return moved.reshape(moved.shape[:axis] + (N_CHIPS * moved.shape[axis + 1],) + moved.shape[axis + 2:])


def _my_shard(full, axis, chip):
    size = full.shape[axis] // N_CHIPS
    return lax.dynamic_slice_in_dim(full, chip * size, size, axis)


def kernel(x, c, cond_w, cond_b, ada_w, ada_b, norm_g, ffn_w1, ffn_w3, ffn_w2, a_w_in, a_b_in, a_dw, a_dw_b, a_ln_g, a_ln_b, a_w_out, a_b_out, b_w_qkv, b_q_g, b_k_g, b_w_o, p_w, p_b, p_scale, loss_target, m_cond_w, m_cond_b, m_ada_w, m_ada_b, m_norm_g, m_ffn_w1, m_ffn_w3, m_ffn_w2, m_a_w_in, m_a_b_in, m_a_dw, m_a_dw_b, m_a_ln_g, m_a_ln_b, m_a_w_out, m_a_b_out, m_b_w_qkv, m_b_q_g, m_b_k_g, m_b_w_o, m_p_w, m_p_b, m_p_scale, v_cond_w, v_cond_b, v_ada_w, v_ada_b, v_norm_g, v_ffn_w1, v_ffn_w3, v_ffn_w2, v_a_w_in, v_a_b_in, v_a_dw, v_a_dw_b, v_a_ln_g, v_a_ln_b, v_a_w_out, v_a_b_out, v_b_w_qkv, v_b_q_g, v_b_k_g, v_b_w_o, v_p_w, v_p_b, v_p_scale):
    w_in = dict(cond_w=cond_w, cond_b=cond_b, ada_w=ada_w, ada_b=ada_b, norm_g=norm_g, ffn_w1=ffn_w1, ffn_w3=ffn_w3, ffn_w2=ffn_w2,
                a_w_in=a_w_in, a_b_in=a_b_in, a_dw=a_dw, a_dw_b=a_dw_b, a_ln_g=a_ln_g, a_ln_b=a_ln_b, a_w_out=a_w_out, a_b_out=a_b_out,
                b_w_qkv=b_w_qkv, b_q_g=b_q_g, b_k_g=b_k_g, b_w_o=b_w_o, p_w=p_w, p_b=p_b, p_scale=p_scale)
    m_in = dict(zip(WEIGHTS, (m_cond_w, m_cond_b, m_ada_w, m_ada_b, m_norm_g, m_ffn_w1, m_ffn_w3, m_ffn_w2, m_a_w_in, m_a_b_in, m_a_dw,
                              m_a_dw_b, m_a_ln_g, m_a_ln_b, m_a_w_out, m_a_b_out, m_b_w_qkv, m_b_q_g, m_b_k_g, m_b_w_o, m_p_w, m_p_b, m_p_scale)))
    v_in = dict(zip(WEIGHTS, (v_cond_w, v_cond_b, v_ada_w, v_ada_b, v_norm_g, v_ffn_w1, v_ffn_w3, v_ffn_w2, v_a_w_in, v_a_b_in, v_a_dw,
                              v_a_dw_b, v_a_ln_g, v_a_ln_b, v_a_w_out, v_a_b_out, v_b_w_qkv, v_b_q_g, v_b_k_g, v_b_w_o, v_p_w, v_p_b, v_p_scale)))
    x = x[0]
    target = loss_target[0]
    S, D = x.shape
    L = ada_w.shape[0]
    assert b_q_g.shape[-1] == HEAD_DIM and D % LANES == 0 and S % HALO == 0
    mx, my, mc = _place()
    chip = 2 * mx + my
    me = 2 * chip + mc

    big = [ffn_w1.astype(BF16), ffn_w3.astype(BF16), ffn_w2.astype(BF16), a_w_in.astype(BF16), a_w_out.astype(BF16),
           b_w_qkv.astype(BF16), b_w_o.astype(BF16)]
    sharded = [k for k, ax in SMALL.items() if ax is not None]
    shards = big + [_pack([w_in[k] for k in sharded])]
    by_half = [a.reshape((-1, 2, a.shape[-2] // 2, a.shape[-1])) for a in shards]
    gathered = [g.reshape((N_CHIPS,) + a.shape) for g, a in zip(_weight_gather(by_half), shards)]
    g_w1, g_w3, g_w2, g_ain, g_aout, g_qkv, g_wo, g_small = gathered
    full = {k: _unshard(a, SMALL[k]) for k, a in zip(sharded, _unpack(g_small, [w_in[k].shape for k in sharded]))}
    n_conv, n_pool = a_w_in.shape[0], p_w.shape[0]
    conv_dw = [jnp.pad(full['a_dw'][ia], ((0, 1), (0, 0))) for ia in range(n_conv)]
    pool_w = [full['p_w'][ic].astype(BF16) for ic in range(n_pool)]
    qk_scale = LOG2E * HEAD_DIM ** -0.5
    gq = jnp.tile(b_q_g[0], D // HEAD_DIM).reshape(1, D) * qk_scale
    gk = jnp.tile(b_k_g[0], D // HEAD_DIM).reshape(1, D)

    c_all = _allgather8(c)[:, 0, :]
    c_cols = _my_shard(c_all, 1, chip)
    pre, e = _cond_e(_allgather8(_cond_pre(c_cols, cond_w)), cond_b.reshape(1, D))
    nc = ada_w.shape[2]
    mod_c = _mod_cols(e, ada_w, _my_shard(ada_b, 1, chip).reshape(L, 1, nc))
    mod_all = _allgather8(mod_c.reshape(L * N_DEV, nc)).reshape(N_CHIPS, 2, L, N_DEV, nc)[:, 0]
    mod = jnp.moveaxis(lax.dynamic_index_in_dim(mod_all, me, axis=2, keepdims=False), 0, 1).reshape(L, 3, 3, D)
    shift, scale, gate = mod[:, :, 0], mod[:, :, 1], 1.0 + mod[:, :, 2]
    gains = full['norm_g']

    def mod_vec(i, k, gate_factor=1.0):
        return _vec(gains[i, k] * (1.0 + scale[i, k]), shift[i, k], gate_factor * gate[i, k])

    saved = []
    ia = ib = ic = 0
    for i in range(L):
        for k, half in ((0, 0), (1, None), (2, 1)):
            if half is not None:
                vec = mod_vec(i, k, 0.5)
                w1, w3, w2 = g_w1[:, i, half], g_w3[:, i, half], g_w2[:, i, half]
                xo, a1, a3, y = _ffn_fwd(x, vec, w1, w3, w2)
                saved.append(('ffn', i, k, half, x, vec, (a1, a3, y, w1, w3, w2)))
            elif i % 3 == 0:
                vec = mod_vec(i, k)
                w_a = g_ain[:, ia]
                w_o = g_aout[:, ia].reshape(D, D)
                p, u = _in_fwd(x, vec, w_a, full['a_b_in'][ia].reshape(1, 2 * D), True, "conv_in_fwd")
                cvec = _vec(gate[i, k], full['a_dw_b'][ia], full['a_ln_g'][ia], full['a_ln_b'][ia], full['a_b_out'][ia])
                xo, v, y = _conv_mid_fwd(u, x, cvec, conv_dw[ia], w_o)
                saved.append(('conv', i, k, ia, x, vec, (p, u, v, y, cvec, w_a, w_o)))
                ia += 1
            elif i % 3 == 1:
                vec = mod_vec(i, k)
                w_q = g_qkv[:, ib]
                w_o = g_wo[:, ib].reshape(D, D)
                raw, = _in_fwd(x, vec, w_q, jnp.zeros((1, 3 * D), F32), False, "attn_in_fwd")
                qkv = _qknorm_fwd(raw, gq, gk)
                o, tot, start = _attn_fwd(qkv)
                xo, y = _out_fwd(x, o, vec, w_o)
                saved.append(('attn', i, k, ib, x, vec, (raw, qkv, o, (tot, start), y, w_q, w_o)))
                ib += 1
            else:
                vec = _vec(gains[i, k] * (1.0 + scale[i, k]), shift[i, k], gate[i, k], full['p_scale'][ic], full['p_b'][ic].reshape(D))
                xo = _pool_fwd(x, vec, pool_w[ic])
                saved.append(('pool', i, k, ic, x, vec, ()))
                ic += 1
            x = xo

    dx, sq = _loss_head(x, target)
    loss = lax.psum(0.5 / D * jnp.sum(sq[0]), ("x", "y", "c"))

    zeros_like_full = lambda k: jnp.zeros(full[k].shape, F32)
    g_full = {k: zeros_like_full(k) for k in sharded}
    g_full['b_q_g'] = jnp.zeros_like(b_q_g)
    g_full['b_k_g'] = jnp.zeros_like(b_k_g)
    dmod = jnp.zeros((L, 3, 3, D), F32)
    fc = ffn_w1.shape[-1]
    n_attn = b_w_qkv.shape[0]
    big_grads = {'ffn_w1': jnp.zeros((2 * L, N_CHIPS, D, fc), F32), 'ffn_w3': jnp.zeros((2 * L, N_CHIPS, D, fc), F32),
                 'ffn_w2': jnp.zeros((2 * L, N_CHIPS, fc, D), F32), 'a_w_in': jnp.zeros((n_conv, N_CHIPS, D, 2 * D // N_CHIPS), F32),
                 'a_w_out': jnp.zeros((n_conv, 1, D, D), F32), 'b_w_qkv': jnp.zeros((n_attn, N_CHIPS, D, 3 * D // N_CHIPS), F32),
                 'b_w_o': jnp.zeros((n_attn, 1, D, D), F32)}

    def wgrad(name, slot, a, b, a_mode, b_mode, nch):
        big_grads[name] = _wgrad(a, b, a_mode, b_mode, nch, name + "_grad", big_grads[name], slot)

    def put(name, idx, val):
        g_full[name] = g_full[name].at[idx].set(val.reshape(g_full[name][idx].shape))

    for kind, i, k, idx, xin, vec, res in reversed(saved):
        if kind == 'ffn':
            a1, a3, y, w1, w3, w2 = res
            dx, h, dy, u, da1, da3, sums = _ffn_bwd(xin, dx, a1, a3, y, vec, w1, w3, w2)
            wgrad('ffn_w1', 2 * i + idx, h, da1, 'full', 'lead', N_CHIPS)
            wgrad('ffn_w3', 2 * i + idx, h, da3, 'full', 'lead', N_CHIPS)
            wgrad('ffn_w2', 2 * i + idx, u, dy, 'lead', 'full', N_CHIPS)
            dgate = 0.5 * sums[2]
        elif kind == 'conv':
            p, u, v, y, cvec, w_a, w_o = res
            dv, q, dout, csums = _conv_mid_bwd(dx, y, v, cvec, w_o)
            wgrad('a_w_out', idx, q, dout, 'full', 'full', 1)
            du, gdw = _conv_transpose(dv, u, conv_dw[idx])
            dp, psums = _glu_bwd(du, p)
            dx, h, sums = _in_bwd(xin, dx, dp, vec, w_a, "conv_in_bwd")
            wgrad('a_w_in', idx, h, dp, 'full', 'col', N_CHIPS)
            dgate = csums[0]
            put('a_b_out', idx, csums[1])
            put('a_ln_g', idx, csums[2])
            put('a_ln_b', idx, csums[3])
            put('a_dw_b', idx, csums[4])
            put('a_dw', idx, gdw[:-1])
            put('a_b_in', idx, psums[0])
        elif kind == 'attn':
            raw, qkv, o, tot, y, w_q, w_o = res
            do, dout, osums = _out_bwd(dx, y, vec, w_o)
            wgrad('b_w_o', idx, o, dout, 'full', 'full', 1)
            dq, dk, dvv = _attn_bwd(qkv, do, *tot)
            draw, qsums = _qknorm_bwd(dq, dk, dvv, raw, gq, gk)
            dx, h, sums = _in_bwd(xin, dx, draw, vec, w_q, "attn_in_bwd")
            wgrad('b_w_qkv', idx, h, draw, 'full', 'col', N_CHIPS)
            dgate = osums[2]
            put('b_q_g', idx, qk_scale * jnp.sum(qsums[0].reshape(-1, HEAD_DIM), axis=0))
            put('b_k_g', idx, jnp.sum(qsums[1].reshape(-1, HEAD_DIM), axis=0))
        else:
            dx, gpw, sums = _pool_bwd(xin, dx, vec, pool_w[idx])
            dgate = sums[2]
            put('p_w', idx, gpw)
            put('p_scale', idx, sums[3])
            put('p_b', idx, sums[4])
        put('norm_g', (i, k), sums[0] * (1.0 + scale[i, k]))
        dmod = dmod.at[i, k].set(jnp.stack([sums[1], sums[0] * gains[i, k], dgate]))
    grad_x = dx[None]

    dmod_all = _allgather8(dmod.reshape(L, 9 * D))
    g_ada_b = _sum_lead(dmod_all)
    dmod_cols = jnp.moveaxis(_my_shard(dmod_all, 2, chip), 0, 1)
    g_ada_w, de_part = _mod_bwd(e, dmod_cols, ada_w)
    g_cond_w, g_cond_b = _cond_bwd(_allgather8(de_part), pre, c_cols)

    small_names = list(SMALL)
    g_full['cond_b'] = g_cond_b.reshape(D)
    g_full['ada_b'] = g_ada_b
    reduced = [k for k in small_names if k not in ('cond_b', 'ada_b')]
    red = _unpack(_sum_lead(_allgather8(_pack([g_full[k] for k in reduced]))), [g_full[k].shape for k in reduced])
    for k, a in zip(reduced, red):
        g_full[k] = a
    grads = {k: (g_full[k] if SMALL[k] is None else _my_shard(g_full[k], SMALL[k], chip)) for k in small_names}
    grads['cond_w'] = g_cond_w
    grads['ada_w'] = g_ada_w

    big_names = list(big_grads)
    by_half = []
    for name in big_names:
        g = big_grads[name]
        rows = g.shape[1] * g.shape[2] // N_CHIPS
        by_half.append(g.reshape(g.shape[0], N_CHIPS, 2, rows // 2, g.shape[3]))
    got = _sibling_send(by_half, True, "pair_send")
    pair = [_add_cast(lax.dynamic_index_in_dim(a, mc, axis=2, keepdims=False), b, BF16) for a, b in zip(by_half, got)]
    mine = [_sum_lead(a) for a in _grad_scatter(pair)]
    theirs = _sibling_send(mine, False, "pair_return")
    for name, a, b in zip(big_names, mine, theirs):
        both = jnp.where(mc == 0, jnp.stack([a, b], axis=1), jnp.stack([b, a], axis=1))
        grads[name] = both.reshape(w_in[name].shape)

    delta, new_m, new_v = {}, {}, {}
    packed = [_pack([d[k] for k in small_names]) for d in (w_in, grads, m_in, v_in)]
    shapes = [w_in[k].shape for k in small_names]
    for out, buf in zip((delta, new_m, new_v), _adamw(*packed)):
        out.update(zip(small_names, _unpack(buf, shapes)))
    for k in WEIGHTS:
        if k not in SMALL:
            delta[k], new_m[k], new_v[k] = _adamw(w_in[k], grads[k], m_in[k], v_in[k])
    return (loss, grad_x, *[grads[k] for k in WEIGHTS], *[delta[k] for k in WEIGHTS], *[new_m[k] for k in WEIGHTS],
            *[new_v[k] for k in WEIGHTS])
```

```python
import functools

import jax
import jax.numpy as jnp
from jax import lax
from jax.experimental import pallas as pl
from jax.experimental.pallas import tpu as pltpu

F32 = jnp.float32
BF16 = jnp.bfloat16
SDS = jax.ShapeDtypeStruct
MESH = pl.DeviceIdType.MESH
ANY = pl.BlockSpec(memory_space=pl.ANY)

EPS = 1e-6
N_CHIPS = 4
N_DEV = 8
LANES = 128
HEAD_DIM = 64
VMEM_LIMIT_BYTES = 56 * 2**20
TOKEN_TILE = 512
WGRAD_TILE = 1024
ATTN_Q_TILE = 1024
ATTN_K_TILE = 256
ATTN_DEAD_BITS = 160.0
LOG2E = 1.4426950408889634
LN2 = 0.6931471805599453
HALO = 32
CONV_ROWS, CONV_COLS = 32, 256
ADAM_TILE_BYTES = 1 << 20
POOL_LEVELS = (1, 2, 3, 4)

ADAM_LR, ADAM_B1, ADAM_B2, ADAM_EPS, ADAM_WD, ADAM_STEP = 0.001, 0.9, 0.999, 1e-08, 0.01, 10

WEIGHTS = ['cond_w', 'cond_b', 'ada_w', 'ada_b', 'norm_g', 'ffn_w1', 'ffn_w3', 'ffn_w2', 'a_w_in', 'a_b_in', 'a_dw',
           'a_dw_b', 'a_ln_g', 'a_ln_b', 'a_w_out', 'a_b_out', 'b_w_qkv', 'b_q_g', 'b_k_g', 'b_w_o', 'p_w', 'p_b', 'p_scale']
SMALL = {'norm_g': 2, 'a_b_in': 1, 'a_dw': 2, 'a_dw_b': 1, 'a_ln_g': 1, 'a_ln_b': 1, 'a_b_out': 1, 'p_w': 2, 'p_b': 2,
         'p_scale': 1, 'cond_b': None, 'ada_b': None, 'b_q_g': None, 'b_k_g': None}


def _tile(n, pref):
    return pref if n % pref == 0 else n


def _params(*sem):
    return pltpu.CompilerParams(dimension_semantics=sem, vmem_limit_bytes=VMEM_LIMIT_BYTES)


def _resident(shape):
    nd = len(shape)
    return pl.BlockSpec(shape, lambda *_: (0,) * nd, pipeline_mode=pl.Buffered(1))


def _rows(tm, width):
    return pl.BlockSpec((tm, width), lambda i: (i, 0))


def _acc_spec(rows, width):
    return pl.BlockSpec((rows, width), lambda i: (0, 0))


def _dot(a, b):
    return jnp.dot(a, b, preferred_element_type=F32)


def _dot_nt(a, b):
    return lax.dot_general(a, b, (((1,), (1,)), ((), ())), preferred_element_type=F32)


def _dot_tn(a, b):
    return lax.dot_general(a, b, (((0,), (0,)), ((), ())), preferred_element_type=F32)


def _hdot(a, b, dims=(((1,), (0,)), ((), ()))):
    return lax.dot_general(a, b, dims, preferred_element_type=F32, precision=lax.Precision.HIGHEST)


def _zero_at_first_step(*refs):
    @pl.when(pl.program_id(0) == 0)
    def _():
        for r in refs:
            r[...] = jnp.zeros_like(r)


def _add_rowsum(ref, row, t):
    ref[row:row + 1, :] += jnp.sum(t, axis=0, keepdims=True)


def _rms(x):
    r = lax.rsqrt(jnp.mean(x * x, axis=-1, keepdims=True) + EPS)
    return x * r, r


def _modulate_bwd(dh, n, r, gs, dxo, sums_ref):
    _add_rowsum(sums_ref, 0, dh * n)
    _add_rowsum(sums_ref, 1, dh)
    dn = dh * gs
    return dxo + r * (dn - n * jnp.mean(dn * n, axis=-1, keepdims=True))


def _silu_grad(a, sg):
    return sg * (1.0 + a * (1.0 - sg))


def _vec(*rows):
    d = rows[0].shape[-1]
    rows = [r.reshape(1, d).astype(F32) for r in rows]
    return jnp.concatenate(rows + [jnp.zeros((8 - len(rows), d), F32)], axis=0)


def _ffn_fwd(x, vec, w1, w3, w2):
    S, D = x.shape
    nch, _, fc = w1.shape
    tm = _tile(S, TOKEN_TILE)

    def body(x_ref, vec_ref, w1_ref, w3_ref, w2_ref, xo_ref, a1_ref, a3_ref, y_ref):
        x = x_ref[...]
        n, _ = _rms(x)
        h = (n * vec_ref[0:1, :] + vec_ref[1:2, :]).astype(BF16)
        acc = jnp.zeros((tm, D), F32)
        for j in range(nch):
            a1 = _dot(h, w1_ref[j]).astype(BF16)
            a3 = _dot(h, w3_ref[j]).astype(BF16)
            a1_ref[j] = a1
            a3_ref[j] = a3
            a1 = a1.astype(F32)
            u = a1 * jax.nn.sigmoid(a1) * a3.astype(F32)
            acc = acc + _dot(u.astype(BF16), w2_ref[j])
        y_ref[...] = acc.astype(BF16)
        xo_ref[...] = x + vec_ref[2:3, :] * acc

    chunked = pl.BlockSpec((nch, tm, fc), lambda i: (0, i, 0))
    return pl.pallas_call(
        body, name="ffn_fwd", grid=(S // tm,),
        in_specs=[_rows(tm, D), _resident((8, D)), _resident(w1.shape), _resident(w3.shape), _resident(w2.shape)],
        out_specs=[_rows(tm, D), chunked, chunked, _rows(tm, D)],
        out_shape=[SDS((S, D), F32), SDS((nch, S, fc), BF16), SDS((nch, S, fc), BF16), SDS((S, D), BF16)],
        compiler_params=_params("parallel"),
    )(x, vec, w1, w3, w2)


def _ffn_bwd(x, dxo, a1, a3, y, vec, w1, w3, w2):
    S, D = x.shape
    nch, _, fc = w1.shape
    tm = _tile(S, TOKEN_TILE // 2)

    def body(x_ref, dxo_ref, a1_ref, a3_ref, y_ref, vec_ref, w1_ref, w3_ref, w2_ref,
             dx_ref, h_ref, dy_ref, u_ref, da1_ref, da3_ref, sums_ref):
        _zero_at_first_step(sums_ref)
        x = x_ref[...]
        dxo = dxo_ref[...]
        gs = vec_ref[0:1, :]
        n, r = _rms(x)
        h_ref[...] = (n * gs + vec_ref[1:2, :]).astype(BF16).T
        _add_rowsum(sums_ref, 2, dxo * y_ref[...].astype(F32))
        dy = (vec_ref[2:3, :] * dxo).astype(BF16)
        dy_ref[...] = dy
        dh = jnp.zeros((tm, D), F32)
        for j in range(nch):
            a1 = a1_ref[j].astype(F32)
            a3 = a3_ref[j].astype(F32)
            sg = jax.nn.sigmoid(a1)
            s = a1 * sg
            du = _dot_nt(dy, w2_ref[j])
            da1 = (du * a3 * _silu_grad(a1, sg)).astype(BF16)
            da3 = (du * s).astype(BF16)
            u_ref[j] = (s * a3).astype(BF16)
            da1_ref[j] = da1
            da3_ref[j] = da3
            dh = dh + _dot_nt(da1, w1_ref[j]) + _dot_nt(da3, w3_ref[j])
        dx_ref[...] = _modulate_bwd(dh, n, r, gs, dxo, sums_ref)

    chunked = pl.BlockSpec((nch, tm, fc), lambda i: (0, i, 0))
    return pl.pallas_call(
        body, name="ffn_bwd", grid=(S // tm,),
        in_specs=[_rows(tm, D), _rows(tm, D), chunked, chunked, _rows(tm, D), _resident((8, D)),
                  _resident(w1.shape), _resident(w3.shape), _resident(w2.shape)],
        out_specs=[_rows(tm, D), pl.BlockSpec((D, tm), lambda i: (0, i)), _rows(tm, D), chunked, chunked, chunked, _acc_spec(8, D)],
        out_shape=[SDS((S, D), F32), SDS((D, S), BF16), SDS((S, D), BF16), SDS((nch, S, fc), BF16),
                   SDS((nch, S, fc), BF16), SDS((nch, S, fc), BF16), SDS((8, D), F32)],
        compiler_params=_params("arbitrary"),
    )(x, dxo, a1, a3, y, vec, w1, w3, w2)


def _wgrad(a, b, a_mode, b_mode, nch, name, acc, slot):
    S = b.shape[-2]
    M = a.shape[0] if a_mode == 'transposed' else a.shape[-1]
    N = b.shape[-1] // (nch if b_mode == 'col' else 1)
    assert acc.shape[1:] == (nch, M, N)
    ts = _tile(S, WGRAD_TILE)

    def spec(mode, width):
        if mode == 'full':
            return pl.BlockSpec((ts, width), lambda j, s: (s, 0))
        if mode == 'lead':
            return pl.BlockSpec((None, ts, width), lambda j, s: (j, s, 0))
        if mode == 'transposed':
            return pl.BlockSpec((width, ts), lambda j, s: (0, s))
        return pl.BlockSpec((ts, width), lambda j, s: (s, j))

    def body(a_ref, b_ref, acc_ref, o_ref):
        @pl.when(pl.program_id(1) == 0)
        def _():
            o_ref[...] = jnp.zeros_like(o_ref)
        o_ref[...] += (_dot if a_mode == 'transposed' else _dot_tn)(a_ref[...], b_ref[...])

    return pl.pallas_call(
        body, name=name, grid=(nch, S // ts), in_specs=[spec(a_mode, M), spec(b_mode, N), ANY],
        out_specs=pl.BlockSpec((None, None, M, N), lambda j, s: (slot, j, 0, 0)), out_shape=SDS(acc.shape, F32),
        input_output_aliases={2: 0}, compiler_params=_params("parallel", "arbitrary"),
    )(a, b, acc)


def _in_fwd(x, vec, w, bias, glu, name):
    S, D = x.shape
    nch, _, nc = w.shape
    N = nch * nc
    tm = _tile(S, TOKEN_TILE)

    def body(x_ref, vec_ref, w_ref, b_ref, p_ref, *u_ref):
        n, _ = _rms(x_ref[...])
        h = (n * vec_ref[0:1, :] + vec_ref[1:2, :]).astype(BF16)
        for j in range(nch):
            cols = slice(j * nc, (j + 1) * nc)
            p_ref[:, cols] = (_dot(h, w_ref[j]) + b_ref[:, cols]).astype(p_ref.dtype)
        if glu:
            half = N // 2
            u_ref[0][...] = p_ref[:, :half].astype(F32) * jax.nn.sigmoid(p_ref[:, half:].astype(F32))

    out_specs = [_rows(tm, N)] + ([_rows(tm, N // 2)] if glu else [])
    out_shape = [SDS((S, N), BF16 if glu else F32)] + ([SDS((S, N // 2), F32)] if glu else [])
    return pl.pallas_call(
        body, name=name, grid=(S // tm,),
        in_specs=[_rows(tm, D), _resident((8, D)), _resident(w.shape), _resident((1, N))],
        out_specs=out_specs, out_shape=out_shape, compiler_params=_params("parallel"),
    )(x, vec, w, bias)


def _in_bwd(x, dxo, dp, vec, w, name):
    S, D = x.shape
    nch, _, nc = w.shape
    tm = _tile(S, TOKEN_TILE)

    def body(x_ref, dxo_ref, dp_ref, vec_ref, w_ref, dx_ref, h_ref, sums_ref):
        _zero_at_first_step(sums_ref)
        gs = vec_ref[0:1, :]
        n, r = _rms(x_ref[...])
        h_ref[...] = (n * gs + vec_ref[1:2, :]).astype(BF16)
        dh = jnp.zeros((tm, D), F32)
        for j in range(nch):
            dh = dh + _dot_nt(dp_ref[:, j * nc:(j + 1) * nc], w_ref[j])
        dx_ref[...] = _modulate_bwd(dh, n, r, gs, dxo_ref[...], sums_ref)

    return pl.pallas_call(
        body, name=name, grid=(S // tm,),
        in_specs=[_rows(tm, D), _rows(tm, D), _rows(tm, nch * nc), _resident((8, D)), _resident(w.shape)],
        out_specs=[_rows(tm, D), _rows(tm, D), _acc_spec(8, D)],
        out_shape=[SDS((S, D), F32), SDS((S, D), BF16), SDS((8, D), F32)],
        compiler_params=_params("arbitrary"),
    )(x, dxo, dp, vec, w)


def _out_fwd(x, t, vec, w):
    S, D = x.shape
    tm = _tile(S, TOKEN_TILE)

    def body(x_ref, t_ref, vec_ref, w_ref, xo_ref, y_ref):
        y = _dot(t_ref[...], w_ref[...])
        y_ref[...] = y.astype(BF16)
        xo_ref[...] = x_ref[...] + vec_ref[2:3, :] * y

    return pl.pallas_call(
        body, name="attn_out_fwd", grid=(S // tm,),
        in_specs=[_rows(tm, D), _rows(tm, t.shape[1]), _resident((8, D)), _resident(w.shape)],
        out_specs=[_rows(tm, D), _rows(tm, D)], out_shape=[SDS((S, D), F32), SDS((S, D), BF16)],
        compiler_params=_params("parallel"),
    )(x, t, vec, w)


def _out_bwd(dxo, y, vec, w):
    S, D = dxo.shape
    K = w.shape[0]
    tm = _tile(S, TOKEN_TILE)

    def body(dxo_ref, y_ref, vec_ref, w_ref, dt_ref, dout_ref, sums_ref):
        _zero_at_first_step(sums_ref)
        dxo = dxo_ref[...]
        _add_rowsum(sums_ref, 2, dxo * y_ref[...].astype(F32))
        dout = (vec_ref[2:3, :] * dxo).astype(BF16)
        dout_ref[...] = dout
        dt_ref[...] = _dot_nt(dout, w_ref[...]).astype(BF16)

    return pl.pallas_call(
        body, name="attn_out_bwd", grid=(S // tm,),
        in_specs=[_rows(tm, D), _rows(tm, D), _resident((8, D)), _resident(w.shape)],
        out_specs=[_rows(tm, K), _rows(tm, D), _acc_spec(8, D)],
        out_shape=[SDS((S, K), BF16), SDS((S, D), BF16), SDS((8, D), F32)],
        compiler_params=_params("arbitrary"),
    )(dxo, y, vec, w)


def _prev_halo(tm, width):
    return pl.BlockSpec((HALO, width), lambda i: (jnp.maximum(i * (tm // HALO) - 1, 0), 0))


def _next_halo(tm, width, n_rows):
    last = n_rows // HALO - 1
    return pl.BlockSpec((HALO, width), lambda i: (jnp.minimum((i + 1) * (tm // HALO), last), 0))


def _layer_norm(v, g, b):
    mu = jnp.mean(v, axis=-1, keepdims=True)
    vc = v - mu
    rstd = lax.rsqrt(jnp.mean(vc * vc, axis=-1, keepdims=True) + EPS)
    vh = vc * rstd
    return vh * g + b, vh, rstd


def _fill_shifts(ext, sh, n):
    ext[n:, :] = jnp.zeros((8, ext.shape[1]), F32)
    for b in range(1, 8):
        sh[b - 1] = ext[pl.ds(b, n), :]


def _shifted(ext, sh, off, r0, rows, cols):
    b = off % 8
    base = r0 + off - b
    return ext[pl.ds(base, rows), cols] if b == 0 else sh[b - 1, pl.ds(base, rows), cols]


def _conv_mid_fwd(u, x, vec, dw, w_out):
    S, D = x.shape
    taps = dw.shape[0] - 1
    tm = _tile(S, TOKEN_TILE // 2)

    def body(u_ref, uh_ref, x_ref, vec_ref, dw_ref, w_ref, xo_ref, v_ref, y_ref, ext, sh):
        n = HALO + tm
        ext[0:HALO, :] = jnp.where(pl.program_id(0) > 0, uh_ref[...], 0.0)
        ext[HALO:n, :] = u_ref[...]
        _fill_shifts(ext, sh, n)
        cb = min(CONV_COLS, D)
        for c0 in range(0, D, cb):
            cols = slice(c0, c0 + cb)
            w = [dw_ref[k:k + 1, cols] for k in range(taps)]
            for r0 in range(0, tm, CONV_ROWS):
                acc = jnp.zeros((CONV_ROWS, cb), F32) + vec_ref[1:2, cols]
                for k in range(taps):
                    acc = acc + w[k] * _shifted(ext, sh, HALO - (taps - 1) + k, r0, CONV_ROWS, cols)
                v_ref[r0:r0 + CONV_ROWS, cols] = acc
        v = v_ref[...]
        l, _, _ = _layer_norm(v, vec_ref[2:3, :], vec_ref[3:4, :])
        q = (l * jax.nn.sigmoid(l)).astype(BF16)
        y = _dot(q, w_ref[...]) + vec_ref[4:5, :]
        y_ref[...] = y.astype(BF16)
        xo_ref[...] = x_ref[...] + vec_ref[0:1, :] * y

    return pl.pallas_call(
        body, name="conv_mid_fwd", grid=(S // tm,),
        in_specs=[_rows(tm, D), _prev_halo(tm, D), _rows(tm, D), _resident((8, D)), _resident(dw.shape), _resident(w_out.shape)],
        out_specs=[_rows(tm, D), _rows(tm, D), _rows(tm, D)],
        out_shape=[SDS((S, D), F32), SDS((S, D), F32), SDS((S, D), BF16)],
        scratch_shapes=[pltpu.VMEM((HALO + tm + 8, D), F32), pltpu.VMEM((7, HALO + tm, D), F32)],
        compiler_params=_params("parallel"),
    )(u, u, x, vec, dw, w_out)


def _conv_mid_bwd(dxo, y, v, vec, w_out):
    S, D = dxo.shape
    tm = _tile(S, TOKEN_TILE)

    def body(dxo_ref, y_ref, v_ref, vec_ref, w_ref, dv_ref, q_ref, dout_ref, sums_ref):
        _zero_at_first_step(sums_ref)
        dxo = dxo_ref[...]
        _add_rowsum(sums_ref, 0, dxo * y_ref[...].astype(F32))
        dout = vec_ref[0:1, :] * dxo
        _add_rowsum(sums_ref, 1, dout)
        dout = dout.astype(BF16)
        dout_ref[...] = dout
        ln_g = vec_ref[2:3, :]
        l, vh, rstd = _layer_norm(v_ref[...], ln_g, vec_ref[3:4, :])
        sg = jax.nn.sigmoid(l)
        q_ref[...] = (l * sg).astype(BF16)
        dl = _dot_nt(dout, w_ref[...]) * _silu_grad(l, sg)
        _add_rowsum(sums_ref, 2, dl * vh)
        _add_rowsum(sums_ref, 3, dl)
        dvh = dl * ln_g
        dv = rstd * (dvh - jnp.mean(dvh, axis=-1, keepdims=True) - vh * jnp.mean(dvh * vh, axis=-1, keepdims=True))
        _add_rowsum(sums_ref, 4, dv)
        dv_ref[...] = dv

    return pl.pallas_call(
        body, name="conv_mid_bwd", grid=(S // tm,),
        in_specs=[_rows(tm, D), _rows(tm, D), _rows(tm, D), _resident((8, D)), _resident(w_out.shape)],
        out_specs=[_rows(tm, D), _rows(tm, D), _rows(tm, D), _acc_spec(8, D)],
        out_shape=[SDS((S, D), F32), SDS((S, D), BF16), SDS((S, D), BF16), SDS((8, D), F32)],
        compiler_params=_params("arbitrary"),
    )(dxo, y, v, vec, w_out)


def _conv_transpose(dv, u, dw):
    S, D = dv.shape
    taps = dw.shape[0] - 1
    tm = _tile(S, TOKEN_TILE // 2)

    def body(dv_ref, dvn_ref, u_ref, uh_ref, dw_ref, du_ref, gdw_ref, extv, shv, extu, shu):
        _zero_at_first_step(gdw_ref)
        i = pl.program_id(0)
        n = HALO + tm
        extv[0:tm, :] = dv_ref[...]
        extv[tm:n, :] = jnp.where(i < pl.num_programs(0) - 1, dvn_ref[...], 0.0)
        _fill_shifts(extv, shv, n)
        extu[0:HALO, :] = jnp.where(i > 0, uh_ref[...], 0.0)
        extu[HALO:n, :] = u_ref[...]
        _fill_shifts(extu, shu, n)
        cb = min(CONV_COLS, D)
        for c0 in range(0, D, cb):
            cols = slice(c0, c0 + cb)
            w = [dw_ref[k:k + 1, cols] for k in range(taps)]
            for r0 in range(0, tm, CONV_ROWS):
                acc = jnp.zeros((CONV_ROWS, cb), F32)
                for k in range(taps):
                    acc = acc + w[k] * _shifted(extv, shv, taps - 1 - k, r0, CONV_ROWS, cols)
                du_ref[r0:r0 + CONV_ROWS, cols] = acc
        for c0 in range(0, D, LANES):
            cols = slice(c0, c0 + LANES)
            accs = [jnp.zeros((8, LANES), F32) for _ in range(taps)]
            for r0 in range(0, tm, CONV_ROWS):
                dvb = extv[r0:r0 + CONV_ROWS, cols]
                for k in range(taps):
                    p = dvb * _shifted(extu, shu, HALO - (taps - 1) + k, r0, CONV_ROWS, cols)
                    for s in range(0, CONV_ROWS, 8):
                        accs[k] = accs[k] + p[s:s + 8]
            for k in range(taps):
                gdw_ref[k:k + 1, cols] += jnp.sum(accs[k], axis=0, keepdims=True)

    return pl.pallas_call(
        body, name="conv_transpose", grid=(S // tm,),
        in_specs=[_rows(tm, D), _next_halo(tm, D, S), _rows(tm, D), _prev_halo(tm, D), _resident(dw.shape)],
        out_specs=[_rows(tm, D), _acc_spec(dw.shape[0], D)],
        out_shape=[SDS((S, D), F32), SDS(dw.shape, F32)],
        scratch_shapes=[pltpu.VMEM((HALO + tm + 8, D), F32), pltpu.VMEM((7, HALO + tm, D), F32),
                        pltpu.VMEM((HALO + tm + 8, D), F32), pltpu.VMEM((7, HALO + tm, D), F32)],
        compiler_params=_params("arbitrary"),
    )(dv, dv, u, u, dw)


def _glu_bwd(du, p):
    S, D = du.shape
    tm = _tile(S, TOKEN_TILE)

    def body(du_ref, p_ref, dp_ref, sums_ref):
        _zero_at_first_step(sums_ref)
        du = du_ref[...]
        a = p_ref[:, :D].astype(F32)
        sb = jax.nn.sigmoid(p_ref[:, D:].astype(F32))
        da = du * sb
        db = du * a * sb * (1.0 - sb)
        dp_ref[:, :D] = da.astype(BF16)
        dp_ref[:, D:] = db.astype(BF16)
        sums_ref[0:1, :D] += jnp.sum(da, axis=0, keepdims=True)
        sums_ref[0:1, D:] += jnp.sum(db, axis=0, keepdims=True)

    return pl.pallas_call(
        body, name="glu_bwd", grid=(S // tm,), in_specs=[_rows(tm, D), _rows(tm, 2 * D)],
        out_specs=[_rows(tm, 2 * D), _acc_spec(8, 2 * D)], out_shape=[SDS((S, 2 * D), BF16), SDS((8, 2 * D), F32)],
        compiler_params=_params("arbitrary"),
    )(du, p)


def _head_mean(t, bd):
    hi = t.astype(BF16)
    lo = (t - hi.astype(F32)).astype(BF16)
    return (_dot(hi, bd) + _dot(lo, bd)) * (1.0 / HEAD_DIM)


def _head_blocks():
    r = lax.broadcasted_iota(jnp.int32, (LANES, LANES), 0) // HEAD_DIM
    c = lax.broadcasted_iota(jnp.int32, (LANES, LANES), 1) // HEAD_DIM
    return (r == c).astype(BF16)


def _qknorm_fwd(raw, gq, gk):
    S, D3 = raw.shape
    D = D3 // 3
    tm = _tile(S, TOKEN_TILE)

    def body(raw_ref, gq_ref, gk_ref, o_ref):
        bd = _head_blocks()
        for off, g_ref in ((0, gq_ref), (D, gk_ref)):
            for c in range(D // LANES):
                cols = slice(off + c * LANES, off + (c + 1) * LANES)
                xs = raw_ref[:, cols]
                r = lax.rsqrt(_head_mean(xs * xs, bd) + EPS)
                o_ref[:, cols] = (xs * r * g_ref[:, c * LANES:(c + 1) * LANES]).astype(BF16)
        o_ref[:, 2 * D:] = raw_ref[:, 2 * D:].astype(BF16)

    return pl.pallas_call(
        body, name="qknorm_fwd", grid=(S // tm,), in_specs=[_rows(tm, D3), _resident((1, D)), _resident((1, D))],
        out_specs=_rows(tm, D3), out_shape=SDS((S, D3), BF16), compiler_params=_params("parallel"),
    )(raw, gq, gk)


def _qknorm_bwd(dq, dk, dv, raw, gq, gk):
    S, D3 = raw.shape
    D = D3 // 3
    nb = D // LANES
    tm = _tile(S, TOKEN_TILE)

    def body(dq_ref, dk_ref, dv_ref, raw_ref, gq_ref, gk_ref, o_ref, sums_ref):
        _zero_at_first_step(sums_ref)
        bd = _head_blocks()
        for row, (off, g_ref, d_ref) in enumerate(((0, gq_ref, dq_ref), (D, gk_ref, dk_ref))):
            for c in range(nb):
                lanes = slice(c * LANES, (c + 1) * LANES)
                cols = slice(off + c * LANES, off + (c + 1) * LANES)
                xs = raw_ref[:, cols]
                r = lax.rsqrt(_head_mean(xs * xs, bd) + EPS)
                n = xs * r
                dhat = d_ref[c]
                sums_ref[row:row + 1, lanes] += jnp.sum(dhat * n, axis=0, keepdims=True)
                dn = dhat * g_ref[:, lanes]
                o_ref[:, cols] = (r * (dn - n * _head_mean(dn * n, bd))).astype(BF16)
        for c in range(nb):
            o_ref[:, 2 * D + c * LANES:2 * D + (c + 1) * LANES] = dv_ref[c].astype(BF16)

    tiles = pl.BlockSpec((nb, tm, LANES), lambda i: (0, i, 0))
    return pl.pallas_call(
        body, name="qknorm_bwd", grid=(S // tm,),
        in_specs=[tiles, tiles, tiles, _rows(tm, D3), _resident((1, D)), _resident((1, D))],
        out_specs=[_rows(tm, D3), _acc_spec(8, D)], out_shape=[SDS((S, D3), BF16), SDS((8, D), F32)],
        compiler_params=_params("arbitrary"),
    )(dq, dk, dv, raw, gq, gk)


def _softplus2(z):
    return jnp.maximum(z, jnp.log2(1.0 + jnp.exp2(jnp.minimum(z, 30.0))))


def _attn_tiles(S):
    tq = _tile(S, ATTN_Q_TILE)
    tk = _tile(tq, ATTN_K_TILE)
    return tq, tk


def _attn_consts(tq, tk):
    lane = lax.broadcasted_iota(jnp.int32, (1, LANES), 1)
    first = lane < HEAD_DIM
    r = lax.broadcasted_iota(jnp.int32, (tq, tk), 0)
    c = lax.broadcasted_iota(jnp.int32, (tq, tk), 1)
    kr = lax.broadcasted_iota(jnp.int32, (tk, tk), 0)
    kc = lax.broadcasted_iota(jnp.int32, (tk, tk), 1)
    tri = c < r
    later = (kr > kc).astype(BF16)
    upto = (kr <= kc).astype(BF16)
    return first, tri, later, upto


def _row_sums(t):
    return jnp.broadcast_to(jnp.sum(t, axis=1, keepdims=True), (t.shape[0], LANES))


def _over_keys(per_row, tk):
    return per_row[:, :tk] if tk <= LANES else jnp.concatenate([per_row] * (tk // LANES), axis=1)


def _from(x, lo, axis=0):
    return x if lo == 0 else lax.slice_in_dim(x, lo, x.shape[axis], axis=axis)


def _add_from(acc, lo, part):
    return acc + (part if lo == 0 else jnp.concatenate([jnp.zeros((lo,) + part.shape[1:], part.dtype), part], axis=0))


def _attn_specs(S, tq, nb):
    q_spec = pl.BlockSpec((tq, LANES), lambda hp, i: (i, hp))
    k_spec = pl.BlockSpec((S, LANES), lambda hp, i: (0, nb + hp), pipeline_mode=pl.Buffered(1))
    v_spec = pl.BlockSpec((S, LANES), lambda hp, i: (0, 2 * nb + hp), pipeline_mode=pl.Buffered(1))
    return q_spec, k_spec, v_spec


def _attn_fwd(qkv):
    S, D3 = qkv.shape
    D = D3 // 3
    nb = D // LANES
    tq, tk = _attn_tiles(S)
    nsub = tq // tk

    def body(q_ref, k_ref, v_ref, o_ref, tot_ref, start_ref):
        hp = pl.program_id(0)
        i = pl.program_id(1)
        first, tri, later, _ = _attn_consts(tq, tk)
        q = q_ref[...]
        qs = (jnp.where(first, q, jnp.zeros_like(q)), jnp.where(first, jnp.zeros_like(q), q))

        def tiles(t0, n, carry, diagonal):
            out = []
            for h in range(2):
                o, c = carry[h]
                for a in reversed(range(n)):
                    lo = a * tk if diagonal else 0
                    mask = tri[:tq - lo] if diagonal else None
                    rows = pl.ds(pl.multiple_of((t0 + a) * tk, tk), tk)
                    kb = k_ref[rows, :]
                    z = _dot_nt(_from(qs[h], lo), kb)
                    sp = _softplus2(z)
                    logsig = z - sp
                    if diagonal:
                        sp = jnp.where(mask, sp, 0.0)
                    av = jnp.exp2(logsig - _dot(sp.astype(BF16), later) - _over_keys(_from(c, lo), tk))
                    if diagonal:
                        av = jnp.where(mask, av, 0.0)
                    o = _add_from(o, lo, _dot(av.astype(BF16), v_ref[rows, :]))
                    c = _add_from(c, lo, _row_sums(sp))
                out.append((o, c))
            return tuple(out)

        def live(carry):
            return jnp.minimum(jnp.min(carry[0][1]), jnp.min(carry[1][1])) < ATTN_DEAD_BITS

        carry = tuple((jnp.zeros((tq, LANES), F32), jnp.zeros((tq, LANES), F32)) for _ in range(2))
        carry = tiles(i * nsub, nsub, carry, True)
        band_floor = jnp.maximum((i - 1) * nsub, 0)
        t, carry = lax.while_loop(lambda st: (st[0] >= band_floor) & (st[0] >= 0) & live(st[1]),
                                  lambda st: (st[0] - 1, tiles(st[0], 1, st[1], False)), (i * nsub - 1, carry))
        t, carry = lax.while_loop(lambda st: (st[0] >= nsub - 1) & live(st[1]),
                                  lambda st: (st[0] - nsub, tiles(st[0] - (nsub - 1), nsub, st[1], False)), (t, carry))
        (o_a, c_a), (o_b, c_b) = carry
        o_ref[...] = jnp.where(first, o_a, o_b).astype(BF16)
        tot_ref[...] = jnp.where(first, c_a, c_b)
        start_ref[hp, i] = t + 1

    q_spec, k_spec, v_spec = _attn_specs(S, tq, nb)
    return pl.pallas_call(
        body, name="attn_fwd", grid=(nb, S // tq), in_specs=[q_spec, k_spec, v_spec],
        out_specs=[q_spec, q_spec, pl.BlockSpec(memory_space=pltpu.SMEM)],
        out_shape=[SDS((S, D), BF16), SDS((S, D), F32), SDS((nb, S // tq), jnp.int32)],
        compiler_params=_params("arbitrary", "arbitrary"),
    )(qkv, qkv, qkv)


def _attn_bwd(qkv, do, tot, start):
    S, D3 = qkv.shape
    D = D3 // 3
    nb = D // LANES
    tq, tk = _attn_tiles(S)
    nsub = tq // tk
    nkb = S // tk

    def body(start_ref, q_ref, k_ref, v_ref, do_ref, tot_ref, dq_ref, dk_hbm, dv_hbm, dkt_acc, dvt_acc, stage, sem):
        hp = pl.program_id(0)
        i = pl.program_id(1)

        @pl.when(i == 0)
        def _():
            dkt_acc[...] = jnp.zeros_like(dkt_acc)
            dvt_acc[...] = jnp.zeros_like(dvt_acc)

        first, tri, later, upto = _attn_consts(tq, tk)
        q = q_ref[...]
        do = do_ref[...]
        zero = jnp.zeros_like(q)
        qs = (jnp.where(first, q, zero), jnp.where(first, zero, q))
        dos = (jnp.where(first, do, zero), jnp.where(first, zero, do))
        qt = q.astype(F32).T.astype(BF16)
        dot_ = do.astype(F32).T.astype(BF16)
        qts = (qt[:HEAD_DIM], qt[HEAD_DIM:])
        dots = (dot_[:HEAD_DIM], dot_[HEAD_DIM:])
        tots = tuple(jnp.broadcast_to(tot_ref[:, l:l + 1], (tq, LANES)) for l in (0, HEAD_DIM))

        def tiles(t0, n, carry, diagonal):
            carry = list(carry)
            for a in range(n):
                lo = a * tk if diagonal else 0
                mask = tri[:tq - lo] if diagonal else None
                j = t0 + a
                rows = pl.ds(pl.multiple_of(j * tk, tk), tk)
                kb = k_ref[rows, :]
                vb = v_ref[rows, :]
                dkts, dvts = [], []
                for h in range(2):
                    dq, cum, pre = carry[h]
                    z = _dot_nt(_from(qs[h], lo), kb)
                    sp = _softplus2(z)
                    logsig = z - sp
                    if diagonal:
                        sp = jnp.where(mask, sp, 0.0)
                    cum = _add_from(cum, lo, _row_sums(sp))
                    av = jnp.exp2(logsig - _dot(sp.astype(BF16), later) - _over_keys(_from(tots[h], lo) - _from(cum, lo), tk))
                    if diagonal:
                        av = jnp.where(mask, av, 0.0)
                    g = _dot_nt(_from(dos[h], lo), vb) * av
                    dz = g - jnp.exp2(logsig) * (_over_keys(_from(pre, lo), tk) + _dot(g.astype(BF16), upto))
                    if diagonal:
                        dz = jnp.where(mask, dz, 0.0)
                    dz = dz.astype(BF16)
                    dkts.append(_dot(_from(qts[h], lo, 1), dz))
                    dvts.append(_dot(_from(dots[h], lo, 1), av.astype(BF16)))
                    carry[h] = (_add_from(dq, lo, _dot(dz, kb)), cum, _add_from(pre, lo, _row_sums(g)))
                dkt_acc[j] += jnp.concatenate(dkts, axis=0)
                dvt_acc[j] += jnp.concatenate(dvts, axis=0)
            return tuple(carry)

        carry = tuple((jnp.zeros((tq, LANES), F32),) * 3 for _ in range(2))
        t0 = start_ref[hp, i]
        odd = lax.rem(i * nsub - t0, nsub)
        carry = lax.fori_loop(0, odd, lambda s, cr: tiles(t0 + s, 1, cr, False), carry)
        carry = lax.fori_loop(0, (i * nsub - t0) // nsub, lambda b, cr: tiles(t0 + odd + b * nsub, nsub, cr, False), carry)
        carry = tiles(i * nsub, nsub, carry, True)
        dq_ref[...] = jnp.where(first, carry[0][0], carry[1][0]) * LN2

        @pl.when(i == pl.num_programs(1) - 1)
        def _():
            def flush(j, _):
                rows = pl.ds(pl.multiple_of(j * tk, tk), tk)
                stage[0] = dkt_acc[j].T * LN2
                stage[1] = dvt_acc[j].T
                ck = pltpu.make_async_copy(stage.at[0], dk_hbm.at[hp, rows], sem.at[0])
                cv = pltpu.make_async_copy(stage.at[1], dv_hbm.at[hp, rows], sem.at[1])
                ck.start()
                cv.start()
                ck.wait()
                cv.wait()
                return 0
            lax.fori_loop(0, nkb, flush, 0)

    q_spec, k_spec, v_spec = _attn_specs(S, tq, nb)
    slab = SDS((nb, S, LANES), F32)
    return pl.pallas_call(
        body, name="attn_bwd", grid=(nb, S // tq),
        in_specs=[pl.BlockSpec(memory_space=pltpu.SMEM), q_spec, k_spec, v_spec, q_spec, q_spec],
        out_specs=[pl.BlockSpec((None, tq, LANES), lambda hp, i: (hp, i, 0)), ANY, ANY], out_shape=[slab, slab, slab],
        scratch_shapes=[pltpu.VMEM((nkb, LANES, tk), F32), pltpu.VMEM((nkb, LANES, tk), F32), pltpu.VMEM((2, tk, LANES), F32),
                        pltpu.SemaphoreType.DMA((2,))],
        compiler_params=_params("arbitrary", "arbitrary"),
    )(start, qkv, qkv, qkv, do, tot)


def _trail_sum(ext, bufs, cols, levels, n):
    def src(lo, size):
        return ext[pl.ds(lo, size), cols]
    for l in range(levels):
        lo = 8 * (l + 1)
        dst = bufs[l % 2]
        dst[lo:, :] = src(lo, n - lo) + src(lo - (1 << l), n - lo)
        def src(lo_, size, d=dst):
            return d[pl.ds(lo_, size), :]
    return src(HALO, n - HALO)


def _lead_sum(ext, bufs, cols, levels, n):
    def src(lo, size):
        return ext[pl.ds(lo, size), cols]
    for l in range(levels):
        hi = n - 8 * (l + 1)
        dst = bufs[l % 2]
        dst[0:hi, :] = src(0, hi) + src(1 << l, hi)
        def src(lo_, size, d=dst):
            return d[pl.ds(lo_, size), :]
    return src(0, n - HALO)


def _pool_diffs(x_ref, xh_ref, vec_ref, ext, bufs, tm, D):
    i = pl.program_id(0)
    gs, shift = vec_ref[0:1, :], vec_ref[1:2, :]
    n, r = _rms(x_ref[...])
    nh, _ = _rms(xh_ref[...])
    ext[0:HALO, :] = jnp.where(i > 0, nh * gs + shift, 0.0)
    ext[HALO:, :] = n * gs + shift
    t = i * tm + lax.broadcasted_iota(jnp.int32, (tm, 1), 0)
    dg = D // len(POOL_LEVELS)
    out = []
    for g, lv in enumerate(POOL_LEVELS):
        cols = slice(g * dg, (g + 1) * dg)
        inv = 1.0 / jnp.minimum(t + 1, 1 << lv).astype(F32)
        out.append((_trail_sum(ext, bufs, cols, lv, HALO + tm) * inv - ext[HALO:, cols], inv))
    return out, n, r


def _pool_fwd(x, vec, pw):
    S, D = x.shape
    ng, dg, _ = pw.shape
    tm = _tile(S, TOKEN_TILE)

    def body(x_ref, xh_ref, vec_ref, pw_ref, xo_ref, ext, buf_a, buf_b):
        diffs, _, _ = _pool_diffs(x_ref, xh_ref, vec_ref, ext, (buf_a, buf_b), tm, D)
        for g, (d, _) in enumerate(diffs):
            cols = slice(g * dg, (g + 1) * dg)
            y = (_dot(d.astype(BF16), pw_ref[g]) + vec_ref[4:5, cols]) * vec_ref[3:4, cols]
            xo_ref[:, cols] = x_ref[:, cols] + vec_ref[2:3, cols] * y

    return pl.pallas_call(
        body, name="pool_fwd", grid=(S // tm,),
        in_specs=[_rows(tm, D), _prev_halo(tm, D), _resident((8, D)), _resident(pw.shape)],
        out_specs=_rows(tm, D), out_shape=SDS((S, D), F32),
        scratch_shapes=[pltpu.VMEM((HALO + tm, D), F32), pltpu.VMEM((HALO + tm, dg), F32), pltpu.VMEM((HALO + tm, dg), F32)],
        compiler_params=_params("parallel"),
    )(x, x, vec, pw)


def _pool_bwd(x, dxo, vec, pw):
    S, D = x.shape
    ng, dg, _ = pw.shape
    tm = _tile(S, TOKEN_TILE)

    def body(x_ref, xh_ref, dxo_ref, dxn_ref, vec_ref, pw_ref, dx_ref, gpw_ref, sums_ref, ext, exte, buf_a, buf_b):
        _zero_at_first_step(gpw_ref, sums_ref)
        i = pl.program_id(0)
        bufs = (buf_a, buf_b)
        diffs, n, r = _pool_diffs(x_ref, xh_ref, vec_ref, ext, bufs, tm, D)
        gate, scale = vec_ref[2:3, :], vec_ref[3:4, :]
        dxo = dxo_ref[...]
        dyp_next = jnp.where(i < pl.num_programs(0) - 1, dxn_ref[...], 0.0) * gate * scale
        t_next = (i + 1) * tm + lax.broadcasted_iota(jnp.int32, (HALO, 1), 0)
        for g, (d, inv) in enumerate(diffs):
            cols = slice(g * dg, (g + 1) * dg)
            w = pw_ref[g]
            db = d.astype(BF16)
            ypre = _dot(db, w) + vec_ref[4:5, cols]
            dy = gate[:, cols] * dxo[:, cols]
            sums_ref[2:3, cols] += jnp.sum(dxo[:, cols] * ypre * scale[:, cols], axis=0, keepdims=True)
            sums_ref[3:4, cols] += jnp.sum(dy * ypre, axis=0, keepdims=True)
            dyp = dy * scale[:, cols]
            sums_ref[4:5, cols] += jnp.sum(dyp, axis=0, keepdims=True)
            dypb = dyp.astype(BF16)
            gpw_ref[g] += _dot_tn(db, dypb)
            dd = _dot_nt(dypb, w)
            dd_next = _dot_nt(dyp_next[:, cols].astype(BF16), w)
            inv_next = 1.0 / jnp.minimum(t_next + 1, 1 << POOL_LEVELS[g]).astype(F32)
            exte[0:tm, cols] = dd * inv
            exte[tm:, cols] = dd_next * inv_next
            ext[HALO:, cols] = _lead_sum(exte, bufs, cols, POOL_LEVELS[g], tm + HALO) - dd
        dx_ref[...] = _modulate_bwd(ext[HALO:, :], n, r, vec_ref[0:1, :], dxo, sums_ref)

    return pl.pallas_call(
        body, name="pool_bwd", grid=(S // tm,),
        in_specs=[_rows(tm, D), _prev_halo(tm, D), _rows(tm, D), _next_halo(tm, D, S), _resident((8, D)), _resident(pw.shape)],
        out_specs=[_rows(tm, D), pl.BlockSpec(pw.shape, lambda i: (0, 0, 0)), _acc_spec(8, D)],
        out_shape=[SDS((S, D), F32), SDS(pw.shape, F32), SDS((8, D), F32)],
        scratch_shapes=[pltpu.VMEM((HALO + tm, D), F32), pltpu.VMEM((tm + HALO, D), F32),
                        pltpu.VMEM((HALO + tm, dg), F32), pltpu.VMEM((HALO + tm, dg), F32)],
        compiler_params=_params("arbitrary"),
    )(x, x, dxo, dxo, vec, pw)


def _loss_head(y, target):
    S, D = y.shape
    tm = _tile(S, TOKEN_TILE)

    def body(y_ref, t_ref, dy_ref, sums_ref):
        _zero_at_first_step(sums_ref)
        err = y_ref[...] - t_ref[...]
        _add_rowsum(sums_ref, 0, err * err)
        dy_ref[...] = err * (1.0 / D)

    return pl.pallas_call(
        body, name="loss_head", grid=(S // tm,), in_specs=[_rows(tm, D), _rows(tm, D)],
        out_specs=[_rows(tm, D), _acc_spec(8, D)], out_shape=[SDS((S, D), F32), SDS((8, D), F32)],
        compiler_params=_params("arbitrary"),
    )(y, target)


def _cond_pre(c_cols, cond_w):
    def body(c_ref, w_ref, o_ref):
        o_ref[...] = _hdot(c_ref[...], w_ref[...])
    return pl.pallas_call(body, name="cond_pre", out_shape=SDS((c_cols.shape[0], cond_w.shape[1]), F32))(c_cols, cond_w)


def _cond_e(parts, cond_b):
    def body(p_ref, b_ref, pre_ref, e_ref):
        pre = p_ref[0] + p_ref[2] + p_ref[4] + p_ref[6] + b_ref[...]
        pre_ref[...] = pre
        e_ref[...] = pre * jax.nn.sigmoid(pre)
    shape = SDS(parts.shape[1:], F32)
    return pl.pallas_call(body, name="cond_e", out_shape=[shape, shape])(parts, cond_b)


def _mod_cols(e, ada_w, ada_b_cols):
    L, D, nc = ada_w.shape
    B = e.shape[0]

    def body(e_ref, w_ref, b_ref, o_ref):
        o_ref[...] = _hdot(e_ref[...], w_ref[...]) + b_ref[...]

    return pl.pallas_call(
        body, name="mod_cols", grid=(L,),
        in_specs=[pl.BlockSpec((B, D), lambda l: (0, 0)), pl.BlockSpec((None, D, nc), lambda l: (l, 0, 0)),
                  pl.BlockSpec((None, 1, nc), lambda l: (l, 0, 0))],
        out_specs=pl.BlockSpec((None, B, nc), lambda l: (l, 0, 0)), out_shape=SDS((L, B, nc), F32),
        compiler_params=_params("parallel"),
    )(e, ada_w, ada_b_cols)


def _mod_bwd(e, dmod_cols, ada_w):
    L, D, nc = ada_w.shape
    B = e.shape[0]

    def body(e_ref, d_ref, w_ref, gw_ref, de_ref):
        _zero_at_first_step(de_ref)
        gw_ref[...] = _hdot(e_ref[...], d_ref[...], (((0,), (0,)), ((), ())))
        de_ref[...] += _hdot(d_ref[...], w_ref[...], (((1,), (1,)), ((), ())))

    return pl.pallas_call(
        body, name="mod_bwd", grid=(L,),
        in_specs=[pl.BlockSpec((B, D), lambda l: (0, 0)), pl.BlockSpec((None, B, nc), lambda l: (l, 0, 0)),
                  pl.BlockSpec((None, D, nc), lambda l: (l, 0, 0))],
        out_specs=[pl.BlockSpec((None, D, nc), lambda l: (l, 0, 0)), pl.BlockSpec((B, D), lambda l: (0, 0))],
        out_shape=[SDS((L, D, nc), F32), SDS((B, D), F32)], compiler_params=_params("arbitrary"),
    )(e, dmod_cols, ada_w)


def _cond_bwd(de_parts, pre, c_cols):
    def body(p_ref, pre_ref, c_ref, gw_ref, gb_ref):
        pre = pre_ref[...]
        dpre = (p_ref[0] + p_ref[2] + p_ref[4] + p_ref[6]) * _silu_grad(pre, jax.nn.sigmoid(pre))
        gb_ref[...] = jnp.sum(dpre, axis=0, keepdims=True)
        gw_ref[...] = _hdot(c_ref[...], dpre, (((0,), (0,)), ((), ())))
    D = pre.shape[1]
    return pl.pallas_call(body, name="cond_bwd", out_shape=[SDS((c_cols.shape[1], D), F32), SDS((1, D), F32)])(de_parts, pre, c_cols)


def _as_rows(a):
    return a.reshape(-1, a.shape[-1])


def _row_tile(rows, width, n_arrays):
    t = max(8, (ADAM_TILE_BYTES // (4 * width)) // 8 * 8)
    while rows % t:
        t -= 8
        if t <= 0:
            return rows
    return t


def _adamw(w, g, m, v):
    shape = w.shape
    w, g, m, v = (_as_rows(a) for a in (w, g, m, v))
    R, C = w.shape
    tr = _row_tile(R, C, 7)

    def body(w_ref, g_ref, m_ref, v_ref, d_ref, nm_ref, nv_ref):
        g = g_ref[...]
        m = ADAM_B1 * m_ref[...] + (1.0 - ADAM_B1) * g
        v = ADAM_B2 * v_ref[...] + (1.0 - ADAM_B2) * (g * g)
        m_hat = m / (1.0 - ADAM_B1 ** ADAM_STEP)
        v_hat = v / (1.0 - ADAM_B2 ** ADAM_STEP)
        d_ref[...] = -ADAM_LR * (m_hat / (jnp.sqrt(v_hat) + ADAM_EPS) + ADAM_WD * w_ref[...])
        nm_ref[...] = m
        nv_ref[...] = v

    spec = _rows(tr, C)
    outs = pl.pallas_call(
        body, name="adamw", grid=(R // tr,), in_specs=[spec] * 4, out_specs=[spec] * 3,
        out_shape=[SDS((R, C), F32)] * 3, compiler_params=_params("parallel"),
    )(w, g, m, v)
    return tuple(o.reshape(shape) for o in outs)


def _sum_lead(a, out_dtype=F32):
    n = a.shape[0]
    shape = a.shape[1:]
    a = a.reshape(n, -1, a.shape[-1])
    _, R, C = a.shape
    tr = _row_tile(R, C, n + 1)

    def body(a_ref, o_ref):
        acc = a_ref[0].astype(F32)
        for k in range(1, n):
            acc = acc + a_ref[k].astype(F32)
        o_ref[...] = acc.astype(out_dtype)

    out = pl.pallas_call(
        body, name="sum_lead", grid=(R // tr,), in_specs=[pl.BlockSpec((n, tr, C), lambda i: (0, i, 0))],
        out_specs=_rows(tr, C), out_shape=SDS((R, C), out_dtype), compiler_params=_params("parallel"),
    )(a)
    return out.reshape(shape)


def _place():
    return lax.axis_index("x"), lax.axis_index("y"), lax.axis_index("c")


def _allgather8(a):
    def body(a_ref, o_ref, send, recv, local):
        mx, my, mc = _place()
        me = 4 * mx + 2 * my + mc
        mine = pltpu.make_async_copy(a_ref, o_ref.at[me], local)
        mine.start()
        copies = []
        for k in range(1, N_DEV):
            peer = (1 - mx if k & 4 else mx, 1 - my if k & 2 else my, 1 - mc if k & 1 else mc)
            cp = pltpu.make_async_remote_copy(a_ref, o_ref.at[me], send.at[k - 1], recv.at[k - 1], device_id=peer, device_id_type=MESH)
            cp.start()
            copies.append(cp)
        for cp in copies:
            cp.wait()
        mine.wait()

    return pl.pallas_call(
        body, name="allgather8", in_specs=[ANY], out_specs=ANY, out_shape=SDS((N_DEV,) + a.shape, a.dtype),
        scratch_shapes=[pltpu.SemaphoreType.DMA((N_DEV - 1,)), pltpu.SemaphoreType.DMA((N_DEV - 1,)), pltpu.SemaphoreType.DMA],
    )(a)


def _grad_scatter(arrs):
    n = len(arrs)
    out_shape = [SDS((N_CHIPS, a.shape[0]) + a.shape[2:], a.dtype) for a in arrs]

    def body(*refs):
        ins, outs, (send, recv, local) = refs[:n], refs[n:2 * n], refs[2 * n:]
        mx, my, mc = _place()
        chip = 2 * mx + my
        copies = []
        for a in range(n):
            cp = pltpu.make_async_copy(ins[a].at[:, chip], outs[a].at[chip], local.at[a])
            cp.start()
            copies.append(cp)
        for k in range(1, N_CHIPS):
            px, py = (1 - mx if k & 2 else mx), (1 - my if k & 1 else my)
            for a in range(n):
                s = (k - 1) * n + a
                cp = pltpu.make_async_remote_copy(ins[a].at[:, 2 * px + py], outs[a].at[chip], send.at[s], recv.at[s],
                                                  device_id=(px, py, mc), device_id_type=MESH)
                cp.start()
                copies.append(cp)
        for cp in copies:
            cp.wait()

    return pl.pallas_call(
        body, name="grad_scatter", in_specs=[ANY] * n, out_specs=[ANY] * n, out_shape=out_shape,
        scratch_shapes=[pltpu.SemaphoreType.DMA((3 * n,)), pltpu.SemaphoreType.DMA((3 * n,)), pltpu.SemaphoreType.DMA((n,))],
    )(*arrs)


def _weight_gather(arrs):
    n = len(arrs)

    def body(*refs):
        ins, outs, (ici_send, ici_recv, d2d_send, d2d_recv, local) = refs[:n], refs[n:2 * n], refs[2 * n:]
        mx, my, mc = _place()
        chip = 2 * mx + my
        own = []
        for a in range(n):
            cp = pltpu.make_async_copy(ins[a], outs[a].at[chip], local.at[a])
            cp.start()
            own.append(cp)
        peers = [(1 - mx if k & 2 else mx, 1 - my if k & 1 else my) for k in range(1, N_CHIPS)]
        fetched = []
        for k, (px, py) in enumerate(peers):
            for a in range(n):
                s = k * n + a
                cp = pltpu.make_async_remote_copy(ins[a].at[:, mc], outs[a].at[chip, :, mc], ici_send.at[s], ici_recv.at[s],
                                                  device_id=(px, py, mc), device_id_type=MESH)
                cp.start()
                fetched.append(cp)
        passed = []
        for k, (px, py) in enumerate(peers):
            for a in range(n):
                s = k * n + a
                fetched[s].wait_recv()
                half = outs[a].at[2 * px + py, :, mc]
                cp = pltpu.make_async_remote_copy(half, half, d2d_send.at[s], d2d_recv.at[s],
                                                  device_id=(mx, my, 1 - mc), device_id_type=MESH)
                cp.start()
                passed.append(cp)
        for cp in fetched:
            cp.wait_send()
        for cp in passed + own:
            cp.wait()

    sems = pltpu.SemaphoreType.DMA((3 * n,))
    return pl.pallas_call(
        body, name="weight_gather", in_specs=[ANY] * n, out_specs=[ANY] * n,
        out_shape=[SDS((N_CHIPS,) + a.shape, a.dtype) for a in arrs],
        scratch_shapes=[sems, sems, sems, sems, pltpu.SemaphoreType.DMA((n,))],
    )(*arrs)


def _sibling_send(arrs, halves, name):
    n = len(arrs)
    out_shape = [SDS(a.shape[:2] + a.shape[3:] if halves else a.shape, a.dtype) for a in arrs]

    def body(*refs):
        ins, outs, (send, recv) = refs[:n], refs[n:2 * n], refs[2 * n:]
        mx, my, mc = _place()
        copies = []
        for a in range(n):
            cp = pltpu.make_async_remote_copy(ins[a].at[:, :, 1 - mc] if halves else ins[a], outs[a], send.at[a], recv.at[a],
                                              device_id=(mx, my, 1 - mc), device_id_type=MESH)
            cp.start()
            copies.append(cp)
        for cp in copies:
            cp.wait()

    return pl.pallas_call(
        body, name=name, in_specs=[ANY] * n, out_specs=[ANY] * n, out_shape=out_shape,
        scratch_shapes=[pltpu.SemaphoreType.DMA((n,)), pltpu.SemaphoreType.DMA((n,))],
    )(*arrs)


def _add_cast(a, b, dtype):
    shape = a.shape
    a, b = _as_rows(a), _as_rows(b)
    R, C = a.shape
    tr = _row_tile(R, C, 3)

    def body(a_ref, b_ref, o_ref):
        o_ref[...] = (a_ref[...] + b_ref[...]).astype(dtype)

    out = pl.pallas_call(body, name="pair_sum", grid=(R // tr,), in_specs=[_rows(tr, C)] * 2, out_specs=_rows(tr, C),
                         out_shape=SDS((R, C), dtype), compiler_params=_params("parallel"))(a, b)
    return out.reshape(shape)


def _pack(arrs):
    flat = jnp.concatenate([a.reshape(-1).astype(F32) for a in arrs])
    pad = (-flat.shape[0]) % (16 * LANES)
    return jnp.pad(flat, (0, pad)).reshape(-1, LANES)


def _unpack(buf, shapes):
    flat = buf.reshape(buf.shape[:-2] + (-1,))
    out, off = [], 0
    for s in shapes:
        size = 1
        for d in s:
            size *= d
        out.append(flat[..., off:off + size].reshape(flat.shape[:-1] + tuple(s)))
        off += size
    return out


def _unshard(stacked, axis):
    moved = jnp.moveaxis(stacked, 0, axis)
    return moved.reshape(moved.shape[:axis] + (N_CHIPS * moved.shape[axis + 1],) + moved.shape[axis + 2:])


def _my_shard(full, axis, chip):
    size = full.shape[axis] // N_CHIPS
    return lax.dynamic_slice_in_dim(full, chip * size, size, axis)


def kernel(x, c, cond_w, cond_b, ada_w, ada_b, norm_g, ffn_w1, ffn_w3, ffn_w2, a_w_in, a_b_in, a_dw, a_dw_b, a_ln_g, a_ln_b, a_w_out, a_b_out, b_w_qkv, b_q_g, b_k_g, b_w_o, p_w, p_b, p_scale, loss_target, m_cond_w, m_cond_b, m_ada_w, m_ada_b, m_norm_g, m_ffn_w1, m_ffn_w3, m_ffn_w2, m_a_w_in, m_a_b_in, m_a_dw, m_a_dw_b, m_a_ln_g, m_a_ln_b, m_a_w_out, m_a_b_out, m_b_w_qkv, m_b_q_g, m_b_k_g, m_b_w_o, m_p_w, m_p_b, m_p_scale, v_cond_w, v_cond_b, v_ada_w, v_ada_b, v_norm_g, v_ffn_w1, v_ffn_w3, v_ffn_w2, v_a_w_in, v_a_b_in, v_a_dw, v_a_dw_b, v_a_ln_g, v_a_ln_b, v_a_w_out, v_a_b_out, v_b_w_qkv, v_b_q_g, v_b_k_g, v_b_w_o, v_p_w, v_p_b, v_p_scale):
    w_in = dict(cond_w=cond_w, cond_b=cond_b, ada_w=ada_w, ada_b=ada_b, norm_g=norm_g, ffn_w1=ffn_w1, ffn_w3=ffn_w3, ffn_w2=ffn_w2,
                a_w_in=a_w_in, a_b_in=a_b_in, a_dw=a_dw, a_dw_b=a_dw_b, a_ln_g=a_ln_g, a_ln_b=a_ln_b, a_w_out=a_w_out, a_b_out=a_b_out,
                b_w_qkv=b_w_qkv, b_q_g=b_q_g, b_k_g=b_k_g, b_w_o=b_w_o, p_w=p_w, p_b=p_b, p_scale=p_scale)
    m_in = dict(zip(WEIGHTS, (m_cond_w, m_cond_b, m_ada_w, m_ada_b, m_norm_g, m_ffn_w1, m_ffn_w3, m_ffn_w2, m_a_w_in, m_a_b_in, m_a_dw,
                              m_a_dw_b, m_a_ln_g, m_a_ln_b, m_a_w_out, m_a_b_out, m_b_w_qkv, m_b_q_g, m_b_k_g, m_b_w_o, m_p_w, m_p_b, m_p_scale)))
    v_in = dict(zip(WEIGHTS, (v_cond_w, v_cond_b, v_ada_w, v_ada_b, v_norm_g, v_ffn_w1, v_ffn_w3, v_ffn_w2, v_a_w_in, v_a_b_in, v_a_dw,
                              v_a_dw_b, v_a_ln_g, v_a_ln_b, v_a_w_out, v_a_b_out, v_b_w_qkv, v_b_q_g, v_b_k_g, v_b_w_o, v_p_w, v_p_b, v_p_scale)))
    x = x[0]
    target = loss_target[0]
    S, D = x.shape
    L = ada_w.shape[0]
    assert b_q_g.shape[-1] == HEAD_DIM and D % LANES == 0 and S % HALO == 0
    mx, my, mc = _place()
    chip = 2 * mx + my
    me = 2 * chip + mc

    big = [ffn_w1.astype(BF16), ffn_w3.astype(BF16), ffn_w2.astype(BF16), a_w_in.astype(BF16), a_w_out.astype(BF16),
           b_w_qkv.astype(BF16), b_w_o.astype(BF16)]
    sharded = [k for k, ax in SMALL.items() if ax is not None]
    shards = big + [_pack([w_in[k] for k in sharded])]
    by_half = [a.reshape((-1, 2, a.shape[-2] // 2, a.shape[-1])) for a in shards]
    gathered = [g.reshape((N_CHIPS,) + a.shape) for g, a in zip(_weight_gather(by_half), shards)]
    g_w1, g_w3, g_w2, g_ain, g_aout, g_qkv, g_wo, g_small = gathered
    full = {k: _unshard(a, SMALL[k]) for k, a in zip(sharded, _unpack(g_small, [w_in[k].shape for k in sharded]))}
    n_conv, n_pool = a_w_in.shape[0], p_w.shape[0]
    conv_dw = [jnp.pad(full['a_dw'][ia], ((0, 1), (0, 0))) for ia in range(n_conv)]
    pool_w = [full['p_w'][ic].astype(BF16) for ic in range(n_pool)]
    qk_scale = LOG2E * HEAD_DIM ** -0.5
    gq = jnp.tile(b_q_g[0], D // HEAD_DIM).reshape(1, D) * qk_scale
    gk = jnp.tile(b_k_g[0], D // HEAD_DIM).reshape(1, D)

    c_all = _allgather8(c)[:, 0, :]
    c_cols = _my_shard(c_all, 1, chip)
    pre, e = _cond_e(_allgather8(_cond_pre(c_cols, cond_w)), cond_b.reshape(1, D))
    nc = ada_w.shape[2]
    mod_c = _mod_cols(e, ada_w, _my_shard(ada_b, 1, chip).reshape(L, 1, nc))
    mod_all = _allgather8(mod_c.reshape(L * N_DEV, nc)).reshape(N_CHIPS, 2, L, N_DEV, nc)[:, 0]
    mod = jnp.moveaxis(lax.dynamic_index_in_dim(mod_all, me, axis=2, keepdims=False), 0, 1).reshape(L, 3, 3, D)
    shift, scale, gate = mod[:, :, 0], mod[:, :, 1], 1.0 + mod[:, :, 2]
    gains = full['norm_g']

    def mod_vec(i, k, gate_factor=1.0):
        return _vec(gains[i, k] * (1.0 + scale[i, k]), shift[i, k], gate_factor * gate[i, k])

    saved = []
    ia = ib = ic = 0
    for i in range(L):
        for k, half in ((0, 0), (1, None), (2, 1)):
            if half is not None:
                vec = mod_vec(i, k, 0.5)
                w1, w3, w2 = g_w1[:, i, half], g_w3[:, i, half], g_w2[:, i, half]
                xo, a1, a3, y = _ffn_fwd(x, vec, w1, w3, w2)
                saved.append(('ffn', i, k, half, x, vec, (a1, a3, y, w1, w3, w2)))
            elif i % 3 == 0:
                vec = mod_vec(i, k)
                w_a = g_ain[:, ia]
                w_o = g_aout[:, ia].reshape(D, D)
                p, u = _in_fwd(x, vec, w_a, full['a_b_in'][ia].reshape(1, 2 * D), True, "conv_in_fwd")
                cvec = _vec(gate[i, k], full['a_dw_b'][ia], full['a_ln_g'][ia], full['a_ln_b'][ia], full['a_b_out'][ia])
                xo, v, y = _conv_mid_fwd(u, x, cvec, conv_dw[ia], w_o)
                saved.append(('conv', i, k, ia, x, vec, (p, u, v, y, cvec, w_a, w_o)))
                ia += 1
            elif i % 3 == 1:
                vec = mod_vec(i, k)
                w_q = g_qkv[:, ib]
                w_o = g_wo[:, ib].reshape(D, D)
                raw, = _in_fwd(x, vec, w_q, jnp.zeros((1, 3 * D), F32), False, "attn_in_fwd")
                qkv = _qknorm_fwd(raw, gq, gk)
                o, tot, start = _attn_fwd(qkv)
                xo, y = _out_fwd(x, o, vec, w_o)
                saved.append(('attn', i, k, ib, x, vec, (raw, qkv, o, (tot, start), y, w_q, w_o)))
                ib += 1
            else:
                vec = _vec(gains[i, k] * (1.0 + scale[i, k]), shift[i, k], gate[i, k], full['p_scale'][ic], full['p_b'][ic].reshape(D))
                xo = _pool_fwd(x, vec, pool_w[ic])
                saved.append(('pool', i, k, ic, x, vec, ()))
                ic += 1
            x = xo

    dx, sq = _loss_head(x, target)
    loss = lax.psum(0.5 / D * jnp.sum(sq[0]), ("x", "y", "c"))

    zeros_like_full = lambda k: jnp.zeros(full[k].shape, F32)
    g_full = {k: zeros_like_full(k) for k in sharded}
    g_full['b_q_g'] = jnp.zeros_like(b_q_g)
    g_full['b_k_g'] = jnp.zeros_like(b_k_g)
    dmod = jnp.zeros((L, 3, 3, D), F32)
    fc = ffn_w1.shape[-1]
    n_attn = b_w_qkv.shape[0]
    big_grads = {'ffn_w1': jnp.zeros((2 * L, N_CHIPS, D, fc), F32), 'ffn_w3': jnp.zeros((2 * L, N_CHIPS, D, fc), F32),
                 'ffn_w2': jnp.zeros((2 * L, N_CHIPS, fc, D), F32), 'a_w_in': jnp.zeros((n_conv, N_CHIPS, D, 2 * D // N_CHIPS), F32),
                 'a_w_out': jnp.zeros((n_conv, 1, D, D), F32), 'b_w_qkv': jnp.zeros((n_attn, N_CHIPS, D, 3 * D // N_CHIPS), F32),
                 'b_w_o': jnp.zeros((n_attn, 1, D, D), F32)}

    def wgrad(name, slot, a, b, a_mode, b_mode, nch):
        big_grads[name] = _wgrad(a, b, a_mode, b_mode, nch, name + "_grad", big_grads[name], slot)

    def put(name, idx, val):
        g_full[name] = g_full[name].at[idx].set(val.reshape(g_full[name][idx].shape))

    for kind, i, k, idx, xin, vec, res in reversed(saved):
        if kind == 'ffn':
            a1, a3, y, w1, w3, w2 = res
            dx, h, dy, u, da1, da3, sums = _ffn_bwd(xin, dx, a1, a3, y, vec, w1, w3, w2)
            wgrad('ffn_w1', 2 * i + idx, h, da1, 'transposed', 'lead', N_CHIPS)
            wgrad('ffn_w3', 2 * i + idx, h, da3, 'transposed', 'lead', N_CHIPS)
            wgrad('ffn_w2', 2 * i + idx, u, dy, 'lead', 'full', N_CHIPS)
            dgate = 0.5 * sums[2]
        elif kind == 'conv':
            p, u, v, y, cvec, w_a, w_o = res
            dv, q, dout, csums = _conv_mid_bwd(dx, y, v, cvec, w_o)
            wgrad('a_w_out', idx, q, dout, 'full', 'full', 1)
            du, gdw = _conv_transpose(dv, u, conv_dw[idx])
            dp, psums = _glu_bwd(du, p)
            dx, h, sums = _in_bwd(xin, dx, dp, vec, w_a, "conv_in_bwd")
            wgrad('a_w_in', idx, h, dp, 'full', 'col', N_CHIPS)
            dgate = csums[0]
            put('a_b_out', idx, csums[1])
            put('a_ln_g', idx, csums[2])
            put('a_ln_b', idx, csums[3])
            put('a_dw_b', idx, csums[4])
            put('a_dw', idx, gdw[:-1])
            put('a_b_in', idx, psums[0])
        elif kind == 'attn':
            raw, qkv, o, tot, y, w_q, w_o = res
            do, dout, osums = _out_bwd(dx, y, vec, w_o)
            wgrad('b_w_o', idx, o, dout, 'full', 'full', 1)
            dq, dk, dvv = _attn_bwd(qkv, do, *tot)
            draw, qsums = _qknorm_bwd(dq, dk, dvv, raw, gq, gk)
            dx, h, sums = _in_bwd(xin, dx, draw, vec, w_q, "attn_in_bwd")
            wgrad('b_w_qkv', idx, h, draw, 'full', 'col', N_CHIPS)
            dgate = osums[2]
            put('b_q_g', idx, qk_scale * jnp.sum(qsums[0].reshape(-1, HEAD_DIM), axis=0))
            put('b_k_g', idx, jnp.sum(qsums[1].reshape(-1, HEAD_DIM), axis=0))
        else:
            dx, gpw, sums = _pool_bwd(xin, dx, vec, pool_w[idx])
            dgate = sums[2]
            put('p_w', idx, gpw)
            put('p_scale', idx, sums[3])
            put('p_b', idx, sums[4])
        put('norm_g', (i, k), sums[0] * (1.0 + scale[i, k]))
        dmod = dmod.at[i, k].set(jnp.stack([sums[1], sums[0] * gains[i, k], dgate]))
    grad_x = dx[None]

    dmod_all = _allgather8(dmod.reshape(L, 9 * D))
    g_ada_b = _sum_lead(dmod_all)
    dmod_cols = jnp.moveaxis(_my_shard(dmod_all, 2, chip), 0, 1)
    g_ada_w, de_part = _mod_bwd(e, dmod_cols, ada_w)
    g_cond_w, g_cond_b = _cond_bwd(_allgather8(de_part), pre, c_cols)

    small_names = list(SMALL)
    g_full['cond_b'] = g_cond_b.reshape(D)
    g_full['ada_b'] = g_ada_b
    reduced = [k for k in small_names if k not in ('cond_b', 'ada_b')]
    red = _unpack(_sum_lead(_allgather8(_pack([g_full[k] for k in reduced]))), [g_full[k].shape for k in reduced])
    for k, a in zip(reduced, red):
        g_full[k] = a
    grads = {k: (g_full[k] if SMALL[k] is None else _my_shard(g_full[k], SMALL[k], chip)) for k in small_names}
    grads['cond_w'] = g_cond_w
    grads['ada_w'] = g_ada_w

    big_names = list(big_grads)
    by_half = []
    for name in big_names:
        g = big_grads[name]
        rows = g.shape[1] * g.shape[2] // N_CHIPS
        by_half.append(g.reshape(g.shape[0], N_CHIPS, 2, rows // 2, g.shape[3]))
    got = _sibling_send(by_half, True, "pair_send")
    pair = [_add_cast(lax.dynamic_index_in_dim(a, mc, axis=2, keepdims=False), b, BF16) for a, b in zip(by_half, got)]
    mine = [_sum_lead(a) for a in _grad_scatter(pair)]
    theirs = _sibling_send(mine, False, "pair_return")
    for name, a, b in zip(big_names, mine, theirs):
        both = jnp.where(mc == 0, jnp.stack([a, b], axis=1), jnp.stack([b, a], axis=1))
        grads[name] = both.reshape(w_in[name].shape)

    delta, new_m, new_v = {}, {}, {}
    packed = [_pack([d[k] for k in small_names]) for d in (w_in, grads, m_in, v_in)]
    shapes = [w_in[k].shape for k in small_names]
    for out, buf in zip((delta, new_m, new_v), _adamw(*packed)):
        out.update(zip(small_names, _unpack(buf, shapes)))
    for k in WEIGHTS:
        if k not in SMALL:
            delta[k], new_m[k], new_v[k] = _adamw(w_in[k], grads[k], m_in[k], v_in[k])
    return (loss, grad_x, *[grads[k] for k in WEIGHTS], *[delta[k] for k in WEIGHTS], *[new_m[k] for k in WEIGHTS],
            *[new_v[k] for k in WEIGHTS])
```

```python
import functools

import jax
import jax.numpy as jnp
from jax import lax
from jax.experimental import pallas as pl
from jax.experimental.pallas import tpu as pltpu

F32 = jnp.float32
BF16 = jnp.bfloat16
SDS = jax.ShapeDtypeStruct
MESH = pl.DeviceIdType.MESH
ANY = pl.BlockSpec(memory_space=pl.ANY)

EPS = 1e-6
N_CHIPS = 4
N_DEV = 8
LANES = 128
HEAD_DIM = 64
VMEM_LIMIT_BYTES = 56 * 2**20
TOKEN_TILE = 512
WGRAD_TILE = 1024
ATTN_Q_TILE = 1024
ATTN_K_TILE = 256
ATTN_DEAD_BITS = 160.0
LOG2E = 1.4426950408889634
LN2 = 0.6931471805599453
HALO = 32
CONV_ROWS, CONV_COLS = 32, 256
ADAM_TILE_BYTES = 1 << 20
POOL_LEVELS = (1, 2, 3, 4)

ADAM_LR, ADAM_B1, ADAM_B2, ADAM_EPS, ADAM_WD, ADAM_STEP = 0.001, 0.9, 0.999, 1e-08, 0.01, 10

WEIGHTS = ['cond_w', 'cond_b', 'ada_w', 'ada_b', 'norm_g', 'ffn_w1', 'ffn_w3', 'ffn_w2', 'a_w_in', 'a_b_in', 'a_dw',
           'a_dw_b', 'a_ln_g', 'a_ln_b', 'a_w_out', 'a_b_out', 'b_w_qkv', 'b_q_g', 'b_k_g', 'b_w_o', 'p_w', 'p_b', 'p_scale']
SMALL = {'norm_g': 2, 'a_b_in': 1, 'a_dw': 2, 'a_dw_b': 1, 'a_ln_g': 1, 'a_ln_b': 1, 'a_b_out': 1, 'p_w': 2, 'p_b': 2,
         'p_scale': 1, 'cond_b': None, 'ada_b': None, 'b_q_g': None, 'b_k_g': None}


def _tile(n, pref):
    return pref if n % pref == 0 else n


def _params(*sem):
    return pltpu.CompilerParams(dimension_semantics=sem, vmem_limit_bytes=VMEM_LIMIT_BYTES)


def _resident(shape):
    nd = len(shape)
    return pl.BlockSpec(shape, lambda *_: (0,) * nd, pipeline_mode=pl.Buffered(1))


def _rows(tm, width):
    return pl.BlockSpec((tm, width), lambda i: (i, 0))


def _acc_spec(rows, width):
    return pl.BlockSpec((rows, width), lambda i: (0, 0))


def _dot(a, b):
    return jnp.dot(a, b, preferred_element_type=F32)


def _dot_nt(a, b):
    return lax.dot_general(a, b, (((1,), (1,)), ((), ())), preferred_element_type=F32)


def _dot_tn(a, b):
    return lax.dot_general(a, b, (((0,), (0,)), ((), ())), preferred_element_type=F32)


def _hdot(a, b, dims=(((1,), (0,)), ((), ()))):
    return lax.dot_general(a, b, dims, preferred_element_type=F32, precision=lax.Precision.HIGHEST)


def _zero_at_first_step(*refs):
    @pl.when(pl.program_id(0) == 0)
    def _():
        for r in refs:
            r[...] = jnp.zeros_like(r)


def _add_rowsum(ref, row, t):
    ref[row:row + 1, :] += jnp.sum(t, axis=0, keepdims=True)


def _rms(x):
    r = lax.rsqrt(jnp.mean(x * x, axis=-1, keepdims=True) + EPS)
    return x * r, r


def _modulate_bwd(dh, n, r, gs, dxo, sums_ref):
    _add_rowsum(sums_ref, 0, dh * n)
    _add_rowsum(sums_ref, 1, dh)
    dn = dh * gs
    return dxo + r * (dn - n * jnp.mean(dn * n, axis=-1, keepdims=True))


def _silu_grad(a, sg):
    return sg * (1.0 + a * (1.0 - sg))


def _vec(*rows):
    d = rows[0].shape[-1]
    rows = [r.reshape(1, d).astype(F32) for r in rows]
    return jnp.concatenate(rows + [jnp.zeros((8 - len(rows), d), F32)], axis=0)


def _ffn_fwd(x, vec, w1, w3, w2):
    S, D = x.shape
    nch, _, fc = w1.shape
    tm = _tile(S, TOKEN_TILE)

    def body(x_ref, vec_ref, w1_ref, w3_ref, w2_ref, xo_ref, a1_ref, a3_ref, y_ref):
        x = x_ref[...]
        n, _ = _rms(x)
        h = (n * vec_ref[0:1, :] + vec_ref[1:2, :]).astype(BF16)
        acc = jnp.zeros((tm, D), F32)
        for j in range(nch):
            a1 = _dot(h, w1_ref[j]).astype(BF16)
            a3 = _dot(h, w3_ref[j]).astype(BF16)
            a1_ref[j] = a1
            a3_ref[j] = a3
            a1 = a1.astype(F32)
            u = a1 * jax.nn.sigmoid(a1) * a3.astype(F32)
            acc = acc + _dot(u.astype(BF16), w2_ref[j])
        y_ref[...] = acc.astype(BF16)
        xo_ref[...] = x + vec_ref[2:3, :] * acc

    chunked = pl.BlockSpec((nch, tm, fc), lambda i: (0, i, 0))
    return pl.pallas_call(
        body, name="ffn_fwd", grid=(S // tm,),
        in_specs=[_rows(tm, D), _resident((8, D)), _resident(w1.shape), _resident(w3.shape), _resident(w2.shape)],
        out_specs=[_rows(tm, D), chunked, chunked, _rows(tm, D)],
        out_shape=[SDS((S, D), F32), SDS((nch, S, fc), BF16), SDS((nch, S, fc), BF16), SDS((S, D), BF16)],
        compiler_params=_params("parallel"),
    )(x, vec, w1, w3, w2)


def _ffn_bwd(x, dxo, a1, a3, y, vec, w1, w3, w2):
    S, D = x.shape
    nch, _, fc = w1.shape
    tm = _tile(S, TOKEN_TILE // 2)

    def body(x_ref, dxo_ref, a1_ref, a3_ref, y_ref, vec_ref, w1_ref, w3_ref, w2_ref,
             dx_ref, h_ref, dy_ref, u_ref, da1_ref, da3_ref, sums_ref):
        _zero_at_first_step(sums_ref)
        x = x_ref[...]
        dxo = dxo_ref[...]
        gs = vec_ref[0:1, :]
        n, r = _rms(x)
        h_ref[...] = (n * gs + vec_ref[1:2, :]).astype(BF16)
        _add_rowsum(sums_ref, 2, dxo * y_ref[...].astype(F32))
        dy = (vec_ref[2:3, :] * dxo).astype(BF16)
        dy_ref[...] = dy
        dh = jnp.zeros((tm, D), F32)
        for j in range(nch):
            a1 = a1_ref[j].astype(F32)
            a3 = a3_ref[j].astype(F32)
            sg = jax.nn.sigmoid(a1)
            s = a1 * sg
            du = _dot_nt(dy, w2_ref[j])
            da1 = (du * a3 * _silu_grad(a1, sg)).astype(BF16)
            da3 = (du * s).astype(BF16)
            u_ref[j] = (s * a3).astype(BF16)
            da1_ref[j] = da1
            da3_ref[j] = da3
            dh = dh + _dot_nt(da1, w1_ref[j]) + _dot_nt(da3, w3_ref[j])
        dx_ref[...] = _modulate_bwd(dh, n, r, gs, dxo, sums_ref)

    chunked = pl.BlockSpec((nch, tm, fc), lambda i: (0, i, 0))
    return pl.pallas_call(
        body, name="ffn_bwd", grid=(S // tm,),
        in_specs=[_rows(tm, D), _rows(tm, D), chunked, chunked, _rows(tm, D), _resident((8, D)),
                  _resident(w1.shape), _resident(w3.shape), _resident(w2.shape)],
        out_specs=[_rows(tm, D), _rows(tm, D), _rows(tm, D), chunked, chunked, chunked, _acc_spec(8, D)],
        out_shape=[SDS((S, D), F32), SDS((S, D), BF16), SDS((S, D), BF16), SDS((nch, S, fc), BF16),
                   SDS((nch, S, fc), BF16), SDS((nch, S, fc), BF16), SDS((8, D), F32)],
        compiler_params=_params("arbitrary"),
    )(x, dxo, a1, a3, y, vec, w1, w3, w2)


def _wgrad(a, b, a_mode, b_mode, nch, name, acc, slot):
    S = a.shape[-2]
    M = a.shape[-1]
    N = b.shape[-1] // (nch if b_mode == 'col' else 1)
    assert acc.shape[1:] == (nch, M, N)
    ts = _tile(S, WGRAD_TILE)

    def spec(mode, width):
        if mode == 'full':
            return pl.BlockSpec((ts, width), lambda j, s: (s, 0))
        if mode == 'lead':
            return pl.BlockSpec((None, ts, width), lambda j, s: (j, s, 0))
        return pl.BlockSpec((ts, width), lambda j, s: (s, j))

    def body(a_ref, b_ref, acc_ref, o_ref):
        @pl.when(pl.program_id(1) == 0)
        def _():
            o_ref[...] = jnp.zeros_like(o_ref)
        o_ref[...] += _dot_tn(a_ref[...], b_ref[...])

    return pl.pallas_call(
        body, name=name, grid=(nch, S // ts), in_specs=[spec(a_mode, M), spec(b_mode, N), ANY],
        out_specs=pl.BlockSpec((None, None, M, N), lambda j, s: (slot, j, 0, 0)), out_shape=SDS(acc.shape, F32),
        input_output_aliases={2: 0}, compiler_params=_params("parallel", "arbitrary"),
    )(a, b, acc)


def _in_fwd(x, vec, w, bias, glu, name):
    S, D = x.shape
    nch, _, nc = w.shape
    N = nch * nc
    tm = _tile(S, TOKEN_TILE)

    def body(x_ref, vec_ref, w_ref, b_ref, p_ref, *u_ref):
        n, _ = _rms(x_ref[...])
        h = (n * vec_ref[0:1, :] + vec_ref[1:2, :]).astype(BF16)
        for j in range(nch):
            cols = slice(j * nc, (j + 1) * nc)
            p_ref[:, cols] = (_dot(h, w_ref[j]) + b_ref[:, cols]).astype(p_ref.dtype)
        if glu:
            half = N // 2
            u_ref[0][...] = p_ref[:, :half].astype(F32) * jax.nn.sigmoid(p_ref[:, half:].astype(F32))

    out_specs = [_rows(tm, N)] + ([_rows(tm, N // 2)] if glu else [])
    out_shape = [SDS((S, N), BF16 if glu else F32)] + ([SDS((S, N // 2), F32)] if glu else [])
    return pl.pallas_call(
        body, name=name, grid=(S // tm,),
        in_specs=[_rows(tm, D), _resident((8, D)), _resident(w.shape), _resident((1, N))],
        out_specs=out_specs, out_shape=out_shape, compiler_params=_params("parallel"),
    )(x, vec, w, bias)


def _in_bwd(x, dxo, dp, vec, w, name):
    S, D = x.shape
    nch, _, nc = w.shape
    tm = _tile(S, TOKEN_TILE)

    def body(x_ref, dxo_ref, dp_ref, vec_ref, w_ref, dx_ref, h_ref, sums_ref):
        _zero_at_first_step(sums_ref)
        gs = vec_ref[0:1, :]
        n, r = _rms(x_ref[...])
        h_ref[...] = (n * gs + vec_ref[1:2, :]).astype(BF16)
        dh = jnp.zeros((tm, D), F32)
        for j in range(nch):
            dh = dh + _dot_nt(dp_ref[:, j * nc:(j + 1) * nc], w_ref[j])
        dx_ref[...] = _modulate_bwd(dh, n, r, gs, dxo_ref[...], sums_ref)

    return pl.pallas_call(
        body, name=name, grid=(S // tm,),
        in_specs=[_rows(tm, D), _rows(tm, D), _rows(tm, nch * nc), _resident((8, D)), _resident(w.shape)],
        out_specs=[_rows(tm, D), _rows(tm, D), _acc_spec(8, D)],
        out_shape=[SDS((S, D), F32), SDS((S, D), BF16), SDS((8, D), F32)],
        compiler_params=_params("arbitrary"),
    )(x, dxo, dp, vec, w)


def _out_fwd(x, t, vec, w):
    S, D = x.shape
    tm = _tile(S, TOKEN_TILE)

    def body(x_ref, t_ref, vec_ref, w_ref, xo_ref, y_ref):
        y = _dot(t_ref[...], w_ref[...])
        y_ref[...] = y.astype(BF16)
        xo_ref[...] = x_ref[...] + vec_ref[2:3, :] * y

    return pl.pallas_call(
        body, name="attn_out_fwd", grid=(S // tm,),
        in_specs=[_rows(tm, D), _rows(tm, t.shape[1]), _resident((8, D)), _resident(w.shape)],
        out_specs=[_rows(tm, D), _rows(tm, D)], out_shape=[SDS((S, D), F32), SDS((S, D), BF16)],
        compiler_params=_params("parallel"),
    )(x, t, vec, w)


def _out_bwd(dxo, y, vec, w):
    S, D = dxo.shape
    K = w.shape[0]
    tm = _tile(S, TOKEN_TILE)

    def body(dxo_ref, y_ref, vec_ref, w_ref, dt_ref, dout_ref, sums_ref):
        _zero_at_first_step(sums_ref)
        dxo = dxo_ref[...]
        _add_rowsum(sums_ref, 2, dxo * y_ref[...].astype(F32))
        dout = (vec_ref[2:3, :] * dxo).astype(BF16)
        dout_ref[...] = dout
        dt_ref[...] = _dot_nt(dout, w_ref[...]).astype(BF16)

    return pl.pallas_call(
        body, name="attn_out_bwd", grid=(S // tm,),
        in_specs=[_rows(tm, D), _rows(tm, D), _resident((8, D)), _resident(w.shape)],
        out_specs=[_rows(tm, K), _rows(tm, D), _acc_spec(8, D)],
        out_shape=[SDS((S, K), BF16), SDS((S, D), BF16), SDS((8, D), F32)],
        compiler_params=_params("arbitrary"),
    )(dxo, y, vec, w)


def _prev_halo(tm, width):
    return pl.BlockSpec((HALO, width), lambda i: (jnp.maximum(i * (tm // HALO) - 1, 0), 0))


def _next_halo(tm, width, n_rows):
    last = n_rows // HALO - 1
    return pl.BlockSpec((HALO, width), lambda i: (jnp.minimum((i + 1) * (tm // HALO), last), 0))


def _layer_norm(v, g, b):
    mu = jnp.mean(v, axis=-1, keepdims=True)
    vc = v - mu
    rstd = lax.rsqrt(jnp.mean(vc * vc, axis=-1, keepdims=True) + EPS)
    vh = vc * rstd
    return vh * g + b, vh, rstd


def _fill_shifts(ext, sh, n):
    ext[n:, :] = jnp.zeros((8, ext.shape[1]), F32)
    for b in range(1, 8):
        sh[b - 1] = ext[pl.ds(b, n), :]


def _shifted(ext, sh, off, r0, rows, cols):
    b = off % 8
    base = r0 + off - b
    return ext[pl.ds(base, rows), cols] if b == 0 else sh[b - 1, pl.ds(base, rows), cols]


def _conv_mid_fwd(u, x, vec, dw, w_out):
    S, D = x.shape
    taps = dw.shape[0] - 1
    tm = _tile(S, TOKEN_TILE // 2)

    def body(u_ref, uh_ref, x_ref, vec_ref, dw_ref, w_ref, xo_ref, v_ref, y_ref, ext, sh):
        n = HALO + tm
        ext[0:HALO, :] = jnp.where(pl.program_id(0) > 0, uh_ref[...], 0.0)
        ext[HALO:n, :] = u_ref[...]
        _fill_shifts(ext, sh, n)
        cb = min(CONV_COLS, D)
        for c0 in range(0, D, cb):
            cols = slice(c0, c0 + cb)
            w = [dw_ref[k:k + 1, cols] for k in range(taps)]
            for r0 in range(0, tm, CONV_ROWS):
                acc = jnp.zeros((CONV_ROWS, cb), F32) + vec_ref[1:2, cols]
                for k in range(taps):
                    acc = acc + w[k] * _shifted(ext, sh, HALO - (taps - 1) + k, r0, CONV_ROWS, cols)
                v_ref[r0:r0 + CONV_ROWS, cols] = acc
        v = v_ref[...]
        l, _, _ = _layer_norm(v, vec_ref[2:3, :], vec_ref[3:4, :])
        q = (l * jax.nn.sigmoid(l)).astype(BF16)
        y = _dot(q, w_ref[...]) + vec_ref[4:5, :]
        y_ref[...] = y.astype(BF16)
        xo_ref[...] = x_ref[...] + vec_ref[0:1, :] * y

    return pl.pallas_call(
        body, name="conv_mid_fwd", grid=(S // tm,),
        in_specs=[_rows(tm, D), _prev_halo(tm, D), _rows(tm, D), _resident((8, D)), _resident(dw.shape), _resident(w_out.shape)],
        out_specs=[_rows(tm, D), _rows(tm, D), _rows(tm, D)],
        out_shape=[SDS((S, D), F32), SDS((S, D), F32), SDS((S, D), BF16)],
        scratch_shapes=[pltpu.VMEM((HALO + tm + 8, D), F32), pltpu.VMEM((7, HALO + tm, D), F32)],
        compiler_params=_params("parallel"),
    )(u, u, x, vec, dw, w_out)


def _conv_mid_bwd(dxo, y, v, vec, w_out):
    S, D = dxo.shape
    tm = _tile(S, TOKEN_TILE)

    def body(dxo_ref, y_ref, v_ref, vec_ref, w_ref, dv_ref, q_ref, dout_ref, sums_ref):
        _zero_at_first_step(sums_ref)
        dxo = dxo_ref[...]
        _add_rowsum(sums_ref, 0, dxo * y_ref[...].astype(F32))
        dout = vec_ref[0:1, :] * dxo
        _add_rowsum(sums_ref, 1, dout)
        dout = dout.astype(BF16)
        dout_ref[...] = dout
        ln_g = vec_ref[2:3, :]
        l, vh, rstd = _layer_norm(v_ref[...], ln_g, vec_ref[3:4, :])
        sg = jax.nn.sigmoid(l)
        q_ref[...] = (l * sg).astype(BF16)
        dl = _dot_nt(dout, w_ref[...]) * _silu_grad(l, sg)
        _add_rowsum(sums_ref, 2, dl * vh)
        _add_rowsum(sums_ref, 3, dl)
        dvh = dl * ln_g
        dv = rstd * (dvh - jnp.mean(dvh, axis=-1, keepdims=True) - vh * jnp.mean(dvh * vh, axis=-1, keepdims=True))
        _add_rowsum(sums_ref, 4, dv)
        dv_ref[...] = dv

    return pl.pallas_call(
        body, name="conv_mid_bwd", grid=(S // tm,),
        in_specs=[_rows(tm, D), _rows(tm, D), _rows(tm, D), _resident((8, D)), _resident(w_out.shape)],
        out_specs=[_rows(tm, D), _rows(tm, D), _rows(tm, D), _acc_spec(8, D)],
        out_shape=[SDS((S, D), F32), SDS((S, D), BF16), SDS((S, D), BF16), SDS((8, D), F32)],
        compiler_params=_params("arbitrary"),
    )(dxo, y, v, vec, w_out)


def _conv_transpose(dv, u, dw):
    S, D = dv.shape
    taps = dw.shape[0] - 1
    tm = _tile(S, TOKEN_TILE // 2)

    def body(dv_ref, dvn_ref, u_ref, uh_ref, dw_ref, du_ref, gdw_ref, extv, shv, extu, shu):
        _zero_at_first_step(gdw_ref)
        i = pl.program_id(0)
        n = HALO + tm
        extv[0:tm, :] = dv_ref[...]
        extv[tm:n, :] = jnp.where(i < pl.num_programs(0) - 1, dvn_ref[...], 0.0)
        _fill_shifts(extv, shv, n)
        extu[0:HALO, :] = jnp.where(i > 0, uh_ref[...], 0.0)
        extu[HALO:n, :] = u_ref[...]
        _fill_shifts(extu, shu, n)
        cb = min(CONV_COLS, D)
        for c0 in range(0, D, cb):
            cols = slice(c0, c0 + cb)
            w = [dw_ref[k:k + 1, cols] for k in range(taps)]
            for r0 in range(0, tm, CONV_ROWS):
                acc = jnp.zeros((CONV_ROWS, cb), F32)
                for k in range(taps):
                    acc = acc + w[k] * _shifted(extv, shv, taps - 1 - k, r0, CONV_ROWS, cols)
                du_ref[r0:r0 + CONV_ROWS, cols] = acc
        for c0 in range(0, D, LANES):
            cols = slice(c0, c0 + LANES)
            accs = [jnp.zeros((8, LANES), F32) for _ in range(taps)]
            for r0 in range(0, tm, CONV_ROWS):
                dvb = extv[r0:r0 + CONV_ROWS, cols]
                for k in range(taps):
                    p = dvb * _shifted(extu, shu, HALO - (taps - 1) + k, r0, CONV_ROWS, cols)
                    for s in range(0, CONV_ROWS, 8):
                        accs[k] = accs[k] + p[s:s + 8]
            for k in range(taps):
                gdw_ref[k:k + 1, cols] += jnp.sum(accs[k], axis=0, keepdims=True)

    return pl.pallas_call(
        body, name="conv_transpose", grid=(S // tm,),
        in_specs=[_rows(tm, D), _next_halo(tm, D, S), _rows(tm, D), _prev_halo(tm, D), _resident(dw.shape)],
        out_specs=[_rows(tm, D), _acc_spec(dw.shape[0], D)],
        out_shape=[SDS((S, D), F32), SDS(dw.shape, F32)],
        scratch_shapes=[pltpu.VMEM((HALO + tm + 8, D), F32), pltpu.VMEM((7, HALO + tm, D), F32),
                        pltpu.VMEM((HALO + tm + 8, D), F32), pltpu.VMEM((7, HALO + tm, D), F32)],
        compiler_params=_params("arbitrary"),
    )(dv, dv, u, u, dw)


def _glu_bwd(du, p):
    S, D = du.shape
    tm = _tile(S, TOKEN_TILE)

    def body(du_ref, p_ref, dp_ref, sums_ref):
        _zero_at_first_step(sums_ref)
        du = du_ref[...]
        a = p_ref[:, :D].astype(F32)
        sb = jax.nn.sigmoid(p_ref[:, D:].astype(F32))
        da = du * sb
        db = du * a * sb * (1.0 - sb)
        dp_ref[:, :D] = da.astype(BF16)
        dp_ref[:, D:] = db.astype(BF16)
        sums_ref[0:1, :D] += jnp.sum(da, axis=0, keepdims=True)
        sums_ref[0:1, D:] += jnp.sum(db, axis=0, keepdims=True)

    return pl.pallas_call(
        body, name="glu_bwd", grid=(S // tm,), in_specs=[_rows(tm, D), _rows(tm, 2 * D)],
        out_specs=[_rows(tm, 2 * D), _acc_spec(8, 2 * D)], out_shape=[SDS((S, 2 * D), BF16), SDS((8, 2 * D), F32)],
        compiler_params=_params("arbitrary"),
    )(du, p)


def _head_mean(t, bd):
    hi = t.astype(BF16)
    lo = (t - hi.astype(F32)).astype(BF16)
    return (_dot(hi, bd) + _dot(lo, bd)) * (1.0 / HEAD_DIM)


def _head_blocks():
    r = lax.broadcasted_iota(jnp.int32, (LANES, LANES), 0) // HEAD_DIM
    c = lax.broadcasted_iota(jnp.int32, (LANES, LANES), 1) // HEAD_DIM
    return (r == c).astype(BF16)


def _qknorm_fwd(raw, gq, gk):
    S, D3 = raw.shape
    D = D3 // 3
    tm = _tile(S, TOKEN_TILE)

    def body(raw_ref, gq_ref, gk_ref, o_ref):
        bd = _head_blocks()
        for off, g_ref in ((0, gq_ref), (D, gk_ref)):
            for c in range(D // LANES):
                cols = slice(off + c * LANES, off + (c + 1) * LANES)
                xs = raw_ref[:, cols]
                r = lax.rsqrt(_head_mean(xs * xs, bd) + EPS)
                o_ref[:, cols] = (xs * r * g_ref[:, c * LANES:(c + 1) * LANES]).astype(BF16)
        o_ref[:, 2 * D:] = raw_ref[:, 2 * D:].astype(BF16)

    return pl.pallas_call(
        body, name="qknorm_fwd", grid=(S // tm,), in_specs=[_rows(tm, D3), _resident((1, D)), _resident((1, D))],
        out_specs=_rows(tm, D3), out_shape=SDS((S, D3), BF16), compiler_params=_params("parallel"),
    )(raw, gq, gk)


def _qknorm_bwd(dq, dk, dv, raw, gq, gk):
    S, D3 = raw.shape
    D = D3 // 3
    nb = D // LANES
    tm = _tile(S, TOKEN_TILE)

    def body(dq_ref, dk_ref, dv_ref, raw_ref, gq_ref, gk_ref, o_ref, sums_ref):
        _zero_at_first_step(sums_ref)
        bd = _head_blocks()
        for row, (off, g_ref, d_ref) in enumerate(((0, gq_ref, dq_ref), (D, gk_ref, dk_ref))):
            for c in range(nb):
                lanes = slice(c * LANES, (c + 1) * LANES)
                cols = slice(off + c * LANES, off + (c + 1) * LANES)
                xs = raw_ref[:, cols]
                r = lax.rsqrt(_head_mean(xs * xs, bd) + EPS)
                n = xs * r
                dhat = d_ref[c]
                sums_ref[row:row + 1, lanes] += jnp.sum(dhat * n, axis=0, keepdims=True)
                dn = dhat * g_ref[:, lanes]
                o_ref[:, cols] = (r * (dn - n * _head_mean(dn * n, bd))).astype(BF16)
        for c in range(nb):
            o_ref[:, 2 * D + c * LANES:2 * D + (c + 1) * LANES] = dv_ref[c].astype(BF16)

    tiles = pl.BlockSpec((nb, tm, LANES), lambda i: (0, i, 0))
    return pl.pallas_call(
        body, name="qknorm_bwd", grid=(S // tm,),
        in_specs=[tiles, tiles, tiles, _rows(tm, D3), _resident((1, D)), _resident((1, D))],
        out_specs=[_rows(tm, D3), _acc_spec(8, D)], out_shape=[SDS((S, D3), BF16), SDS((8, D), F32)],
        compiler_params=_params("arbitrary"),
    )(dq, dk, dv, raw, gq, gk)


def _softplus2(z):
    return jnp.maximum(z, jnp.log2(1.0 + jnp.exp2(jnp.minimum(z, 30.0))))


def _attn_tiles(S):
    tq = _tile(S, ATTN_Q_TILE)
    tk = _tile(tq, ATTN_K_TILE)
    return tq, tk


def _attn_consts(tq, tk):
    lane = lax.broadcasted_iota(jnp.int32, (1, LANES), 1)
    first = lane < HEAD_DIM
    r = lax.broadcasted_iota(jnp.int32, (tq, tk), 0)
    c = lax.broadcasted_iota(jnp.int32, (tq, tk), 1)
    kr = lax.broadcasted_iota(jnp.int32, (tk, tk), 0)
    kc = lax.broadcasted_iota(jnp.int32, (tk, tk), 1)
    tri = c < r
    later = (kr > kc).astype(BF16)
    upto = (kr <= kc).astype(BF16)
    return first, tri, later, upto


def _row_sums(t):
    return jnp.broadcast_to(jnp.sum(t, axis=1, keepdims=True), (t.shape[0], LANES))


def _over_keys(per_row, tk):
    return per_row[:, :tk] if tk <= LANES else jnp.concatenate([per_row] * (tk // LANES), axis=1)


def _from(x, lo, axis=0):
    return x if lo == 0 else lax.slice_in_dim(x, lo, x.shape[axis], axis=axis)


def _add_from(acc, lo, part):
    return acc + (part if lo == 0 else jnp.concatenate([jnp.zeros((lo,) + part.shape[1:], part.dtype), part], axis=0))


def _attn_specs(S, tq, nb):
    q_spec = pl.BlockSpec((tq, LANES), lambda hp, i: (i, hp))
    k_spec = pl.BlockSpec((S, LANES), lambda hp, i: (0, nb + hp), pipeline_mode=pl.Buffered(1))
    v_spec = pl.BlockSpec((S, LANES), lambda hp, i: (0, 2 * nb + hp), pipeline_mode=pl.Buffered(1))
    return q_spec, k_spec, v_spec


def _attn_fwd(qkv):
    S, D3 = qkv.shape
    D = D3 // 3
    nb = D // LANES
    tq, tk = _attn_tiles(S)
    nsub = tq // tk

    def body(q_ref, k_ref, v_ref, o_ref, tot_ref, start_ref):
        hp = pl.program_id(0)
        i = pl.program_id(1)
        first, tri, later, _ = _attn_consts(tq, tk)
        q = q_ref[...]
        qs = (jnp.where(first, q, jnp.zeros_like(q)), jnp.where(first, jnp.zeros_like(q), q))

        def tiles(t0, n, carry, diagonal):
            out = []
            for h in range(2):
                o, c = carry[h]
                for a in reversed(range(n)):
                    lo = a * tk if diagonal else 0
                    mask = tri[:tq - lo] if diagonal else None
                    rows = pl.ds(pl.multiple_of((t0 + a) * tk, tk), tk)
                    kb = k_ref[rows, :]
                    z = _dot_nt(_from(qs[h], lo), kb)
                    sp = _softplus2(z)
                    logsig = z - sp
                    if diagonal:
                        sp = jnp.where(mask, sp, 0.0)
                    av = jnp.exp2(logsig - _dot(sp.astype(BF16), later) - _over_keys(_from(c, lo), tk))
                    if diagonal:
                        av = jnp.where(mask, av, 0.0)
                    o = _add_from(o, lo, _dot(av.astype(BF16), v_ref[rows, :]))
                    c = _add_from(c, lo, _row_sums(sp))
                out.append((o, c))
            return tuple(out)

        def live(carry):
            return jnp.minimum(jnp.min(carry[0][1]), jnp.min(carry[1][1])) < ATTN_DEAD_BITS

        carry = tuple((jnp.zeros((tq, LANES), F32), jnp.zeros((tq, LANES), F32)) for _ in range(2))
        carry = tiles(i * nsub, nsub, carry, True)
        band_floor = jnp.maximum((i - 1) * nsub, 0)
        t, carry = lax.while_loop(lambda st: (st[0] >= band_floor) & (st[0] >= 0) & live(st[1]),
                                  lambda st: (st[0] - 1, tiles(st[0], 1, st[1], False)), (i * nsub - 1, carry))
        t, carry = lax.while_loop(lambda st: (st[0] >= nsub - 1) & live(st[1]),
                                  lambda st: (st[0] - nsub, tiles(st[0] - (nsub - 1), nsub, st[1], False)), (t, carry))
        (o_a, c_a), (o_b, c_b) = carry
        o_ref[...] = jnp.where(first, o_a, o_b).astype(BF16)
        tot_ref[...] = jnp.where(first, c_a, c_b)
        start_ref[hp, i] = t + 1

    q_spec, k_spec, v_spec = _attn_specs(S, tq, nb)
    return pl.pallas_call(
        body, name="attn_fwd", grid=(nb, S // tq), in_specs=[q_spec, k_spec, v_spec],
        out_specs=[q_spec, q_spec, pl.BlockSpec(memory_space=pltpu.SMEM)],
        out_shape=[SDS((S, D), BF16), SDS((S, D), F32), SDS((nb, S // tq), jnp.int32)],
        compiler_params=_params("arbitrary", "arbitrary"),
    )(qkv, qkv, qkv)


def _attn_bwd(qkv, do, tot, start):
    S, D3 = qkv.shape
    D = D3 // 3
    nb = D // LANES
    tq, tk = _attn_tiles(S)
    nsub = tq // tk
    nkb = S // tk

    def body(start_ref, q_ref, k_ref, v_ref, do_ref, tot_ref, dq_ref, dk_hbm, dv_hbm, dkt_acc, dvt_acc, stage, sem):
        hp = pl.program_id(0)
        i = pl.program_id(1)

        @pl.when(i == 0)
        def _():
            dkt_acc[...] = jnp.zeros_like(dkt_acc)
            dvt_acc[...] = jnp.zeros_like(dvt_acc)

        first, tri, later, upto = _attn_consts(tq, tk)
        q = q_ref[...]
        do = do_ref[...]
        zero = jnp.zeros_like(q)
        qs = (jnp.where(first, q, zero), jnp.where(first, zero, q))
        dos = (jnp.where(first, do, zero), jnp.where(first, zero, do))
        qt = q.astype(F32).T.astype(BF16)
        dot_ = do.astype(F32).T.astype(BF16)
        qts = (qt[:HEAD_DIM], qt[HEAD_DIM:])
        dots = (dot_[:HEAD_DIM], dot_[HEAD_DIM:])
        tots = tuple(jnp.broadcast_to(tot_ref[:, l:l + 1], (tq, LANES)) for l in (0, HEAD_DIM))

        def tiles(t0, n, carry, diagonal):
            carry = list(carry)
            for a in range(n):
                lo = a * tk if diagonal else 0
                mask = tri[:tq - lo] if diagonal else None
                j = t0 + a
                rows = pl.ds(pl.multiple_of(j * tk, tk), tk)
                kb = k_ref[rows, :]
                vb = v_ref[rows, :]
                dkts, dvts = [], []
                for h in range(2):
                    dq, cum, pre = carry[h]
                    z = _dot_nt(_from(qs[h], lo), kb)
                    sp = _softplus2(z)
                    logsig = z - sp
                    if diagonal:
                        sp = jnp.where(mask, sp, 0.0)
                    cum = _add_from(cum, lo, _row_sums(sp))
                    av = jnp.exp2(logsig - _dot(sp.astype(BF16), later) - _over_keys(_from(tots[h], lo) - _from(cum, lo), tk))
                    if diagonal:
                        av = jnp.where(mask, av, 0.0)
                    g = _dot_nt(_from(dos[h], lo), vb) * av
                    dz = g - jnp.exp2(logsig) * (_over_keys(_from(pre, lo), tk) + _dot(g.astype(BF16), upto))
                    if diagonal:
                        dz = jnp.where(mask, dz, 0.0)
                    dz = dz.astype(BF16)
                    dkts.append(_dot(_from(qts[h], lo, 1), dz))
                    dvts.append(_dot(_from(dots[h], lo, 1), av.astype(BF16)))
                    carry[h] = (_add_from(dq, lo, _dot(dz, kb)), cum, _add_from(pre, lo, _row_sums(g)))
                dkt_acc[j] += jnp.concatenate(dkts, axis=0)
                dvt_acc[j] += jnp.concatenate(dvts, axis=0)
            return tuple(carry)

        carry = tuple((jnp.zeros((tq, LANES), F32),) * 3 for _ in range(2))
        t0 = start_ref[hp, i]
        odd = lax.rem(i * nsub - t0, nsub)
        carry = lax.fori_loop(0, odd, lambda s, cr: tiles(t0 + s, 1, cr, False), carry)
        carry = lax.fori_loop(0, (i * nsub - t0) // nsub, lambda b, cr: tiles(t0 + odd + b * nsub, nsub, cr, False), carry)
        carry = tiles(i * nsub, nsub, carry, True)
        dq_ref[...] = jnp.where(first, carry[0][0], carry[1][0]) * LN2

        @pl.when(i == pl.num_programs(1) - 1)
        def _():
            def flush(j, _):
                rows = pl.ds(pl.multiple_of(j * tk, tk), tk)
                stage[0] = dkt_acc[j].T * LN2
                stage[1] = dvt_acc[j].T
                ck = pltpu.make_async_copy(stage.at[0], dk_hbm.at[hp, rows], sem.at[0])
                cv = pltpu.make_async_copy(stage.at[1], dv_hbm.at[hp, rows], sem.at[1])
                ck.start()
                cv.start()
                ck.wait()
                cv.wait()
                return 0
            lax.fori_loop(0, nkb, flush, 0)

    q_spec, k_spec, v_spec = _attn_specs(S, tq, nb)
    slab = SDS((nb, S, LANES), F32)
    return pl.pallas_call(
        body, name="attn_bwd", grid=(nb, S // tq),
        in_specs=[pl.BlockSpec(memory_space=pltpu.SMEM), q_spec, k_spec, v_spec, q_spec, q_spec],
        out_specs=[pl.BlockSpec((None, tq, LANES), lambda hp, i: (hp, i, 0)), ANY, ANY], out_shape=[slab, slab, slab],
        scratch_shapes=[pltpu.VMEM((nkb, LANES, tk), F32), pltpu.VMEM((nkb, LANES, tk), F32), pltpu.VMEM((2, tk, LANES), F32),
                        pltpu.SemaphoreType.DMA((2,))],
        compiler_params=_params("arbitrary", "arbitrary"),
    )(start, qkv, qkv, qkv, do, tot)


def _trail_sum(ext, bufs, cols, levels, n):
    def src(lo, size):
        return ext[pl.ds(lo, size), cols]
    for l in range(levels):
        lo = 8 * (l + 1)
        dst = bufs[l % 2]
        dst[lo:, :] = src(lo, n - lo) + src(lo - (1 << l), n - lo)
        def src(lo_, size, d=dst):
            return d[pl.ds(lo_, size), :]
    return src(HALO, n - HALO)


def _lead_sum(ext, bufs, cols, levels, n):
    def src(lo, size):
        return ext[pl.ds(lo, size), cols]
    for l in range(levels):
        hi = n - 8 * (l + 1)
        dst = bufs[l % 2]
        dst[0:hi, :] = src(0, hi) + src(1 << l, hi)
        def src(lo_, size, d=dst):
            return d[pl.ds(lo_, size), :]
    return src(0, n - HALO)


def _pool_diffs(x_ref, xh_ref, vec_ref, ext, bufs, tm, D):
    i = pl.program_id(0)
    gs, shift = vec_ref[0:1, :], vec_ref[1:2, :]
    n, r = _rms(x_ref[...])
    nh, _ = _rms(xh_ref[...])
    ext[0:HALO, :] = jnp.where(i > 0, nh * gs + shift, 0.0)
    ext[HALO:, :] = n * gs + shift
    t = i * tm + lax.broadcasted_iota(jnp.int32, (tm, 1), 0)
    dg = D // len(POOL_LEVELS)
    out = []
    for g, lv in enumerate(POOL_LEVELS):
        cols = slice(g * dg, (g + 1) * dg)
        inv = 1.0 / jnp.minimum(t + 1, 1 << lv).astype(F32)
        out.append((_trail_sum(ext, bufs, cols, lv, HALO + tm) * inv - ext[HALO:, cols], inv))
    return out, n, r


def _pool_fwd(x, vec, pw):
    S, D = x.shape
    ng, dg, _ = pw.shape
    tm = _tile(S, TOKEN_TILE)

    def body(x_ref, xh_ref, vec_ref, pw_ref, xo_ref, ext, buf_a, buf_b):
        diffs, _, _ = _pool_diffs(x_ref, xh_ref, vec_ref, ext, (buf_a, buf_b), tm, D)
        for g, (d, _) in enumerate(diffs):
            cols = slice(g * dg, (g + 1) * dg)
            y = (_dot(d.astype(BF16), pw_ref[g]) + vec_ref[4:5, cols]) * vec_ref[3:4, cols]
            xo_ref[:, cols] = x_ref[:, cols] + vec_ref[2:3, cols] * y

    return pl.pallas_call(
        body, name="pool_fwd", grid=(S // tm,),
        in_specs=[_rows(tm, D), _prev_halo(tm, D), _resident((8, D)), _resident(pw.shape)],
        out_specs=_rows(tm, D), out_shape=SDS((S, D), F32),
        scratch_shapes=[pltpu.VMEM((HALO + tm, D), F32), pltpu.VMEM((HALO + tm, dg), F32), pltpu.VMEM((HALO + tm, dg), F32)],
        compiler_params=_params("parallel"),
    )(x, x, vec, pw)


def _pool_bwd(x, dxo, vec, pw):
    S, D = x.shape
    ng, dg, _ = pw.shape
    tm = _tile(S, TOKEN_TILE)

    def body(x_ref, xh_ref, dxo_ref, dxn_ref, vec_ref, pw_ref, dx_ref, gpw_ref, sums_ref, ext, exte, buf_a, buf_b):
        _zero_at_first_step(gpw_ref, sums_ref)
        i = pl.program_id(0)
        bufs = (buf_a, buf_b)
        diffs, n, r = _pool_diffs(x_ref, xh_ref, vec_ref, ext, bufs, tm, D)
        gate, scale = vec_ref[2:3, :], vec_ref[3:4, :]
        dxo = dxo_ref[...]
        dyp_next = jnp.where(i < pl.num_programs(0) - 1, dxn_ref[...], 0.0) * gate * scale
        t_next = (i + 1) * tm + lax.broadcasted_iota(jnp.int32, (HALO, 1), 0)
        for g, (d, inv) in enumerate(diffs):
            cols = slice(g * dg, (g + 1) * dg)
            w = pw_ref[g]
            db = d.astype(BF16)
            ypre = _dot(db, w) + vec_ref[4:5, cols]
            dy = gate[:, cols] * dxo[:, cols]
            sums_ref[2:3, cols] += jnp.sum(dxo[:, cols] * ypre * scale[:, cols], axis=0, keepdims=True)
            sums_ref[3:4, cols] += jnp.sum(dy * ypre, axis=0, keepdims=True)
            dyp = dy * scale[:, cols]
            sums_ref[4:5, cols] += jnp.sum(dyp, axis=0, keepdims=True)
            dypb = dyp.astype(BF16)
            gpw_ref[g] += _dot_tn(db, dypb)
            dd = _dot_nt(dypb, w)
            dd_next = _dot_nt(dyp_next[:, cols].astype(BF16), w)
            inv_next = 1.0 / jnp.minimum(t_next + 1, 1 << POOL_LEVELS[g]).astype(F32)
            exte[0:tm, cols] = dd * inv
            exte[tm:, cols] = dd_next * inv_next
            ext[HALO:, cols] = _lead_sum(exte, bufs, cols, POOL_LEVELS[g], tm + HALO) - dd
        dx_ref[...] = _modulate_bwd(ext[HALO:, :], n, r, vec_ref[0:1, :], dxo, sums_ref)

    return pl.pallas_call(
        body, name="pool_bwd", grid=(S // tm,),
        in_specs=[_rows(tm, D), _prev_halo(tm, D), _rows(tm, D), _next_halo(tm, D, S), _resident((8, D)), _resident(pw.shape)],
        out_specs=[_rows(tm, D), pl.BlockSpec(pw.shape, lambda i: (0, 0, 0)), _acc_spec(8, D)],
        out_shape=[SDS((S, D), F32), SDS(pw.shape, F32), SDS((8, D), F32)],
        scratch_shapes=[pltpu.VMEM((HALO + tm, D), F32), pltpu.VMEM((tm + HALO, D), F32),
                        pltpu.VMEM((HALO + tm, dg), F32), pltpu.VMEM((HALO + tm, dg), F32)],
        compiler_params=_params("arbitrary"),
    )(x, x, dxo, dxo, vec, pw)


def _loss_head(y, target):
    S, D = y.shape
    tm = _tile(S, TOKEN_TILE)

    def body(y_ref, t_ref, dy_ref, sums_ref):
        _zero_at_first_step(sums_ref)
        err = y_ref[...] - t_ref[...]
        _add_rowsum(sums_ref, 0, err * err)
        dy_ref[...] = err * (1.0 / D)

    return pl.pallas_call(
        body, name="loss_head", grid=(S // tm,), in_specs=[_rows(tm, D), _rows(tm, D)],
        out_specs=[_rows(tm, D), _acc_spec(8, D)], out_shape=[SDS((S, D), F32), SDS((8, D), F32)],
        compiler_params=_params("arbitrary"),
    )(y, target)


def _cond_pre(c_cols, cond_w):
    def body(c_ref, w_ref, o_ref):
        o_ref[...] = _hdot(c_ref[...], w_ref[...])
    return pl.pallas_call(body, name="cond_pre", out_shape=SDS((c_cols.shape[0], cond_w.shape[1]), F32))(c_cols, cond_w)


def _cond_e(parts, cond_b):
    def body(p_ref, b_ref, pre_ref, e_ref):
        pre = p_ref[0] + p_ref[2] + p_ref[4] + p_ref[6] + b_ref[...]
        pre_ref[...] = pre
        e_ref[...] = pre * jax.nn.sigmoid(pre)
    shape = SDS(parts.shape[1:], F32)
    return pl.pallas_call(body, name="cond_e", out_shape=[shape, shape])(parts, cond_b)


def _mod_cols(e, ada_w, ada_b_cols):
    L, D, nc = ada_w.shape
    B = e.shape[0]

    def body(e_ref, w_ref, b_ref, o_ref):
        o_ref[...] = _hdot(e_ref[...], w_ref[...]) + b_ref[...]

    return pl.pallas_call(
        body, name="mod_cols", grid=(L,),
        in_specs=[pl.BlockSpec((B, D), lambda l: (0, 0)), pl.BlockSpec((None, D, nc), lambda l: (l, 0, 0)),
                  pl.BlockSpec((None, 1, nc), lambda l: (l, 0, 0))],
        out_specs=pl.BlockSpec((None, B, nc), lambda l: (l, 0, 0)), out_shape=SDS((L, B, nc), F32),
        compiler_params=_params("parallel"),
    )(e, ada_w, ada_b_cols)


def _mod_bwd(e, dmod_cols, ada_w):
    L, D, nc = ada_w.shape
    B = e.shape[0]

    def body(e_ref, d_ref, w_ref, gw_ref, de_ref):
        _zero_at_first_step(de_ref)
        gw_ref[...] = _hdot(e_ref[...], d_ref[...], (((0,), (0,)), ((), ())))
        de_ref[...] += _hdot(d_ref[...], w_ref[...], (((1,), (1,)), ((), ())))

    return pl.pallas_call(
        body, name="mod_bwd", grid=(L,),
        in_specs=[pl.BlockSpec((B, D), lambda l: (0, 0)), pl.BlockSpec((None, B, nc), lambda l: (l, 0, 0)),
                  pl.BlockSpec((None, D, nc), lambda l: (l, 0, 0))],
        out_specs=[pl.BlockSpec((None, D, nc), lambda l: (l, 0, 0)), pl.BlockSpec((B, D), lambda l: (0, 0))],
        out_shape=[SDS((L, D, nc), F32), SDS((B, D), F32)], compiler_params=_params("arbitrary"),
    )(e, dmod_cols, ada_w)


def _cond_bwd(de_parts, pre, c_cols):
    def body(p_ref, pre_ref, c_ref, gw_ref, gb_ref):
        pre = pre_ref[...]
        dpre = (p_ref[0] + p_ref[2] + p_ref[4] + p_ref[6]) * _silu_grad(pre, jax.nn.sigmoid(pre))
        gb_ref[...] = jnp.sum(dpre, axis=0, keepdims=True)
        gw_ref[...] = _hdot(c_ref[...], dpre, (((0,), (0,)), ((), ())))
    D = pre.shape[1]
    return pl.pallas_call(body, name="cond_bwd", out_shape=[SDS((c_cols.shape[1], D), F32), SDS((1, D), F32)])(de_parts, pre, c_cols)


def _as_rows(a):
    return a.reshape(-1, a.shape[-1])


def _row_tile(rows, width, n_arrays):
    t = max(8, (ADAM_TILE_BYTES // (4 * width)) // 8 * 8)
    while rows % t:
        t -= 8
        if t <= 0:
            return rows
    return t


def _adamw(w, g, m, v):
    shape = w.shape
    w, g, m, v = (_as_rows(a) for a in (w, g, m, v))
    R, C = w.shape
    tr = _row_tile(R, C, 7)

    def body(w_ref, g_ref, m_ref, v_ref, d_ref, nm_ref, nv_ref):
        g = g_ref[...]
        m = ADAM_B1 * m_ref[...] + (1.0 - ADAM_B1) * g
        v = ADAM_B2 * v_ref[...] + (1.0 - ADAM_B2) * (g * g)
        m_hat = m / (1.0 - ADAM_B1 ** ADAM_STEP)
        v_hat = v / (1.0 - ADAM_B2 ** ADAM_STEP)
        d_ref[...] = -ADAM_LR * (m_hat / (jnp.sqrt(v_hat) + ADAM_EPS) + ADAM_WD * w_ref[...])
        nm_ref[...] = m
        nv_ref[...] = v

    spec = _rows(tr, C)
    outs = pl.pallas_call(
        body, name="adamw", grid=(R // tr,), in_specs=[spec] * 4, out_specs=[spec] * 3,
        out_shape=[SDS((R, C), F32)] * 3, compiler_params=_params("parallel"),
    )(w, g, m, v)
    return tuple(o.reshape(shape) for o in outs)


def _sum_lead(a, out_dtype=F32):
    n = a.shape[0]
    shape = a.shape[1:]
    a = a.reshape(n, -1, a.shape[-1])
    _, R, C = a.shape
    tr = _row_tile(R, C, n + 1)

    def body(a_ref, o_ref):
        acc = a_ref[0].astype(F32)
        for k in range(1, n):
            acc = acc + a_ref[k].astype(F32)
        o_ref[...] = acc.astype(out_dtype)

    out = pl.pallas_call(
        body, name="sum_lead", grid=(R // tr,), in_specs=[pl.BlockSpec((n, tr, C), lambda i: (0, i, 0))],
        out_specs=_rows(tr, C), out_shape=SDS((R, C), out_dtype), compiler_params=_params("parallel"),
    )(a)
    return out.reshape(shape)


def _place():
    return lax.axis_index("x"), lax.axis_index("y"), lax.axis_index("c")


def _allgather8(a):
    def body(a_ref, o_ref, send, recv, local):
        mx, my, mc = _place()
        me = 4 * mx + 2 * my + mc
        mine = pltpu.make_async_copy(a_ref, o_ref.at[me], local)
        mine.start()
        copies = []
        for k in range(1, N_DEV):
            peer = (1 - mx if k & 4 else mx, 1 - my if k & 2 else my, 1 - mc if k & 1 else mc)
            cp = pltpu.make_async_remote_copy(a_ref, o_ref.at[me], send.at[k - 1], recv.at[k - 1], device_id=peer, device_id_type=MESH)
            cp.start()
            copies.append(cp)
        for cp in copies:
            cp.wait()
        mine.wait()

    return pl.pallas_call(
        body, name="allgather8", in_specs=[ANY], out_specs=ANY, out_shape=SDS((N_DEV,) + a.shape, a.dtype),
        scratch_shapes=[pltpu.SemaphoreType.DMA((N_DEV - 1,)), pltpu.SemaphoreType.DMA((N_DEV - 1,)), pltpu.SemaphoreType.DMA],
    )(a)


def _grad_scatter(arrs):
    n = len(arrs)
    out_shape = [SDS((N_CHIPS, a.shape[0]) + a.shape[2:], a.dtype) for a in arrs]

    def body(*refs):
        ins, outs, (send, recv, local) = refs[:n], refs[n:2 * n], refs[2 * n:]
        mx, my, mc = _place()
        chip = 2 * mx + my
        copies = []
        for a in range(n):
            cp = pltpu.make_async_copy(ins[a].at[:, chip], outs[a].at[chip], local.at[a])
            cp.start()
            copies.append(cp)
        for k in range(1, N_CHIPS):
            px, py = (1 - mx if k & 2 else mx), (1 - my if k & 1 else my)
            for a in range(n):
                s = (k - 1) * n + a
                cp = pltpu.make_async_remote_copy(ins[a].at[:, 2 * px + py], outs[a].at[chip], send.at[s], recv.at[s],
                                                  device_id=(px, py, mc), device_id_type=MESH)
                cp.start()
                copies.append(cp)
        for cp in copies:
            cp.wait()

    return pl.pallas_call(
        body, name="grad_scatter", in_specs=[ANY] * n, out_specs=[ANY] * n, out_shape=out_shape,
        scratch_shapes=[pltpu.SemaphoreType.DMA((3 * n,)), pltpu.SemaphoreType.DMA((3 * n,)), pltpu.SemaphoreType.DMA((n,))],
    )(*arrs)


def _weight_gather(arrs):
    n = len(arrs)

    def body(*refs):
        ins, outs, (ici_send, ici_recv, d2d_send, d2d_recv, local) = refs[:n], refs[n:2 * n], refs[2 * n:]
        mx, my, mc = _place()
        chip = 2 * mx + my
        own = []
        for a in range(n):
            cp = pltpu.make_async_copy(ins[a], outs[a].at[:, chip], local.at[a])
            cp.start()
            own.append(cp)
        peers = [(1 - mx if k & 2 else mx, 1 - my if k & 1 else my) for k in range(1, N_CHIPS)]
        fetched = []
        for k, (px, py) in enumerate(peers):
            for a in range(n):
                s = k * n + a
                cp = pltpu.make_async_remote_copy(ins[a].at[:, mc], outs[a].at[:, chip, mc], ici_send.at[s], ici_recv.at[s],
                                                  device_id=(px, py, mc), device_id_type=MESH)
                cp.start()
                fetched.append(cp)
        passed = []
        for k, (px, py) in enumerate(peers):
            for a in range(n):
                s = k * n + a
                fetched[s].wait_recv()
                half = outs[a].at[:, 2 * px + py, mc]
                cp = pltpu.make_async_remote_copy(half, half, d2d_send.at[s], d2d_recv.at[s],
                                                  device_id=(mx, my, 1 - mc), device_id_type=MESH)
                cp.start()
                passed.append(cp)
        for cp in fetched:
            cp.wait_send()
        for cp in passed + own:
            cp.wait()

    sems = pltpu.SemaphoreType.DMA((3 * n,))
    return pl.pallas_call(
        body, name="weight_gather", in_specs=[ANY] * n, out_specs=[ANY] * n,
        out_shape=[SDS((a.shape[0], N_CHIPS) + a.shape[1:], a.dtype) for a in arrs],
        scratch_shapes=[sems, sems, sems, sems, pltpu.SemaphoreType.DMA((n,))],
    )(*arrs)


def _sibling_send(arrs, halves, name):
    n = len(arrs)
    out_shape = [SDS(a.shape[:2] + a.shape[3:] if halves else a.shape, a.dtype) for a in arrs]

    def body(*refs):
        ins, outs, (send, recv) = refs[:n], refs[n:2 * n], refs[2 * n:]
        mx, my, mc = _place()
        copies = []
        for a in range(n):
            cp = pltpu.make_async_remote_copy(ins[a].at[:, :, 1 - mc] if halves else ins[a], outs[a], send.at[a], recv.at[a],
                                              device_id=(mx, my, 1 - mc), device_id_type=MESH)
            cp.start()
            copies.append(cp)
        for cp in copies:
            cp.wait()

    return pl.pallas_call(
        body, name=name, in_specs=[ANY] * n, out_specs=[ANY] * n, out_shape=out_shape,
        scratch_shapes=[pltpu.SemaphoreType.DMA((n,)), pltpu.SemaphoreType.DMA((n,))],
    )(*arrs)


def _add_cast(a, b, dtype):
    shape = a.shape
    a, b = _as_rows(a), _as_rows(b)
    R, C = a.shape
    tr = _row_tile(R, C, 3)

    def body(a_ref, b_ref, o_ref):
        o_ref[...] = (a_ref[...] + b_ref[...]).astype(dtype)

    out = pl.pallas_call(body, name="pair_sum", grid=(R // tr,), in_specs=[_rows(tr, C)] * 2, out_specs=_rows(tr, C),
                         out_shape=SDS((R, C), dtype), compiler_params=_params("parallel"))(a, b)
    return out.reshape(shape)


def _pack(arrs):
    flat = jnp.concatenate([a.reshape(-1).astype(F32) for a in arrs])
    pad = (-flat.shape[0]) % (16 * LANES)
    return jnp.pad(flat, (0, pad)).reshape(-1, LANES)


def _unpack(buf, shapes):
    flat = buf.reshape(buf.shape[:-2] + (-1,))
    out, off = [], 0
    for s in shapes:
        size = 1
        for d in s:
            size *= d
        out.append(flat[..., off:off + size].reshape(flat.shape[:-1] + tuple(s)))
        off += size
    return out


def _unshard(stacked, axis):
    moved = jnp.moveaxis(stacked, 0, axis)
    return moved.reshape(moved.shape[:axis] + (N_CHIPS * moved.shape[axis + 1],) + moved.shape[axis + 2:])


def _my_shard(full, axis, chip):
    size = full.shape[axis] // N_CHIPS
    return lax.dynamic_slice_in_dim(full, chip * size, size, axis)


def kernel(x, c, cond_w, cond_b, ada_w, ada_b, norm_g, ffn_w1, ffn_w3, ffn_w2, a_w_in, a_b_in, a_dw, a_dw_b, a_ln_g, a_ln_b, a_w_out, a_b_out, b_w_qkv, b_q_g, b_k_g, b_w_o, p_w, p_b, p_scale, loss_target, m_cond_w, m_cond_b, m_ada_w, m_ada_b, m_norm_g, m_ffn_w1, m_ffn_w3, m_ffn_w2, m_a_w_in, m_a_b_in, m_a_dw, m_a_dw_b, m_a_ln_g, m_a_ln_b, m_a_w_out, m_a_b_out, m_b_w_qkv, m_b_q_g, m_b_k_g, m_b_w_o, m_p_w, m_p_b, m_p_scale, v_cond_w, v_cond_b, v_ada_w, v_ada_b, v_norm_g, v_ffn_w1, v_ffn_w3, v_ffn_w2, v_a_w_in, v_a_b_in, v_a_dw, v_a_dw_b, v_a_ln_g, v_a_ln_b, v_a_w_out, v_a_b_out, v_b_w_qkv, v_b_q_g, v_b_k_g, v_b_w_o, v_p_w, v_p_b, v_p_scale):
    w_in = dict(cond_w=cond_w, cond_b=cond_b, ada_w=ada_w, ada_b=ada_b, norm_g=norm_g, ffn_w1=ffn_w1, ffn_w3=ffn_w3, ffn_w2=ffn_w2,
                a_w_in=a_w_in, a_b_in=a_b_in, a_dw=a_dw, a_dw_b=a_dw_b, a_ln_g=a_ln_g, a_ln_b=a_ln_b, a_w_out=a_w_out, a_b_out=a_b_out,
                b_w_qkv=b_w_qkv, b_q_g=b_q_g, b_k_g=b_k_g, b_w_o=b_w_o, p_w=p_w, p_b=p_b, p_scale=p_scale)
    m_in = dict(zip(WEIGHTS, (m_cond_w, m_cond_b, m_ada_w, m_ada_b, m_norm_g, m_ffn_w1, m_ffn_w3, m_ffn_w2, m_a_w_in, m_a_b_in, m_a_dw,
                              m_a_dw_b, m_a_ln_g, m_a_ln_b, m_a_w_out, m_a_b_out, m_b_w_qkv, m_b_q_g, m_b_k_g, m_b_w_o, m_p_w, m_p_b, m_p_scale)))
    v_in = dict(zip(WEIGHTS, (v_cond_w, v_cond_b, v_ada_w, v_ada_b, v_norm_g, v_ffn_w1, v_ffn_w3, v_ffn_w2, v_a_w_in, v_a_b_in, v_a_dw,
                              v_a_dw_b, v_a_ln_g, v_a_ln_b, v_a_w_out, v_a_b_out, v_b_w_qkv, v_b_q_g, v_b_k_g, v_b_w_o, v_p_w, v_p_b, v_p_scale)))
    x = x[0]
    target = loss_target[0]
    S, D = x.shape
    L = ada_w.shape[0]
    assert b_q_g.shape[-1] == HEAD_DIM and D % LANES == 0 and S % HALO == 0
    mx, my, mc = _place()
    chip = 2 * mx + my
    me = 2 * chip + mc

    big = [ffn_w1.astype(BF16), ffn_w3.astype(BF16), ffn_w2.astype(BF16), a_w_in.astype(BF16), a_w_out.astype(BF16),
           b_w_qkv.astype(BF16), b_w_o.astype(BF16)]
    sharded = [k for k, ax in SMALL.items() if ax is not None]
    shards = big + [_pack([w_in[k] for k in sharded])]
    by_half = [a.reshape((-1, 2, a.shape[-2] // 2, a.shape[-1])) for a in shards]
    gathered = [g.reshape(a.shape[:-2] + (N_CHIPS,) + a.shape[-2:]) for g, a in zip(_weight_gather(by_half), shards)]
    g_w1, g_w3, g_w2, g_ain, g_aout, g_qkv, g_wo, g_small = gathered
    full = {k: _unshard(a, SMALL[k]) for k, a in zip(sharded, _unpack(g_small, [w_in[k].shape for k in sharded]))}
    n_conv, n_pool = a_w_in.shape[0], p_w.shape[0]
    conv_dw = [jnp.pad(full['a_dw'][ia], ((0, 1), (0, 0))) for ia in range(n_conv)]
    pool_w = [full['p_w'][ic].astype(BF16) for ic in range(n_pool)]
    qk_scale = LOG2E * HEAD_DIM ** -0.5
    gq = jnp.tile(b_q_g[0], D // HEAD_DIM).reshape(1, D) * qk_scale
    gk = jnp.tile(b_k_g[0], D // HEAD_DIM).reshape(1, D)

    c_all = _allgather8(c)[:, 0, :]
    c_cols = _my_shard(c_all, 1, chip)
    pre, e = _cond_e(_allgather8(_cond_pre(c_cols, cond_w)), cond_b.reshape(1, D))
    nc = ada_w.shape[2]
    mod_c = _mod_cols(e, ada_w, _my_shard(ada_b, 1, chip).reshape(L, 1, nc))
    mod_all = _allgather8(mod_c.reshape(L * N_DEV, nc)).reshape(N_CHIPS, 2, L, N_DEV, nc)[:, 0]
    mod = jnp.moveaxis(lax.dynamic_index_in_dim(mod_all, me, axis=2, keepdims=False), 0, 1).reshape(L, 3, 3, D)
    shift, scale, gate = mod[:, :, 0], mod[:, :, 1], 1.0 + mod[:, :, 2]
    gains = full['norm_g']

    def mod_vec(i, k, gate_factor=1.0):
        return _vec(gains[i, k] * (1.0 + scale[i, k]), shift[i, k], gate_factor * gate[i, k])

    saved = []
    ia = ib = ic = 0
    for i in range(L):
        for k, half in ((0, 0), (1, None), (2, 1)):
            if half is not None:
                vec = mod_vec(i, k, 0.5)
                w1, w3, w2 = g_w1[i, half], g_w3[i, half], g_w2[i, half]
                xo, a1, a3, y = _ffn_fwd(x, vec, w1, w3, w2)
                saved.append(('ffn', i, k, half, x, vec, (a1, a3, y, w1, w3, w2)))
            elif i % 3 == 0:
                vec = mod_vec(i, k)
                w_a = g_ain[ia]
                w_o = g_aout[ia].reshape(D, D)
                p, u = _in_fwd(x, vec, w_a, full['a_b_in'][ia].reshape(1, 2 * D), True, "conv_in_fwd")
                cvec = _vec(gate[i, k], full['a_dw_b'][ia], full['a_ln_g'][ia], full['a_ln_b'][ia], full['a_b_out'][ia])
                xo, v, y = _conv_mid_fwd(u, x, cvec, conv_dw[ia], w_o)
                saved.append(('conv', i, k, ia, x, vec, (p, u, v, y, cvec, w_a, w_o)))
                ia += 1
            elif i % 3 == 1:
                vec = mod_vec(i, k)
                w_q = g_qkv[ib]
                w_o = g_wo[ib].reshape(D, D)
                raw, = _in_fwd(x, vec, w_q, jnp.zeros((1, 3 * D), F32), False, "attn_in_fwd")
                qkv = _qknorm_fwd(raw, gq, gk)
                o, tot, start = _attn_fwd(qkv)
                xo, y = _out_fwd(x, o, vec, w_o)
                saved.append(('attn', i, k, ib, x, vec, (raw, qkv, o, (tot, start), y, w_q, w_o)))
                ib += 1
            else:
                vec = _vec(gains[i, k] * (1.0 + scale[i, k]), shift[i, k], gate[i, k], full['p_scale'][ic], full['p_b'][ic].reshape(D))
                xo = _pool_fwd(x, vec, pool_w[ic])
                saved.append(('pool', i, k, ic, x, vec, ()))
                ic += 1
            x = xo

    dx, sq = _loss_head(x, target)
    loss = lax.psum(0.5 / D * jnp.sum(sq[0]), ("x", "y", "c"))

    zeros_like_full = lambda k: jnp.zeros(full[k].shape, F32)
    g_full = {k: zeros_like_full(k) for k in sharded}
    g_full['b_q_g'] = jnp.zeros_like(b_q_g)
    g_full['b_k_g'] = jnp.zeros_like(b_k_g)
    dmod = jnp.zeros((L, 3, 3, D), F32)
    fc = ffn_w1.shape[-1]
    n_attn = b_w_qkv.shape[0]
    big_grads = {'ffn_w1': lax.empty((2 * L, N_CHIPS, fc, D), F32), 'ffn_w3': lax.empty((2 * L, N_CHIPS, fc, D), F32),
                 'ffn_w2': lax.empty((2 * L, N_CHIPS, fc, D), F32), 'a_w_in': lax.empty((n_conv, N_CHIPS, D, 2 * D // N_CHIPS), F32),
                 'a_w_out': lax.empty((n_conv, 1, D, D), F32), 'b_w_qkv': lax.empty((n_attn, N_CHIPS, D, 3 * D // N_CHIPS), F32),
                 'b_w_o': lax.empty((n_attn, 1, D, D), F32)}
    transposed = ('ffn_w1', 'ffn_w3')

    def wgrad(name, slot, a, b, a_mode, b_mode, nch):
        big_grads[name] = _wgrad(a, b, a_mode, b_mode, nch, name + "_grad", big_grads[name], slot)

    def put(name, idx, val):
        g_full[name] = g_full[name].at[idx].set(val.reshape(g_full[name][idx].shape))

    for kind, i, k, idx, xin, vec, res in reversed(saved):
        if kind == 'ffn':
            a1, a3, y, w1, w3, w2 = res
            dx, h, dy, u, da1, da3, sums = _ffn_bwd(xin, dx, a1, a3, y, vec, w1, w3, w2)
            wgrad('ffn_w1', 2 * i + idx, da1, h, 'lead', 'full', N_CHIPS)
            wgrad('ffn_w3', 2 * i + idx, da3, h, 'lead', 'full', N_CHIPS)
            wgrad('ffn_w2', 2 * i + idx, u, dy, 'lead', 'full', N_CHIPS)
            dgate = 0.5 * sums[2]
        elif kind == 'conv':
            p, u, v, y, cvec, w_a, w_o = res
            dv, q, dout, csums = _conv_mid_bwd(dx, y, v, cvec, w_o)
            wgrad('a_w_out', idx, q, dout, 'full', 'full', 1)
            du, gdw = _conv_transpose(dv, u, conv_dw[idx])
            dp, psums = _glu_bwd(du, p)
            dx, h, sums = _in_bwd(xin, dx, dp, vec, w_a, "conv_in_bwd")
            wgrad('a_w_in', idx, h, dp, 'full', 'col', N_CHIPS)
            dgate = csums[0]
            put('a_b_out', idx, csums[1])
            put('a_ln_g', idx, csums[2])
            put('a_ln_b', idx, csums[3])
            put('a_dw_b', idx, csums[4])
            put('a_dw', idx, gdw[:-1])
            put('a_b_in', idx, psums[0])
        elif kind == 'attn':
            raw, qkv, o, tot, y, w_q, w_o = res
            do, dout, osums = _out_bwd(dx, y, vec, w_o)
            wgrad('b_w_o', idx, o, dout, 'full', 'full', 1)
            dq, dk, dvv = _attn_bwd(qkv, do, *tot)
            draw, qsums = _qknorm_bwd(dq, dk, dvv, raw, gq, gk)
            dx, h, sums = _in_bwd(xin, dx, draw, vec, w_q, "attn_in_bwd")
            wgrad('b_w_qkv', idx, h, draw, 'full', 'col', N_CHIPS)
            dgate = osums[2]
            put('b_q_g', idx, qk_scale * jnp.sum(qsums[0].reshape(-1, HEAD_DIM), axis=0))
            put('b_k_g', idx, jnp.sum(qsums[1].reshape(-1, HEAD_DIM), axis=0))
        else:
            dx, gpw, sums = _pool_bwd(xin, dx, vec, pool_w[idx])
            dgate = sums[2]
            put('p_w', idx, gpw)
            put('p_scale', idx, sums[3])
            put('p_b', idx, sums[4])
        put('norm_g', (i, k), sums[0] * (1.0 + scale[i, k]))
        dmod = dmod.at[i, k].set(jnp.stack([sums[1], sums[0] * gains[i, k], dgate]))
    grad_x = dx[None]

    dmod_all = _allgather8(dmod.reshape(L, 9 * D))
    g_ada_b = _sum_lead(dmod_all)
    dmod_cols = jnp.moveaxis(_my_shard(dmod_all, 2, chip), 0, 1)
    g_ada_w, de_part = _mod_bwd(e, dmod_cols, ada_w)
    g_cond_w, g_cond_b = _cond_bwd(_allgather8(de_part), pre, c_cols)

    small_names = list(SMALL)
    g_full['cond_b'] = g_cond_b.reshape(D)
    g_full['ada_b'] = g_ada_b
    reduced = [k for k in small_names if k not in ('cond_b', 'ada_b')]
    red = _unpack(_sum_lead(_allgather8(_pack([g_full[k] for k in reduced]))), [g_full[k].shape for k in reduced])
    for k, a in zip(reduced, red):
        g_full[k] = a
    grads = {k: (g_full[k] if SMALL[k] is None else _my_shard(g_full[k], SMALL[k], chip)) for k in small_names}
    grads['cond_w'] = g_cond_w
    grads['ada_w'] = g_ada_w

    big_names = list(big_grads)
    by_half = []
    for name in big_names:
        g = big_grads[name]
        rows = g.shape[1] * g.shape[2] // N_CHIPS
        by_half.append(g.reshape(g.shape[0], N_CHIPS, 2, rows // 2, g.shape[3]))
    got = _sibling_send(by_half, True, "pair_send")
    pair = [_add_cast(lax.dynamic_index_in_dim(a, mc, axis=2, keepdims=False), b, BF16) for a, b in zip(by_half, got)]
    mine = [_sum_lead(a) for a in _grad_scatter(pair)]
    theirs = _sibling_send(mine, False, "pair_return")
    flip = lambda a: jnp.swapaxes(a, -1, -2)
    for name, a, b in zip(big_names, mine, theirs):
        both = jnp.where(mc == 0, jnp.stack([a, b], axis=1), jnp.stack([b, a], axis=1))
        shape = w_in[name].shape
        grads[name] = both.reshape(shape[:-2] + (shape[-1], shape[-2])) if name in transposed else both.reshape(shape)

    delta, new_m, new_v = {}, {}, {}
    packed = [_pack([d[k] for k in small_names]) for d in (w_in, grads, m_in, v_in)]
    shapes = [w_in[k].shape for k in small_names]
    for out, buf in zip((delta, new_m, new_v), _adamw(*packed)):
        out.update(zip(small_names, _unpack(buf, shapes)))
    for k in WEIGHTS:
        if k in transposed:
            delta[k], new_m[k], new_v[k] = (flip(a) for a in _adamw(flip(w_in[k]), grads[k], flip(m_in[k]), flip(v_in[k])))
            grads[k] = flip(grads[k])
        elif k not in SMALL:
            delta[k], new_m[k], new_v[k] = _adamw(w_in[k], grads[k], m_in[k], v_in[k])
    return (loss, grad_x, *[grads[k] for k in WEIGHTS], *[delta[k] for k in WEIGHTS], *[new_m[k] for k in WEIGHTS],
            *[new_v[k] for k in WEIGHTS])
```

```python
import functools

import jax
import jax.numpy as jnp
from jax import lax
from jax.experimental import pallas as pl
from jax.experimental.pallas import tpu as pltpu

F32 = jnp.float32
BF16 = jnp.bfloat16
SDS = jax.ShapeDtypeStruct
MESH = pl.DeviceIdType.MESH
ANY = pl.BlockSpec(memory_space=pl.ANY)

EPS = 1e-6
N_CHIPS = 4
N_DEV = 8
LANES = 128
HEAD_DIM = 64
VMEM_LIMIT_BYTES = 56 * 2**20
TOKEN_TILE = 512
WGRAD_TILE = 1024
ATTN_Q_TILE = 1024
ATTN_K_TILE = 256
ATTN_DEAD_BITS = 160.0
LOG2E = 1.4426950408889634
LN2 = 0.6931471805599453
HALO = 32
CONV_ROWS, CONV_COLS = 32, 256
ADAM_TILE_BYTES = 1 << 20
POOL_LEVELS = (1, 2, 3, 4)

ADAM_LR, ADAM_B1, ADAM_B2, ADAM_EPS, ADAM_WD, ADAM_STEP = 0.001, 0.9, 0.999, 1e-08, 0.01, 10

WEIGHTS = ['cond_w', 'cond_b', 'ada_w', 'ada_b', 'norm_g', 'ffn_w1', 'ffn_w3', 'ffn_w2', 'a_w_in', 'a_b_in', 'a_dw',
           'a_dw_b', 'a_ln_g', 'a_ln_b', 'a_w_out', 'a_b_out', 'b_w_qkv', 'b_q_g', 'b_k_g', 'b_w_o', 'p_w', 'p_b', 'p_scale']
SMALL = {'norm_g': 2, 'a_b_in': 1, 'a_dw': 2, 'a_dw_b': 1, 'a_ln_g': 1, 'a_ln_b': 1, 'a_b_out': 1, 'p_w': 2, 'p_b': 2,
         'p_scale': 1, 'cond_b': None, 'ada_b': None, 'b_q_g': None, 'b_k_g': None}


def _tile(n, pref):
    return pref if n % pref == 0 else n


def _params(*sem):
    return pltpu.CompilerParams(dimension_semantics=sem, vmem_limit_bytes=VMEM_LIMIT_BYTES)


def _resident(shape):
    nd = len(shape)
    return pl.BlockSpec(shape, lambda *_: (0,) * nd, pipeline_mode=pl.Buffered(1))


def _rows(tm, width):
    return pl.BlockSpec((tm, width), lambda i: (i, 0))


def _acc_spec(rows, width):
    return pl.BlockSpec((rows, width), lambda i: (0, 0))


def _dot(a, b):
    return jnp.dot(a, b, preferred_element_type=F32)


def _dot_nt(a, b):
    return lax.dot_general(a, b, (((1,), (1,)), ((), ())), preferred_element_type=F32)


def _dot_tn(a, b):
    return lax.dot_general(a, b, (((0,), (0,)), ((), ())), preferred_element_type=F32)


def _hdot(a, b, dims=(((1,), (0,)), ((), ()))):
    return lax.dot_general(a, b, dims, preferred_element_type=F32, precision=lax.Precision.HIGHEST)


def _zero_at_first_step(*refs):
    @pl.when(pl.program_id(0) == 0)
    def _():
        for r in refs:
            r[...] = jnp.zeros_like(r)


def _add_rowsum(ref, row, t):
    ref[row:row + 1, :] += jnp.sum(t, axis=0, keepdims=True)


def _rms(x):
    r = lax.rsqrt(jnp.mean(x * x, axis=-1, keepdims=True) + EPS)
    return x * r, r


def _modulate_bwd(dh, n, r, gs, dxo, sums_ref):
    _add_rowsum(sums_ref, 0, dh * n)
    _add_rowsum(sums_ref, 1, dh)
    dn = dh * gs
    return dxo + r * (dn - n * jnp.mean(dn * n, axis=-1, keepdims=True))


def _silu_grad(a, sg):
    return sg * (1.0 + a * (1.0 - sg))


def _vec(*rows):
    d = rows[0].shape[-1]
    rows = [r.reshape(1, d).astype(F32) for r in rows]
    return jnp.concatenate(rows + [jnp.zeros((8 - len(rows), d), F32)], axis=0)


def _layer(w, at):
    rest = w.shape[len(at):]
    return pl.BlockSpec((None,) * len(at) + rest, lambda *_: tuple(at) + (0,) * len(rest), pipeline_mode=pl.Buffered(1))


def _ffn_fwd(x, vec, w1, w3, w2, at):
    S, D = x.shape
    nch, _, fc = w1.shape[-3:]
    tm = _tile(S, TOKEN_TILE)

    def body(x_ref, vec_ref, w1_ref, w3_ref, w2_ref, xo_ref, a1_ref, a3_ref, y_ref):
        x = x_ref[...]
        n, _ = _rms(x)
        h = (n * vec_ref[0:1, :] + vec_ref[1:2, :]).astype(BF16)
        acc = jnp.zeros((tm, D), F32)
        for j in range(nch):
            a1 = _dot(h, w1_ref[j]).astype(BF16)
            a3 = _dot(h, w3_ref[j]).astype(BF16)
            a1_ref[j] = a1
            a3_ref[j] = a3
            a1 = a1.astype(F32)
            u = a1 * jax.nn.sigmoid(a1) * a3.astype(F32)
            acc = acc + _dot(u.astype(BF16), w2_ref[j])
        y_ref[...] = acc.astype(BF16)
        xo_ref[...] = x + vec_ref[2:3, :] * acc

    chunked = pl.BlockSpec((nch, tm, fc), lambda i: (0, i, 0))
    return pl.pallas_call(
        body, name="ffn_fwd", grid=(S // tm,),
        in_specs=[_rows(tm, D), _resident((8, D)), _layer(w1, at), _layer(w3, at), _layer(w2, at)],
        out_specs=[_rows(tm, D), chunked, chunked, _rows(tm, D)],
        out_shape=[SDS((S, D), F32), SDS((nch, S, fc), BF16), SDS((nch, S, fc), BF16), SDS((S, D), BF16)],
        compiler_params=_params("parallel"),
    )(x, vec, w1, w3, w2)


def _ffn_bwd(x, dxo, a1, a3, y, vec, w1, w3, w2, at):
    S, D = x.shape
    nch, _, fc = w1.shape[-3:]
    tm = _tile(S, TOKEN_TILE // 2)

    def body(x_ref, dxo_ref, a1_ref, a3_ref, y_ref, vec_ref, w1_ref, w3_ref, w2_ref,
             dx_ref, h_ref, dy_ref, u_ref, da1_ref, da3_ref, sums_ref):
        _zero_at_first_step(sums_ref)
        x = x_ref[...]
        dxo = dxo_ref[...]
        gs = vec_ref[0:1, :]
        n, r = _rms(x)
        h_ref[...] = (n * gs + vec_ref[1:2, :]).astype(BF16)
        _add_rowsum(sums_ref, 2, dxo * y_ref[...].astype(F32))
        dy = (vec_ref[2:3, :] * dxo).astype(BF16)
        dy_ref[...] = dy
        dh = jnp.zeros((tm, D), F32)
        for j in range(nch):
            a1 = a1_ref[j].astype(F32)
            a3 = a3_ref[j].astype(F32)
            sg = jax.nn.sigmoid(a1)
            s = a1 * sg
            du = _dot_nt(dy, w2_ref[j])
            da1 = (du * a3 * _silu_grad(a1, sg)).astype(BF16)
            da3 = (du * s).astype(BF16)
            u_ref[j] = (s * a3).astype(BF16)
            da1_ref[j] = da1
            da3_ref[j] = da3
            dh = dh + _dot_nt(da1, w1_ref[j]) + _dot_nt(da3, w3_ref[j])
        dx_ref[...] = _modulate_bwd(dh, n, r, gs, dxo, sums_ref)

    chunked = pl.BlockSpec((nch, tm, fc), lambda i: (0, i, 0))
    return pl.pallas_call(
        body, name="ffn_bwd", grid=(S // tm,),
        in_specs=[_rows(tm, D), _rows(tm, D), chunked, chunked, _rows(tm, D), _resident((8, D)),
                  _layer(w1, at), _layer(w3, at), _layer(w2, at)],
        out_specs=[_rows(tm, D), _rows(tm, D), _rows(tm, D), chunked, chunked, chunked, _acc_spec(8, D)],
        out_shape=[SDS((S, D), F32), SDS((S, D), BF16), SDS((S, D), BF16), SDS((nch, S, fc), BF16),
                   SDS((nch, S, fc), BF16), SDS((nch, S, fc), BF16), SDS((8, D), F32)],
        compiler_params=_params("arbitrary"),
    )(x, dxo, a1, a3, y, vec, w1, w3, w2)


def _wgrad(a, b, a_mode, b_mode, nch, name, acc, slot):
    S = a.shape[-2]
    M = a.shape[-1]
    N = b.shape[-1] // (nch if b_mode == 'col' else 1)
    assert acc.shape[1:] == (nch, M, N)
    ts = _tile(S, WGRAD_TILE)

    def spec(mode, width):
        if mode == 'full':
            return pl.BlockSpec((ts, width), lambda j, s: (s, 0))
        if mode == 'lead':
            return pl.BlockSpec((None, ts, width), lambda j, s: (j, s, 0))
        return pl.BlockSpec((ts, width), lambda j, s: (s, j))

    def body(a_ref, b_ref, acc_ref, o_ref):
        @pl.when(pl.program_id(1) == 0)
        def _():
            o_ref[...] = jnp.zeros_like(o_ref)
        o_ref[...] += _dot_tn(a_ref[...], b_ref[...])

    return pl.pallas_call(
        body, name=name, grid=(nch, S // ts), in_specs=[spec(a_mode, M), spec(b_mode, N), ANY],
        out_specs=pl.BlockSpec((None, None, M, N), lambda j, s: (slot, j, 0, 0)), out_shape=SDS(acc.shape, F32),
        input_output_aliases={2: 0}, compiler_params=_params("parallel", "arbitrary"),
    )(a, b, acc)


def _in_fwd(x, vec, w, bias, glu, name):
    S, D = x.shape
    nch, _, nc = w.shape
    N = nch * nc
    tm = _tile(S, TOKEN_TILE)

    def body(x_ref, vec_ref, w_ref, b_ref, p_ref, *u_ref):
        n, _ = _rms(x_ref[...])
        h = (n * vec_ref[0:1, :] + vec_ref[1:2, :]).astype(BF16)
        for j in range(nch):
            cols = slice(j * nc, (j + 1) * nc)
            p_ref[:, cols] = (_dot(h, w_ref[j]) + b_ref[:, cols]).astype(p_ref.dtype)
        if glu:
            half = N // 2
            u_ref[0][...] = p_ref[:, :half].astype(F32) * jax.nn.sigmoid(p_ref[:, half:].astype(F32))

    out_specs = [_rows(tm, N)] + ([_rows(tm, N // 2)] if glu else [])
    out_shape = [SDS((S, N), BF16 if glu else F32)] + ([SDS((S, N // 2), F32)] if glu else [])
    return pl.pallas_call(
        body, name=name, grid=(S // tm,),
        in_specs=[_rows(tm, D), _resident((8, D)), _resident(w.shape), _resident((1, N))],
        out_specs=out_specs, out_shape=out_shape, compiler_params=_params("parallel"),
    )(x, vec, w, bias)


def _in_bwd(x, dxo, dp, vec, w, name):
    S, D = x.shape
    nch, _, nc = w.shape
    tm = _tile(S, TOKEN_TILE)

    def body(x_ref, dxo_ref, dp_ref, vec_ref, w_ref, dx_ref, h_ref, sums_ref):
        _zero_at_first_step(sums_ref)
        gs = vec_ref[0:1, :]
        n, r = _rms(x_ref[...])
        h_ref[...] = (n * gs + vec_ref[1:2, :]).astype(BF16)
        dh = jnp.zeros((tm, D), F32)
        for j in range(nch):
            dh = dh + _dot_nt(dp_ref[:, j * nc:(j + 1) * nc], w_ref[j])
        dx_ref[...] = _modulate_bwd(dh, n, r, gs, dxo_ref[...], sums_ref)

    return pl.pallas_call(
        body, name=name, grid=(S // tm,),
        in_specs=[_rows(tm, D), _rows(tm, D), _rows(tm, nch * nc), _resident((8, D)), _resident(w.shape)],
        out_specs=[_rows(tm, D), _rows(tm, D), _acc_spec(8, D)],
        out_shape=[SDS((S, D), F32), SDS((S, D), BF16), SDS((8, D), F32)],
        compiler_params=_params("arbitrary"),
    )(x, dxo, dp, vec, w)


def _out_fwd(x, t, vec, w):
    S, D = x.shape
    tm = _tile(S, TOKEN_TILE)

    def body(x_ref, t_ref, vec_ref, w_ref, xo_ref, y_ref):
        y = _dot(t_ref[...], w_ref[...])
        y_ref[...] = y.astype(BF16)
        xo_ref[...] = x_ref[...] + vec_ref[2:3, :] * y

    return pl.pallas_call(
        body, name="attn_out_fwd", grid=(S // tm,),
        in_specs=[_rows(tm, D), _rows(tm, t.shape[1]), _resident((8, D)), _resident(w.shape)],
        out_specs=[_rows(tm, D), _rows(tm, D)], out_shape=[SDS((S, D), F32), SDS((S, D), BF16)],
        compiler_params=_params("parallel"),
    )(x, t, vec, w)


def _out_bwd(dxo, y, vec, w):
    S, D = dxo.shape
    K = w.shape[0]
    tm = _tile(S, TOKEN_TILE)

    def body(dxo_ref, y_ref, vec_ref, w_ref, dt_ref, dout_ref, sums_ref):
        _zero_at_first_step(sums_ref)
        dxo = dxo_ref[...]
        _add_rowsum(sums_ref, 2, dxo * y_ref[...].astype(F32))
        dout = (vec_ref[2:3, :] * dxo).astype(BF16)
        dout_ref[...] = dout
        dt_ref[...] = _dot_nt(dout, w_ref[...]).astype(BF16)

    return pl.pallas_call(
        body, name="attn_out_bwd", grid=(S // tm,),
        in_specs=[_rows(tm, D), _rows(tm, D), _resident((8, D)), _resident(w.shape)],
        out_specs=[_rows(tm, K), _rows(tm, D), _acc_spec(8, D)],
        out_shape=[SDS((S, K), BF16), SDS((S, D), BF16), SDS((8, D), F32)],
        compiler_params=_params("arbitrary"),
    )(dxo, y, vec, w)


def _prev_halo(tm, width):
    return pl.BlockSpec((HALO, width), lambda i: (jnp.maximum(i * (tm // HALO) - 1, 0), 0))


def _next_halo(tm, width, n_rows):
    last = n_rows // HALO - 1
    return pl.BlockSpec((HALO, width), lambda i: (jnp.minimum((i + 1) * (tm // HALO), last), 0))


def _layer_norm(v, g, b):
    mu = jnp.mean(v, axis=-1, keepdims=True)
    vc = v - mu
    rstd = lax.rsqrt(jnp.mean(vc * vc, axis=-1, keepdims=True) + EPS)
    vh = vc * rstd
    return vh * g + b, vh, rstd


def _fill_shifts(ext, sh, n):
    ext[n:, :] = jnp.zeros((8, ext.shape[1]), F32)
    for b in range(1, 8):
        sh[b - 1] = ext[pl.ds(b, n), :]


def _shifted(ext, sh, off, r0, rows, cols):
    b = off % 8
    base = r0 + off - b
    return ext[pl.ds(base, rows), cols] if b == 0 else sh[b - 1, pl.ds(base, rows), cols]


def _conv_mid_fwd(u, x, vec, dw, w_out):
    S, D = x.shape
    taps = dw.shape[0] - 1
    tm = _tile(S, TOKEN_TILE // 2)

    def body(u_ref, uh_ref, x_ref, vec_ref, dw_ref, w_ref, xo_ref, v_ref, y_ref, ext, sh):
        n = HALO + tm
        ext[0:HALO, :] = jnp.where(pl.program_id(0) > 0, uh_ref[...], 0.0)
        ext[HALO:n, :] = u_ref[...]
        _fill_shifts(ext, sh, n)
        cb = min(CONV_COLS, D)
        for c0 in range(0, D, cb):
            cols = slice(c0, c0 + cb)
            w = [dw_ref[k:k + 1, cols] for k in range(taps)]
            for r0 in range(0, tm, CONV_ROWS):
                acc = jnp.zeros((CONV_ROWS, cb), F32) + vec_ref[1:2, cols]
                for k in range(taps):
                    acc = acc + w[k] * _shifted(ext, sh, HALO - (taps - 1) + k, r0, CONV_ROWS, cols)
                v_ref[r0:r0 + CONV_ROWS, cols] = acc
        v = v_ref[...]
        l, _, _ = _layer_norm(v, vec_ref[2:3, :], vec_ref[3:4, :])
        q = (l * jax.nn.sigmoid(l)).astype(BF16)
        y = _dot(q, w_ref[...]) + vec_ref[4:5, :]
        y_ref[...] = y.astype(BF16)
        xo_ref[...] = x_ref[...] + vec_ref[0:1, :] * y

    return pl.pallas_call(
        body, name="conv_mid_fwd", grid=(S // tm,),
        in_specs=[_rows(tm, D), _prev_halo(tm, D), _rows(tm, D), _resident((8, D)), _resident(dw.shape), _resident(w_out.shape)],
        out_specs=[_rows(tm, D), _rows(tm, D), _rows(tm, D)],
        out_shape=[SDS((S, D), F32), SDS((S, D), F32), SDS((S, D), BF16)],
        scratch_shapes=[pltpu.VMEM((HALO + tm + 8, D), F32), pltpu.VMEM((7, HALO + tm, D), F32)],
        compiler_params=_params("parallel"),
    )(u, u, x, vec, dw, w_out)


def _conv_mid_bwd(dxo, y, v, vec, w_out):
    S, D = dxo.shape
    tm = _tile(S, TOKEN_TILE)

    def body(dxo_ref, y_ref, v_ref, vec_ref, w_ref, dv_ref, q_ref, dout_ref, sums_ref):
        _zero_at_first_step(sums_ref)
        dxo = dxo_ref[...]
        _add_rowsum(sums_ref, 0, dxo * y_ref[...].astype(F32))
        dout = vec_ref[0:1, :] * dxo
        _add_rowsum(sums_ref, 1, dout)
        dout = dout.astype(BF16)
        dout_ref[...] = dout
        ln_g = vec_ref[2:3, :]
        l, vh, rstd = _layer_norm(v_ref[...], ln_g, vec_ref[3:4, :])
        sg = jax.nn.sigmoid(l)
        q_ref[...] = (l * sg).astype(BF16)
        dl = _dot_nt(dout, w_ref[...]) * _silu_grad(l, sg)
        _add_rowsum(sums_ref, 2, dl * vh)
        _add_rowsum(sums_ref, 3, dl)
        dvh = dl * ln_g
        dv = rstd * (dvh - jnp.mean(dvh, axis=-1, keepdims=True) - vh * jnp.mean(dvh * vh, axis=-1, keepdims=True))
        _add_rowsum(sums_ref, 4, dv)
        dv_ref[...] = dv

    return pl.pallas_call(
        body, name="conv_mid_bwd", grid=(S // tm,),
        in_specs=[_rows(tm, D), _rows(tm, D), _rows(tm, D), _resident((8, D)), _resident(w_out.shape)],
        out_specs=[_rows(tm, D), _rows(tm, D), _rows(tm, D), _acc_spec(8, D)],
        out_shape=[SDS((S, D), F32), SDS((S, D), BF16), SDS((S, D), BF16), SDS((8, D), F32)],
        compiler_params=_params("arbitrary"),
    )(dxo, y, v, vec, w_out)


def _conv_transpose(dv, u, dw):
    S, D = dv.shape
    taps = dw.shape[0] - 1
    tm = _tile(S, TOKEN_TILE // 2)

    def body(dv_ref, dvn_ref, u_ref, uh_ref, dw_ref, du_ref, gdw_ref, extv, shv, extu, shu):
        _zero_at_first_step(gdw_ref)
        i = pl.program_id(0)
        n = HALO + tm
        extv[0:tm, :] = dv_ref[...]
        extv[tm:n, :] = jnp.where(i < pl.num_programs(0) - 1, dvn_ref[...], 0.0)
        _fill_shifts(extv, shv, n)
        extu[0:HALO, :] = jnp.where(i > 0, uh_ref[...], 0.0)
        extu[HALO:n, :] = u_ref[...]
        _fill_shifts(extu, shu, n)
        cb = min(CONV_COLS, D)
        for c0 in range(0, D, cb):
            cols = slice(c0, c0 + cb)
            w = [dw_ref[k:k + 1, cols] for k in range(taps)]
            for r0 in range(0, tm, CONV_ROWS):
                acc = jnp.zeros((CONV_ROWS, cb), F32)
                for k in range(taps):
                    acc = acc + w[k] * _shifted(extv, shv, taps - 1 - k, r0, CONV_ROWS, cols)
                du_ref[r0:r0 + CONV_ROWS, cols] = acc
        for c0 in range(0, D, LANES):
            cols = slice(c0, c0 + LANES)
            accs = [jnp.zeros((8, LANES), F32) for _ in range(taps)]
            for r0 in range(0, tm, CONV_ROWS):
                dvb = extv[r0:r0 + CONV_ROWS, cols]
                for k in range(taps):
                    p = dvb * _shifted(extu, shu, HALO - (taps - 1) + k, r0, CONV_ROWS, cols)
                    for s in range(0, CONV_ROWS, 8):
                        accs[k] = accs[k] + p[s:s + 8]
            for k in range(taps):
                gdw_ref[k:k + 1, cols] += jnp.sum(accs[k], axis=0, keepdims=True)

    return pl.pallas_call(
        body, name="conv_transpose", grid=(S // tm,),
        in_specs=[_rows(tm, D), _next_halo(tm, D, S), _rows(tm, D), _prev_halo(tm, D), _resident(dw.shape)],
        out_specs=[_rows(tm, D), _acc_spec(dw.shape[0], D)],
        out_shape=[SDS((S, D), F32), SDS(dw.shape, F32)],
        scratch_shapes=[pltpu.VMEM((HALO + tm + 8, D), F32), pltpu.VMEM((7, HALO + tm, D), F32),
                        pltpu.VMEM((HALO + tm + 8, D), F32), pltpu.VMEM((7, HALO + tm, D), F32)],
        compiler_params=_params("arbitrary"),
    )(dv, dv, u, u, dw)


def _glu_bwd(du, p):
    S, D = du.shape
    tm = _tile(S, TOKEN_TILE)

    def body(du_ref, p_ref, dp_ref, sums_ref):
        _zero_at_first_step(sums_ref)
        du = du_ref[...]
        a = p_ref[:, :D].astype(F32)
        sb = jax.nn.sigmoid(p_ref[:, D:].astype(F32))
        da = du * sb
        db = du * a * sb * (1.0 - sb)
        dp_ref[:, :D] = da.astype(BF16)
        dp_ref[:, D:] = db.astype(BF16)
        sums_ref[0:1, :D] += jnp.sum(da, axis=0, keepdims=True)
        sums_ref[0:1, D:] += jnp.sum(db, axis=0, keepdims=True)

    return pl.pallas_call(
        body, name="glu_bwd", grid=(S // tm,), in_specs=[_rows(tm, D), _rows(tm, 2 * D)],
        out_specs=[_rows(tm, 2 * D), _acc_spec(8, 2 * D)], out_shape=[SDS((S, 2 * D), BF16), SDS((8, 2 * D), F32)],
        compiler_params=_params("arbitrary"),
    )(du, p)


def _head_mean(t, bd):
    hi = t.astype(BF16)
    lo = (t - hi.astype(F32)).astype(BF16)
    return (_dot(hi, bd) + _dot(lo, bd)) * (1.0 / HEAD_DIM)


def _head_blocks():
    r = lax.broadcasted_iota(jnp.int32, (LANES, LANES), 0) // HEAD_DIM
    c = lax.broadcasted_iota(jnp.int32, (LANES, LANES), 1) // HEAD_DIM
    return (r == c).astype(BF16)


def _qknorm_fwd(raw, gq, gk):
    S, D3 = raw.shape
    D = D3 // 3
    tm = _tile(S, TOKEN_TILE)

    def body(raw_ref, gq_ref, gk_ref, o_ref):
        bd = _head_blocks()
        for off, g_ref in ((0, gq_ref), (D, gk_ref)):
            for c in range(D // LANES):
                cols = slice(off + c * LANES, off + (c + 1) * LANES)
                xs = raw_ref[:, cols]
                r = lax.rsqrt(_head_mean(xs * xs, bd) + EPS)
                o_ref[:, cols] = (xs * r * g_ref[:, c * LANES:(c + 1) * LANES]).astype(BF16)
        o_ref[:, 2 * D:] = raw_ref[:, 2 * D:].astype(BF16)

    return pl.pallas_call(
        body, name="qknorm_fwd", grid=(S // tm,), in_specs=[_rows(tm, D3), _resident((1, D)), _resident((1, D))],
        out_specs=_rows(tm, D3), out_shape=SDS((S, D3), BF16), compiler_params=_params("parallel"),
    )(raw, gq, gk)


def _qknorm_bwd(dq, dk, dv, raw, gq, gk):
    S, D3 = raw.shape
    D = D3 // 3
    nb = D // LANES
    tm = _tile(S, TOKEN_TILE)

    def body(dq_ref, dk_ref, dv_ref, raw_ref, gq_ref, gk_ref, o_ref, sums_ref):
        _zero_at_first_step(sums_ref)
        bd = _head_blocks()
        for row, (off, g_ref, d_ref) in enumerate(((0, gq_ref, dq_ref), (D, gk_ref, dk_ref))):
            for c in range(nb):
                lanes = slice(c * LANES, (c + 1) * LANES)
                cols = slice(off + c * LANES, off + (c + 1) * LANES)
                xs = raw_ref[:, cols]
                r = lax.rsqrt(_head_mean(xs * xs, bd) + EPS)
                n = xs * r
                dhat = d_ref[c]
                sums_ref[row:row + 1, lanes] += jnp.sum(dhat * n, axis=0, keepdims=True)
                dn = dhat * g_ref[:, lanes]
                o_ref[:, cols] = (r * (dn - n * _head_mean(dn * n, bd))).astype(BF16)
        for c in range(nb):
            o_ref[:, 2 * D + c * LANES:2 * D + (c + 1) * LANES] = dv_ref[c].astype(BF16)

    tiles = pl.BlockSpec((nb, tm, LANES), lambda i: (0, i, 0))
    return pl.pallas_call(
        body, name="qknorm_bwd", grid=(S // tm,),
        in_specs=[tiles, tiles, tiles, _rows(tm, D3), _resident((1, D)), _resident((1, D))],
        out_specs=[_rows(tm, D3), _acc_spec(8, D)], out_shape=[SDS((S, D3), BF16), SDS((8, D), F32)],
        compiler_params=_params("arbitrary"),
    )(dq, dk, dv, raw, gq, gk)


def _softplus2(z):
    return jnp.maximum(z, jnp.log2(1.0 + jnp.exp2(jnp.minimum(z, 30.0))))


def _attn_tiles(S):
    tq = _tile(S, ATTN_Q_TILE)
    tk = _tile(tq, ATTN_K_TILE)
    return tq, tk


def _attn_consts(tq, tk):
    lane = lax.broadcasted_iota(jnp.int32, (1, LANES), 1)
    first = lane < HEAD_DIM
    r = lax.broadcasted_iota(jnp.int32, (tq, tk), 0)
    c = lax.broadcasted_iota(jnp.int32, (tq, tk), 1)
    kr = lax.broadcasted_iota(jnp.int32, (tk, tk), 0)
    kc = lax.broadcasted_iota(jnp.int32, (tk, tk), 1)
    tri = c < r
    later = (kr > kc).astype(BF16)
    upto = (kr <= kc).astype(BF16)
    return first, tri, later, upto


def _row_sums(t):
    return jnp.broadcast_to(jnp.sum(t, axis=1, keepdims=True), (t.shape[0], LANES))


def _over_keys(per_row, tk):
    return per_row[:, :tk] if tk <= LANES else jnp.concatenate([per_row] * (tk // LANES), axis=1)


def _from(x, lo, axis=0):
    return x if lo == 0 else lax.slice_in_dim(x, lo, x.shape[axis], axis=axis)


def _add_from(acc, lo, part):
    return acc + (part if lo == 0 else jnp.concatenate([jnp.zeros((lo,) + part.shape[1:], part.dtype), part], axis=0))


def _attn_specs(S, tq, nb):
    q_spec = pl.BlockSpec((tq, LANES), lambda hp, i: (i, hp))
    k_spec = pl.BlockSpec((S, LANES), lambda hp, i: (0, nb + hp), pipeline_mode=pl.Buffered(1))
    v_spec = pl.BlockSpec((S, LANES), lambda hp, i: (0, 2 * nb + hp), pipeline_mode=pl.Buffered(1))
    return q_spec, k_spec, v_spec


def _attn_fwd(qkv):
    S, D3 = qkv.shape
    D = D3 // 3
    nb = D // LANES
    tq, tk = _attn_tiles(S)
    nsub = tq // tk

    def body(q_ref, k_ref, v_ref, o_ref, tot_ref, start_ref):
        hp = pl.program_id(0)
        i = pl.program_id(1)
        first, tri, later, _ = _attn_consts(tq, tk)
        q = q_ref[...]
        qs = (jnp.where(first, q, jnp.zeros_like(q)), jnp.where(first, jnp.zeros_like(q), q))

        def tiles(t0, n, carry, diagonal):
            out = []
            for h in range(2):
                o, c = carry[h]
                for a in reversed(range(n)):
                    lo = a * tk if diagonal else 0
                    mask = tri[:tq - lo] if diagonal else None
                    rows = pl.ds(pl.multiple_of((t0 + a) * tk, tk), tk)
                    kb = k_ref[rows, :]
                    z = _dot_nt(_from(qs[h], lo), kb)
                    sp = _softplus2(z)
                    logsig = z - sp
                    if diagonal:
                        sp = jnp.where(mask, sp, 0.0)
                    av = jnp.exp2(logsig - _dot(sp.astype(BF16), later) - _over_keys(_from(c, lo), tk))
                    if diagonal:
                        av = jnp.where(mask, av, 0.0)
                    o = _add_from(o, lo, _dot(av.astype(BF16), v_ref[rows, :]))
                    c = _add_from(c, lo, _row_sums(sp))
                out.append((o, c))
            return tuple(out)

        def live(carry):
            return jnp.minimum(jnp.min(carry[0][1]), jnp.min(carry[1][1])) < ATTN_DEAD_BITS

        carry = tuple((jnp.zeros((tq, LANES), F32), jnp.zeros((tq, LANES), F32)) for _ in range(2))
        carry = tiles(i * nsub, nsub, carry, True)
        band_floor = jnp.maximum((i - 1) * nsub, 0)
        t, carry = lax.while_loop(lambda st: (st[0] >= band_floor) & (st[0] >= 0) & live(st[1]),
                                  lambda st: (st[0] - 1, tiles(st[0], 1, st[1], False)), (i * nsub - 1, carry))
        t, carry = lax.while_loop(lambda st: (st[0] >= nsub - 1) & live(st[1]),
                                  lambda st: (st[0] - nsub, tiles(st[0] - (nsub - 1), nsub, st[1], False)), (t, carry))
        (o_a, c_a), (o_b, c_b) = carry
        o_ref[...] = jnp.where(first, o_a, o_b).astype(BF16)
        tot_ref[...] = jnp.where(first, c_a, c_b)
        start_ref[hp, i] = t + 1

    q_spec, k_spec, v_spec = _attn_specs(S, tq, nb)
    return pl.pallas_call(
        body, name="attn_fwd", grid=(nb, S // tq), in_specs=[q_spec, k_spec, v_spec],
        out_specs=[q_spec, q_spec, pl.BlockSpec(memory_space=pltpu.SMEM)],
        out_shape=[SDS((S, D), BF16), SDS((S, D), F32), SDS((nb, S // tq), jnp.int32)],
        compiler_params=_params("arbitrary", "arbitrary"),
    )(qkv, qkv, qkv)


def _attn_bwd(qkv, do, tot, start):
    S, D3 = qkv.shape
    D = D3 // 3
    nb = D // LANES
    tq, tk = _attn_tiles(S)
    nsub = tq // tk
    nkb = S // tk

    def body(start_ref, q_ref, k_ref, v_ref, do_ref, tot_ref, dq_ref, dk_hbm, dv_hbm, dkt_acc, dvt_acc, stage, sem):
        hp = pl.program_id(0)
        i = pl.program_id(1)

        @pl.when(i == 0)
        def _():
            dkt_acc[...] = jnp.zeros_like(dkt_acc)
            dvt_acc[...] = jnp.zeros_like(dvt_acc)

        first, tri, later, upto = _attn_consts(tq, tk)
        q = q_ref[...]
        do = do_ref[...]
        zero = jnp.zeros_like(q)
        qs = (jnp.where(first, q, zero), jnp.where(first, zero, q))
        dos = (jnp.where(first, do, zero), jnp.where(first, zero, do))
        qt = q.astype(F32).T.astype(BF16)
        dot_ = do.astype(F32).T.astype(BF16)
        qts = (qt[:HEAD_DIM], qt[HEAD_DIM:])
        dots = (dot_[:HEAD_DIM], dot_[HEAD_DIM:])
        tots = tuple(jnp.broadcast_to(tot_ref[:, l:l + 1], (tq, LANES)) for l in (0, HEAD_DIM))

        def tiles(t0, n, carry, diagonal):
            carry = list(carry)
            for a in range(n):
                lo = a * tk if diagonal else 0
                mask = tri[:tq - lo] if diagonal else None
                j = t0 + a
                rows = pl.ds(pl.multiple_of(j * tk, tk), tk)
                kb = k_ref[rows, :]
                vb = v_ref[rows, :]
                dkts, dvts = [], []
                for h in range(2):
                    dq, cum, pre = carry[h]
                    z = _dot_nt(_from(qs[h], lo), kb)
                    sp = _softplus2(z)
                    logsig = z - sp
                    if diagonal:
                        sp = jnp.where(mask, sp, 0.0)
                    cum = _add_from(cum, lo, _row_sums(sp))
                    av = jnp.exp2(logsig - _dot(sp.astype(BF16), later) - _over_keys(_from(tots[h], lo) - _from(cum, lo), tk))
                    if diagonal:
                        av = jnp.where(mask, av, 0.0)
                    g = _dot_nt(_from(dos[h], lo), vb) * av
                    dz = g - jnp.exp2(logsig) * (_over_keys(_from(pre, lo), tk) + _dot(g.astype(BF16), upto))
                    if diagonal:
                        dz = jnp.where(mask, dz, 0.0)
                    dz = dz.astype(BF16)
                    dkts.append(_dot(_from(qts[h], lo, 1), dz))
                    dvts.append(_dot(_from(dots[h], lo, 1), av.astype(BF16)))
                    carry[h] = (_add_from(dq, lo, _dot(dz, kb)), cum, _add_from(pre, lo, _row_sums(g)))
                dkt_acc[j] += jnp.concatenate(dkts, axis=0)
                dvt_acc[j] += jnp.concatenate(dvts, axis=0)
            return tuple(carry)

        carry = tuple((jnp.zeros((tq, LANES), F32),) * 3 for _ in range(2))
        t0 = start_ref[hp, i]
        odd = lax.rem(i * nsub - t0, nsub)
        carry = lax.fori_loop(0, odd, lambda s, cr: tiles(t0 + s, 1, cr, False), carry)
        carry = lax.fori_loop(0, (i * nsub - t0) // nsub, lambda b, cr: tiles(t0 + odd + b * nsub, nsub, cr, False), carry)
        carry = tiles(i * nsub, nsub, carry, True)
        dq_ref[...] = jnp.where(first, carry[0][0], carry[1][0]) * LN2

        @pl.when(i == pl.num_programs(1) - 1)
        def _():
            def flush(j, _):
                rows = pl.ds(pl.multiple_of(j * tk, tk), tk)
                stage[0] = dkt_acc[j].T * LN2
                stage[1] = dvt_acc[j].T
                ck = pltpu.make_async_copy(stage.at[0], dk_hbm.at[hp, rows], sem.at[0])
                cv = pltpu.make_async_copy(stage.at[1], dv_hbm.at[hp, rows], sem.at[1])
                ck.start()
                cv.start()
                ck.wait()
                cv.wait()
                return 0
            lax.fori_loop(0, nkb, flush, 0)

    q_spec, k_spec, v_spec = _attn_specs(S, tq, nb)
    slab = SDS((nb, S, LANES), F32)
    return pl.pallas_call(
        body, name="attn_bwd", grid=(nb, S // tq),
        in_specs=[pl.BlockSpec(memory_space=pltpu.SMEM), q_spec, k_spec, v_spec, q_spec, q_spec],
        out_specs=[pl.BlockSpec((None, tq, LANES), lambda hp, i: (hp, i, 0)), ANY, ANY], out_shape=[slab, slab, slab],
        scratch_shapes=[pltpu.VMEM((nkb, LANES, tk), F32), pltpu.VMEM((nkb, LANES, tk), F32), pltpu.VMEM((2, tk, LANES), F32),
                        pltpu.SemaphoreType.DMA((2,))],
        compiler_params=_params("arbitrary", "arbitrary"),
    )(start, qkv, qkv, qkv, do, tot)


def _trail_sum(ext, bufs, cols, levels, n):
    def src(lo, size):
        return ext[pl.ds(lo, size), cols]
    for l in range(levels):
        lo = 8 * (l + 1)
        dst = bufs[l % 2]
        dst[lo:, :] = src(lo, n - lo) + src(lo - (1 << l), n - lo)
        def src(lo_, size, d=dst):
            return d[pl.ds(lo_, size), :]
    return src(HALO, n - HALO)


def _lead_sum(ext, bufs, cols, levels, n):
    def src(lo, size):
        return ext[pl.ds(lo, size), cols]
    for l in range(levels):
        hi = n - 8 * (l + 1)
        dst = bufs[l % 2]
        dst[0:hi, :] = src(0, hi) + src(1 << l, hi)
        def src(lo_, size, d=dst):
            return d[pl.ds(lo_, size), :]
    return src(0, n - HALO)


def _pool_diffs(x_ref, xh_ref, vec_ref, ext, bufs, tm, D):
    i = pl.program_id(0)
    gs, shift = vec_ref[0:1, :], vec_ref[1:2, :]
    n, r = _rms(x_ref[...])
    nh, _ = _rms(xh_ref[...])
    ext[0:HALO, :] = jnp.where(i > 0, nh * gs + shift, 0.0)
    ext[HALO:, :] = n * gs + shift
    t = i * tm + lax.broadcasted_iota(jnp.int32, (tm, 1), 0)
    dg = D // len(POOL_LEVELS)
    out = []
    for g, lv in enumerate(POOL_LEVELS):
        cols = slice(g * dg, (g + 1) * dg)
        inv = 1.0 / jnp.minimum(t + 1, 1 << lv).astype(F32)
        out.append((_trail_sum(ext, bufs, cols, lv, HALO + tm) * inv - ext[HALO:, cols], inv))
    return out, n, r


def _pool_fwd(x, vec, pw):
    S, D = x.shape
    ng, dg, _ = pw.shape
    tm = _tile(S, TOKEN_TILE)

    def body(x_ref, xh_ref, vec_ref, pw_ref, xo_ref, ext, buf_a, buf_b):
        diffs, _, _ = _pool_diffs(x_ref, xh_ref, vec_ref, ext, (buf_a, buf_b), tm, D)
        for g, (d, _) in enumerate(diffs):
            cols = slice(g * dg, (g + 1) * dg)
            y = (_dot(d.astype(BF16), pw_ref[g]) + vec_ref[4:5, cols]) * vec_ref[3:4, cols]
            xo_ref[:, cols] = x_ref[:, cols] + vec_ref[2:3, cols] * y

    return pl.pallas_call(
        body, name="pool_fwd", grid=(S // tm,),
        in_specs=[_rows(tm, D), _prev_halo(tm, D), _resident((8, D)), _resident(pw.shape)],
        out_specs=_rows(tm, D), out_shape=SDS((S, D), F32),
        scratch_shapes=[pltpu.VMEM((HALO + tm, D), F32), pltpu.VMEM((HALO + tm, dg), F32), pltpu.VMEM((HALO + tm, dg), F32)],
        compiler_params=_params("parallel"),
    )(x, x, vec, pw)


def _pool_bwd(x, dxo, vec, pw):
    S, D = x.shape
    ng, dg, _ = pw.shape
    tm = _tile(S, TOKEN_TILE)

    def body(x_ref, xh_ref, dxo_ref, dxn_ref, vec_ref, pw_ref, dx_ref, gpw_ref, sums_ref, ext, exte, buf_a, buf_b):
        _zero_at_first_step(gpw_ref, sums_ref)
        i = pl.program_id(0)
        bufs = (buf_a, buf_b)
        diffs, n, r = _pool_diffs(x_ref, xh_ref, vec_ref, ext, bufs, tm, D)
        gate, scale = vec_ref[2:3, :], vec_ref[3:4, :]
        dxo = dxo_ref[...]
        dyp_next = jnp.where(i < pl.num_programs(0) - 1, dxn_ref[...], 0.0) * gate * scale
        t_next = (i + 1) * tm + lax.broadcasted_iota(jnp.int32, (HALO, 1), 0)
        for g, (d, inv) in enumerate(diffs):
            cols = slice(g * dg, (g + 1) * dg)
            w = pw_ref[g]
            db = d.astype(BF16)
            ypre = _dot(db, w) + vec_ref[4:5, cols]
            dy = gate[:, cols] * dxo[:, cols]
            sums_ref[2:3, cols] += jnp.sum(dxo[:, cols] * ypre * scale[:, cols], axis=0, keepdims=True)
            sums_ref[3:4, cols] += jnp.sum(dy * ypre, axis=0, keepdims=True)
            dyp = dy * scale[:, cols]
            sums_ref[4:5, cols] += jnp.sum(dyp, axis=0, keepdims=True)
            dypb = dyp.astype(BF16)
            gpw_ref[g] += _dot_tn(db, dypb)
            dd = _dot_nt(dypb, w)
            dd_next = _dot_nt(dyp_next[:, cols].astype(BF16), w)
            inv_next = 1.0 / jnp.minimum(t_next + 1, 1 << POOL_LEVELS[g]).astype(F32)
            exte[0:tm, cols] = dd * inv
            exte[tm:, cols] = dd_next * inv_next
            ext[HALO:, cols] = _lead_sum(exte, bufs, cols, POOL_LEVELS[g], tm + HALO) - dd
        dx_ref[...] = _modulate_bwd(ext[HALO:, :], n, r, vec_ref[0:1, :], dxo, sums_ref)

    return pl.pallas_call(
        body, name="pool_bwd", grid=(S // tm,),
        in_specs=[_rows(tm, D), _prev_halo(tm, D), _rows(tm, D), _next_halo(tm, D, S), _resident((8, D)), _resident(pw.shape)],
        out_specs=[_rows(tm, D), pl.BlockSpec(pw.shape, lambda i: (0, 0, 0)), _acc_spec(8, D)],
        out_shape=[SDS((S, D), F32), SDS(pw.shape, F32), SDS((8, D), F32)],
        scratch_shapes=[pltpu.VMEM((HALO + tm, D), F32), pltpu.VMEM((tm + HALO, D), F32),
                        pltpu.VMEM((HALO + tm, dg), F32), pltpu.VMEM((HALO + tm, dg), F32)],
        compiler_params=_params("arbitrary"),
    )(x, x, dxo, dxo, vec, pw)


def _loss_head(y, target):
    S, D = y.shape
    tm = _tile(S, TOKEN_TILE)

    def body(y_ref, t_ref, dy_ref, sums_ref):
        _zero_at_first_step(sums_ref)
        err = y_ref[...] - t_ref[...]
        _add_rowsum(sums_ref, 0, err * err)
        dy_ref[...] = err * (1.0 / D)

    return pl.pallas_call(
        body, name="loss_head", grid=(S // tm,), in_specs=[_rows(tm, D), _rows(tm, D)],
        out_specs=[_rows(tm, D), _acc_spec(8, D)], out_shape=[SDS((S, D), F32), SDS((8, D), F32)],
        compiler_params=_params("arbitrary"),
    )(y, target)


def _cond_pre(c_cols, cond_w):
    def body(c_ref, w_ref, o_ref):
        o_ref[...] = _hdot(c_ref[...], w_ref[...])
    return pl.pallas_call(body, name="cond_pre", out_shape=SDS((c_cols.shape[0], cond_w.shape[1]), F32))(c_cols, cond_w)


def _cond_e(parts, cond_b):
    def body(p_ref, b_ref, pre_ref, e_ref):
        pre = p_ref[0] + p_ref[2] + p_ref[4] + p_ref[6] + b_ref[...]
        pre_ref[...] = pre
        e_ref[...] = pre * jax.nn.sigmoid(pre)
    shape = SDS(parts.shape[1:], F32)
    return pl.pallas_call(body, name="cond_e", out_shape=[shape, shape])(parts, cond_b)


def _mod_cols(e, ada_w, ada_b_cols):
    L, D, nc = ada_w.shape
    B = e.shape[0]

    def body(e_ref, w_ref, b_ref, o_ref):
        o_ref[...] = _hdot(e_ref[...], w_ref[...]) + b_ref[...]

    return pl.pallas_call(
        body, name="mod_cols", grid=(L,),
        in_specs=[pl.BlockSpec((B, D), lambda l: (0, 0)), pl.BlockSpec((None, D, nc), lambda l: (l, 0, 0)),
                  pl.BlockSpec((None, 1, nc), lambda l: (l, 0, 0))],
        out_specs=pl.BlockSpec((None, B, nc), lambda l: (l, 0, 0)), out_shape=SDS((L, B, nc), F32),
        compiler_params=_params("parallel"),
    )(e, ada_w, ada_b_cols)


def _mod_bwd(e, dmod_cols, ada_w):
    L, D, nc = ada_w.shape
    B = e.shape[0]

    def body(e_ref, d_ref, w_ref, gw_ref, de_ref):
        _zero_at_first_step(de_ref)
        gw_ref[...] = _hdot(e_ref[...], d_ref[...], (((0,), (0,)), ((), ())))
        de_ref[...] += _hdot(d_ref[...], w_ref[...], (((1,), (1,)), ((), ())))

    return pl.pallas_call(
        body, name="mod_bwd", grid=(L,),
        in_specs=[pl.BlockSpec((B, D), lambda l: (0, 0)), pl.BlockSpec((None, B, nc), lambda l: (l, 0, 0)),
                  pl.BlockSpec((None, D, nc), lambda l: (l, 0, 0))],
        out_specs=[pl.BlockSpec((None, D, nc), lambda l: (l, 0, 0)), pl.BlockSpec((B, D), lambda l: (0, 0))],
        out_shape=[SDS((L, D, nc), F32), SDS((B, D), F32)], compiler_params=_params("arbitrary"),
    )(e, dmod_cols, ada_w)


def _cond_bwd(de_parts, pre, c_cols):
    def body(p_ref, pre_ref, c_ref, gw_ref, gb_ref):
        pre = pre_ref[...]
        dpre = (p_ref[0] + p_ref[2] + p_ref[4] + p_ref[6]) * _silu_grad(pre, jax.nn.sigmoid(pre))
        gb_ref[...] = jnp.sum(dpre, axis=0, keepdims=True)
        gw_ref[...] = _hdot(c_ref[...], dpre, (((0,), (0,)), ((), ())))
    D = pre.shape[1]
    return pl.pallas_call(body, name="cond_bwd", out_shape=[SDS((c_cols.shape[1], D), F32), SDS((1, D), F32)])(de_parts, pre, c_cols)


def _as_rows(a):
    return a.reshape(-1, a.shape[-1])


def _row_tile(rows, width, n_arrays):
    t = max(8, (ADAM_TILE_BYTES // (4 * width)) // 8 * 8)
    while rows % t:
        t -= 8
        if t <= 0:
            return rows
    return t


def _adamw(w, g, m, v):
    shape = w.shape
    w, g, m, v = (_as_rows(a) for a in (w, g, m, v))
    R, C = w.shape
    tr = _row_tile(R, C, 7)

    def body(w_ref, g_ref, m_ref, v_ref, d_ref, nm_ref, nv_ref):
        g = g_ref[...]
        m = ADAM_B1 * m_ref[...] + (1.0 - ADAM_B1) * g
        v = ADAM_B2 * v_ref[...] + (1.0 - ADAM_B2) * (g * g)
        m_hat = m / (1.0 - ADAM_B1 ** ADAM_STEP)
        v_hat = v / (1.0 - ADAM_B2 ** ADAM_STEP)
        d_ref[...] = -ADAM_LR * (m_hat / (jnp.sqrt(v_hat) + ADAM_EPS) + ADAM_WD * w_ref[...])
        nm_ref[...] = m
        nv_ref[...] = v

    spec = _rows(tr, C)
    outs = pl.pallas_call(
        body, name="adamw", grid=(R // tr,), in_specs=[spec] * 4, out_specs=[spec] * 3,
        out_shape=[SDS((R, C), F32)] * 3, compiler_params=_params("parallel"),
    )(w, g, m, v)
    return tuple(o.reshape(shape) for o in outs)


def _sum_lead(a, out_dtype=F32):
    n = a.shape[0]
    shape = a.shape[1:]
    a = a.reshape(n, -1, a.shape[-1])
    _, R, C = a.shape
    tr = _row_tile(R, C, n + 1)

    def body(a_ref, o_ref):
        acc = a_ref[0].astype(F32)
        for k in range(1, n):
            acc = acc + a_ref[k].astype(F32)
        o_ref[...] = acc.astype(out_dtype)

    out = pl.pallas_call(
        body, name="sum_lead", grid=(R // tr,), in_specs=[pl.BlockSpec((n, tr, C), lambda i: (0, i, 0))],
        out_specs=_rows(tr, C), out_shape=SDS((R, C), out_dtype), compiler_params=_params("parallel"),
    )(a)
    return out.reshape(shape)


def _place():
    return lax.axis_index("x"), lax.axis_index("y"), lax.axis_index("c")


def _allgather8(a):
    def body(a_ref, o_ref, send, recv, local):
        mx, my, mc = _place()
        me = 4 * mx + 2 * my + mc
        mine = pltpu.make_async_copy(a_ref, o_ref.at[me], local)
        mine.start()
        copies = []
        for k in range(1, N_DEV):
            peer = (1 - mx if k & 4 else mx, 1 - my if k & 2 else my, 1 - mc if k & 1 else mc)
            cp = pltpu.make_async_remote_copy(a_ref, o_ref.at[me], send.at[k - 1], recv.at[k - 1], device_id=peer, device_id_type=MESH)
            cp.start()
            copies.append(cp)
        for cp in copies:
            cp.wait()
        mine.wait()

    return pl.pallas_call(
        body, name="allgather8", in_specs=[ANY], out_specs=ANY, out_shape=SDS((N_DEV,) + a.shape, a.dtype),
        scratch_shapes=[pltpu.SemaphoreType.DMA((N_DEV - 1,)), pltpu.SemaphoreType.DMA((N_DEV - 1,)), pltpu.SemaphoreType.DMA],
    )(a)


def _grad_scatter(arrs):
    n = len(arrs)
    out_shape = [SDS((N_CHIPS, a.shape[0]) + a.shape[2:], a.dtype) for a in arrs]

    def body(*refs):
        ins, outs, (send, recv, local) = refs[:n], refs[n:2 * n], refs[2 * n:]
        mx, my, mc = _place()
        chip = 2 * mx + my
        copies = []
        for a in range(n):
            cp = pltpu.make_async_copy(ins[a].at[:, chip], outs[a].at[chip], local.at[a])
            cp.start()
            copies.append(cp)
        for k in range(1, N_CHIPS):
            px, py = (1 - mx if k & 2 else mx), (1 - my if k & 1 else my)
            for a in range(n):
                s = (k - 1) * n + a
                cp = pltpu.make_async_remote_copy(ins[a].at[:, 2 * px + py], outs[a].at[chip], send.at[s], recv.at[s],
                                                  device_id=(px, py, mc), device_id_type=MESH)
                cp.start()
                copies.append(cp)
        for cp in copies:
            cp.wait()

    return pl.pallas_call(
        body, name="grad_scatter", in_specs=[ANY] * n, out_specs=[ANY] * n, out_shape=out_shape,
        scratch_shapes=[pltpu.SemaphoreType.DMA((3 * n,)), pltpu.SemaphoreType.DMA((3 * n,)), pltpu.SemaphoreType.DMA((n,))],
    )(*arrs)


def _weight_gather(arrs):
    n = len(arrs)

    def body(*refs):
        ins, outs, (ici_send, ici_recv, d2d_send, d2d_recv, local) = refs[:n], refs[n:2 * n], refs[2 * n:]
        mx, my, mc = _place()
        chip = 2 * mx + my
        peers = [(1 - mx if k & 2 else mx, 1 - my if k & 1 else my) for k in range(1, N_CHIPS)]
        fetched = []
        for k, (px, py) in enumerate(peers):
            for a in range(n):
                s = k * n + a
                cp = pltpu.make_async_remote_copy(ins[a].at[:, mc], outs[a].at[:, chip, mc], ici_send.at[s], ici_recv.at[s],
                                                  device_id=(px, py, mc), device_id_type=MESH)
                cp.start()
                fetched.append(cp)
        own = []
        for a in range(n):
            cp = pltpu.make_async_copy(ins[a], outs[a].at[:, chip], local.at[a])
            cp.start()
            own.append(cp)
        passed = []
        for k, (px, py) in enumerate(peers):
            for a in range(n):
                s = k * n + a
                fetched[s].wait_recv()
                half = outs[a].at[:, 2 * px + py, mc]
                cp = pltpu.make_async_remote_copy(half, half, d2d_send.at[s], d2d_recv.at[s],
                                                  device_id=(mx, my, 1 - mc), device_id_type=MESH)
                cp.start()
                passed.append(cp)
        for cp in fetched:
            cp.wait_send()
        for cp in passed + own:
            cp.wait()

    sems = pltpu.SemaphoreType.DMA((3 * n,))
    return pl.pallas_call(
        body, name="weight_gather", in_specs=[ANY] * n, out_specs=[ANY] * n,
        out_shape=[SDS((a.shape[0], N_CHIPS) + a.shape[1:], a.dtype) for a in arrs],
        scratch_shapes=[sems, sems, sems, sems, pltpu.SemaphoreType.DMA((n,))],
    )(*arrs)


def _sibling_send(arrs, halves, name):
    n = len(arrs)
    out_shape = [SDS(a.shape[:2] + a.shape[3:] if halves else a.shape, a.dtype) for a in arrs]

    def body(*refs):
        ins, outs, (send, recv) = refs[:n], refs[n:2 * n], refs[2 * n:]
        mx, my, mc = _place()
        copies = []
        for a in range(n):
            cp = pltpu.make_async_remote_copy(ins[a].at[:, :, 1 - mc] if halves else ins[a], outs[a], send.at[a], recv.at[a],
                                              device_id=(mx, my, 1 - mc), device_id_type=MESH)
            cp.start()
            copies.append(cp)
        for cp in copies:
            cp.wait()

    return pl.pallas_call(
        body, name=name, in_specs=[ANY] * n, out_specs=[ANY] * n, out_shape=out_shape,
        scratch_shapes=[pltpu.SemaphoreType.DMA((n,)), pltpu.SemaphoreType.DMA((n,))],
    )(*arrs)


def _add_cast(a, b, dtype):
    shape = a.shape
    a, b = _as_rows(a), _as_rows(b)
    R, C = a.shape
    tr = _row_tile(R, C, 3)

    def body(a_ref, b_ref, o_ref):
        o_ref[...] = (a_ref[...] + b_ref[...]).astype(dtype)

    out = pl.pallas_call(body, name="pair_sum", grid=(R // tr,), in_specs=[_rows(tr, C)] * 2, out_specs=_rows(tr, C),
                         out_shape=SDS((R, C), dtype), compiler_params=_params("parallel"))(a, b)
    return out.reshape(shape)


def _pack(arrs):
    flat = jnp.concatenate([a.reshape(-1).astype(F32) for a in arrs])
    pad = (-flat.shape[0]) % (16 * LANES)
    return jnp.pad(flat, (0, pad)).reshape(-1, LANES)


def _unpack(buf, shapes):
    flat = buf.reshape(buf.shape[:-2] + (-1,))
    out, off = [], 0
    for s in shapes:
        size = 1
        for d in s:
            size *= d
        out.append(flat[..., off:off + size].reshape(flat.shape[:-1] + tuple(s)))
        off += size
    return out


def _unshard(stacked, axis):
    moved = jnp.moveaxis(stacked, 0, axis)
    return moved.reshape(moved.shape[:axis] + (N_CHIPS * moved.shape[axis + 1],) + moved.shape[axis + 2:])


def _my_shard(full, axis, chip):
    size = full.shape[axis] // N_CHIPS
    return lax.dynamic_slice_in_dim(full, chip * size, size, axis)


def kernel(x, c, cond_w, cond_b, ada_w, ada_b, norm_g, ffn_w1, ffn_w3, ffn_w2, a_w_in, a_b_in, a_dw, a_dw_b, a_ln_g, a_ln_b, a_w_out, a_b_out, b_w_qkv, b_q_g, b_k_g, b_w_o, p_w, p_b, p_scale, loss_target, m_cond_w, m_cond_b, m_ada_w, m_ada_b, m_norm_g, m_ffn_w1, m_ffn_w3, m_ffn_w2, m_a_w_in, m_a_b_in, m_a_dw, m_a_dw_b, m_a_ln_g, m_a_ln_b, m_a_w_out, m_a_b_out, m_b_w_qkv, m_b_q_g, m_b_k_g, m_b_w_o, m_p_w, m_p_b, m_p_scale, v_cond_w, v_cond_b, v_ada_w, v_ada_b, v_norm_g, v_ffn_w1, v_ffn_w3, v_ffn_w2, v_a_w_in, v_a_b_in, v_a_dw, v_a_dw_b, v_a_ln_g, v_a_ln_b, v_a_w_out, v_a_b_out, v_b_w_qkv, v_b_q_g, v_b_k_g, v_b_w_o, v_p_w, v_p_b, v_p_scale):
    w_in = dict(cond_w=cond_w, cond_b=cond_b, ada_w=ada_w, ada_b=ada_b, norm_g=norm_g, ffn_w1=ffn_w1, ffn_w3=ffn_w3, ffn_w2=ffn_w2,
                a_w_in=a_w_in, a_b_in=a_b_in, a_dw=a_dw, a_dw_b=a_dw_b, a_ln_g=a_ln_g, a_ln_b=a_ln_b, a_w_out=a_w_out, a_b_out=a_b_out,
                b_w_qkv=b_w_qkv, b_q_g=b_q_g, b_k_g=b_k_g, b_w_o=b_w_o, p_w=p_w, p_b=p_b, p_scale=p_scale)
    m_in = dict(zip(WEIGHTS, (m_cond_w, m_cond_b, m_ada_w, m_ada_b, m_norm_g, m_ffn_w1, m_ffn_w3, m_ffn_w2, m_a_w_in, m_a_b_in, m_a_dw,
                              m_a_dw_b, m_a_ln_g, m_a_ln_b, m_a_w_out, m_a_b_out, m_b_w_qkv, m_b_q_g, m_b_k_g, m_b_w_o, m_p_w, m_p_b, m_p_scale)))
    v_in = dict(zip(WEIGHTS, (v_cond_w, v_cond_b, v_ada_w, v_ada_b, v_norm_g, v_ffn_w1, v_ffn_w3, v_ffn_w2, v_a_w_in, v_a_b_in, v_a_dw,
                              v_a_dw_b, v_a_ln_g, v_a_ln_b, v_a_w_out, v_a_b_out, v_b_w_qkv, v_b_q_g, v_b_k_g, v_b_w_o, v_p_w, v_p_b, v_p_scale)))
    x = x[0]
    target = loss_target[0]
    S, D = x.shape
    L = ada_w.shape[0]
    assert b_q_g.shape[-1] == HEAD_DIM and D % LANES == 0 and S % HALO == 0
    mx, my, mc = _place()
    chip = 2 * mx + my
    me = 2 * chip + mc

    big = [ffn_w1.astype(BF16), ffn_w3.astype(BF16), ffn_w2.astype(BF16), a_w_in.astype(BF16), a_w_out.astype(BF16),
           b_w_qkv.astype(BF16), b_w_o.astype(BF16)]
    sharded = [k for k, ax in SMALL.items() if ax is not None]
    shards = big + [_pack([w_in[k] for k in sharded])]
    by_half = [a.reshape((-1, 2, a.shape[-2] // 2, a.shape[-1])) for a in shards]
    gathered = [g.reshape(a.shape[:-2] + (N_CHIPS,) + a.shape[-2:]) for g, a in zip(_weight_gather(by_half), shards)]
    g_w1, g_w3, g_w2, g_ain, g_aout, g_qkv, g_wo, g_small = gathered
    full = {k: _unshard(a, SMALL[k]) for k, a in zip(sharded, _unpack(g_small, [w_in[k].shape for k in sharded]))}
    n_conv, n_pool = a_w_in.shape[0], p_w.shape[0]
    conv_dw = [jnp.pad(full['a_dw'][ia], ((0, 1), (0, 0))) for ia in range(n_conv)]
    pool_w = [full['p_w'][ic].astype(BF16) for ic in range(n_pool)]
    qk_scale = LOG2E * HEAD_DIM ** -0.5
    gq = jnp.tile(b_q_g[0], D // HEAD_DIM).reshape(1, D) * qk_scale
    gk = jnp.tile(b_k_g[0], D // HEAD_DIM).reshape(1, D)

    c_all = _allgather8(c)[:, 0, :]
    c_cols = _my_shard(c_all, 1, chip)
    pre, e = _cond_e(_allgather8(_cond_pre(c_cols, cond_w)), cond_b.reshape(1, D))
    nc = ada_w.shape[2]
    mod_c = _mod_cols(e, ada_w, _my_shard(ada_b, 1, chip).reshape(L, 1, nc))
    mod_all = _allgather8(mod_c.reshape(L * N_DEV, nc)).reshape(N_CHIPS, 2, L, N_DEV, nc)[:, 0]
    mod = jnp.moveaxis(lax.dynamic_index_in_dim(mod_all, me, axis=2, keepdims=False), 0, 1).reshape(L, 3, 3, D)
    shift, scale, gate = mod[:, :, 0], mod[:, :, 1], 1.0 + mod[:, :, 2]
    gains = full['norm_g']

    def mod_vec(i, k, gate_factor=1.0):
        return _vec(gains[i, k] * (1.0 + scale[i, k]), shift[i, k], gate_factor * gate[i, k])

    saved = []
    ia = ib = ic = 0
    for i in range(L):
        for k, half in ((0, 0), (1, None), (2, 1)):
            if half is not None:
                vec = mod_vec(i, k, 0.5)
                xo, a1, a3, y = _ffn_fwd(x, vec, g_w1, g_w3, g_w2, (i, half))
                saved.append(('ffn', i, k, half, x, vec, (a1, a3, y)))
            elif i % 3 == 0:
                vec = mod_vec(i, k)
                w_a = g_ain[ia]
                w_o = g_aout[ia].reshape(D, D)
                p, u = _in_fwd(x, vec, w_a, full['a_b_in'][ia].reshape(1, 2 * D), True, "conv_in_fwd")
                cvec = _vec(gate[i, k], full['a_dw_b'][ia], full['a_ln_g'][ia], full['a_ln_b'][ia], full['a_b_out'][ia])
                xo, v, y = _conv_mid_fwd(u, x, cvec, conv_dw[ia], w_o)
                saved.append(('conv', i, k, ia, x, vec, (p, u, v, y, cvec, w_a, w_o)))
                ia += 1
            elif i % 3 == 1:
                vec = mod_vec(i, k)
                w_q = g_qkv[ib]
                w_o = g_wo[ib].reshape(D, D)
                raw, = _in_fwd(x, vec, w_q, jnp.zeros((1, 3 * D), F32), False, "attn_in_fwd")
                qkv = _qknorm_fwd(raw, gq, gk)
                o, tot, start = _attn_fwd(qkv)
                xo, y = _out_fwd(x, o, vec, w_o)
                saved.append(('attn', i, k, ib, x, vec, (raw, qkv, o, (tot, start), y, w_q, w_o)))
                ib += 1
            else:
                vec = _vec(gains[i, k] * (1.0 + scale[i, k]), shift[i, k], gate[i, k], full['p_scale'][ic], full['p_b'][ic].reshape(D))
                xo = _pool_fwd(x, vec, pool_w[ic])
                saved.append(('pool', i, k, ic, x, vec, ()))
                ic += 1
            x = xo

    dx, sq = _loss_head(x, target)
    loss = lax.psum(0.5 / D * jnp.sum(sq[0]), ("x", "y", "c"))

    zeros_like_full = lambda k: jnp.zeros(full[k].shape, F32)
    g_full = {k: zeros_like_full(k) for k in sharded}
    g_full['b_q_g'] = jnp.zeros_like(b_q_g)
    g_full['b_k_g'] = jnp.zeros_like(b_k_g)
    dmod = jnp.zeros((L, 3, 3, D), F32)
    fc = ffn_w1.shape[-1]
    n_attn = b_w_qkv.shape[0]
    big_grads = {'ffn_w1': lax.empty((2 * L, N_CHIPS, fc, D), F32), 'ffn_w3': lax.empty((2 * L, N_CHIPS, fc, D), F32),
                 'ffn_w2': lax.empty((2 * L, N_CHIPS, fc, D), F32), 'a_w_in': lax.empty((n_conv, N_CHIPS, D, 2 * D // N_CHIPS), F32),
                 'a_w_out': lax.empty((n_conv, 1, D, D), F32), 'b_w_qkv': lax.empty((n_attn, N_CHIPS, D, 3 * D // N_CHIPS), F32),
                 'b_w_o': lax.empty((n_attn, 1, D, D), F32)}
    transposed = ('ffn_w1', 'ffn_w3')

    def wgrad(name, slot, a, b, a_mode, b_mode, nch):
        big_grads[name] = _wgrad(a, b, a_mode, b_mode, nch, name + "_grad", big_grads[name], slot)

    def put(name, idx, val):
        g_full[name] = g_full[name].at[idx].set(val.reshape(g_full[name][idx].shape))

    for kind, i, k, idx, xin, vec, res in reversed(saved):
        if kind == 'ffn':
            a1, a3, y = res
            dx, h, dy, u, da1, da3, sums = _ffn_bwd(xin, dx, a1, a3, y, vec, g_w1, g_w3, g_w2, (i, idx))
            wgrad('ffn_w1', 2 * i + idx, da1, h, 'lead', 'full', N_CHIPS)
            wgrad('ffn_w3', 2 * i + idx, da3, h, 'lead', 'full', N_CHIPS)
            wgrad('ffn_w2', 2 * i + idx, u, dy, 'lead', 'full', N_CHIPS)
            dgate = 0.5 * sums[2]
        elif kind == 'conv':
            p, u, v, y, cvec, w_a, w_o = res
            dv, q, dout, csums = _conv_mid_bwd(dx, y, v, cvec, w_o)
            wgrad('a_w_out', idx, q, dout, 'full', 'full', 1)
            du, gdw = _conv_transpose(dv, u, conv_dw[idx])
            dp, psums = _glu_bwd(du, p)
            dx, h, sums = _in_bwd(xin, dx, dp, vec, w_a, "conv_in_bwd")
            wgrad('a_w_in', idx, h, dp, 'full', 'col', N_CHIPS)
            dgate = csums[0]
            put('a_b_out', idx, csums[1])
            put('a_ln_g', idx, csums[2])
            put('a_ln_b', idx, csums[3])
            put('a_dw_b', idx, csums[4])
            put('a_dw', idx, gdw[:-1])
            put('a_b_in', idx, psums[0])
        elif kind == 'attn':
            raw, qkv, o, tot, y, w_q, w_o = res
            do, dout, osums = _out_bwd(dx, y, vec, w_o)
            wgrad('b_w_o', idx, o, dout, 'full', 'full', 1)
            dq, dk, dvv = _attn_bwd(qkv, do, *tot)
            draw, qsums = _qknorm_bwd(dq, dk, dvv, raw, gq, gk)
            dx, h, sums = _in_bwd(xin, dx, draw, vec, w_q, "attn_in_bwd")
            wgrad('b_w_qkv', idx, h, draw, 'full', 'col', N_CHIPS)
            dgate = osums[2]
            put('b_q_g', idx, qk_scale * jnp.sum(qsums[0].reshape(-1, HEAD_DIM), axis=0))
            put('b_k_g', idx, jnp.sum(qsums[1].reshape(-1, HEAD_DIM), axis=0))
        else:
            dx, gpw, sums = _pool_bwd(xin, dx, vec, pool_w[idx])
            dgate = sums[2]
            put('p_w', idx, gpw)
            put('p_scale', idx, sums[3])
            put('p_b', idx, sums[4])
        put('norm_g', (i, k), sums[0] * (1.0 + scale[i, k]))
        dmod = dmod.at[i, k].set(jnp.stack([sums[1], sums[0] * gains[i, k], dgate]))
    grad_x = dx[None]

    dmod_all = _allgather8(dmod.reshape(L, 9 * D))
    g_ada_b = _sum_lead(dmod_all)
    dmod_cols = jnp.moveaxis(_my_shard(dmod_all, 2, chip), 0, 1)
    g_ada_w, de_part = _mod_bwd(e, dmod_cols, ada_w)
    g_cond_w, g_cond_b = _cond_bwd(_allgather8(de_part), pre, c_cols)

    small_names = list(SMALL)
    g_full['cond_b'] = g_cond_b.reshape(D)
    g_full['ada_b'] = g_ada_b
    reduced = [k for k in small_names if k not in ('cond_b', 'ada_b')]
    red = _unpack(_sum_lead(_allgather8(_pack([g_full[k] for k in reduced]))), [g_full[k].shape for k in reduced])
    for k, a in zip(reduced, red):
        g_full[k] = a
    grads = {k: (g_full[k] if SMALL[k] is None else _my_shard(g_full[k], SMALL[k], chip)) for k in small_names}
    grads['cond_w'] = g_cond_w
    grads['ada_w'] = g_ada_w

    big_names = list(big_grads)
    by_half = []
    for name in big_names:
        g = big_grads[name]
        rows = g.shape[1] * g.shape[2] // N_CHIPS
        by_half.append(g.reshape(g.shape[0], N_CHIPS, 2, rows // 2, g.shape[3]))
    got = _sibling_send(by_half, True, "pair_send")
    pair = [_add_cast(lax.dynamic_index_in_dim(a, mc, axis=2, keepdims=False), b, BF16) for a, b in zip(by_half, got)]
    mine = [_sum_lead(a) for a in _grad_scatter(pair)]
    theirs = _sibling_send(mine, False, "pair_return")
    flip = lambda a: jnp.swapaxes(a, -1, -2)
    for name, a, b in zip(big_names, mine, theirs):
        both = jnp.where(mc == 0, jnp.stack([a, b], axis=1), jnp.stack([b, a], axis=1))
        shape = w_in[name].shape
        grads[name] = both.reshape(shape[:-2] + (shape[-1], shape[-2])) if name in transposed else both.reshape(shape)

    delta, new_m, new_v = {}, {}, {}
    packed = [_pack([d[k] for k in small_names]) for d in (w_in, grads, m_in, v_in)]
    shapes = [w_in[k].shape for k in small_names]
    for out, buf in zip((delta, new_m, new_v), _adamw(*packed)):
        out.update(zip(small_names, _unpack(buf, shapes)))
    for k in WEIGHTS:
        if k in transposed:
            delta[k], new_m[k], new_v[k] = (flip(a) for a in _adamw(flip(w_in[k]), grads[k], flip(m_in[k]), flip(v_in[k])))
            grads[k] = flip(grads[k])
        elif k not in SMALL:
            delta[k], new_m[k], new_v[k] = _adamw(w_in[k], grads[k], m_in[k], v_in[k])
    return (loss, grad_x, *[grads[k] for k in WEIGHTS], *[delta[k] for k in WEIGHTS], *[new_m[k] for k in WEIGHTS],
            *[new_v[k] for k in WEIGHTS])
```

```python
import functools

import jax
import jax.numpy as jnp
from jax import lax
from jax.experimental import pallas as pl
from jax.experimental.pallas import tpu as pltpu

F32 = jnp.float32
BF16 = jnp.bfloat16
SDS = jax.ShapeDtypeStruct
MESH = pl.DeviceIdType.MESH
ANY = pl.BlockSpec(memory_space=pl.ANY)

EPS = 1e-6
N_CHIPS = 4
N_DEV = 8
LANES = 128
HEAD_DIM = 64
VMEM_LIMIT_BYTES = 56 * 2**20
TOKEN_TILE = 512
WGRAD_TILE = 1024
ATTN_Q_TILE = 512
ATTN_K_TILE = 256
ATTN_DEAD_BITS = 160.0
LOG2E = 1.4426950408889634
LN2 = 0.6931471805599453
HALO = 32
CONV_ROWS, CONV_COLS = 32, 256
ADAM_TILE_BYTES = 1 << 20
POOL_LEVELS = (1, 2, 3, 4)

ADAM_LR, ADAM_B1, ADAM_B2, ADAM_EPS, ADAM_WD, ADAM_STEP = 0.001, 0.9, 0.999, 1e-08, 0.01, 10

WEIGHTS = ['cond_w', 'cond_b', 'ada_w', 'ada_b', 'norm_g', 'ffn_w1', 'ffn_w3', 'ffn_w2', 'a_w_in', 'a_b_in', 'a_dw',
           'a_dw_b', 'a_ln_g', 'a_ln_b', 'a_w_out', 'a_b_out', 'b_w_qkv', 'b_q_g', 'b_k_g', 'b_w_o', 'p_w', 'p_b', 'p_scale']
SMALL = {'norm_g': 2, 'a_b_in': 1, 'a_dw': 2, 'a_dw_b': 1, 'a_ln_g': 1, 'a_ln_b': 1, 'a_b_out': 1, 'p_w': 2, 'p_b': 2,
         'p_scale': 1, 'cond_b': None, 'ada_b': None, 'b_q_g': None, 'b_k_g': None}


def _tile(n, pref):
    return pref if n % pref == 0 else n


def _params(*sem):
    return pltpu.CompilerParams(dimension_semantics=sem, vmem_limit_bytes=VMEM_LIMIT_BYTES)


def _resident(shape):
    nd = len(shape)
    return pl.BlockSpec(shape, lambda *_: (0,) * nd, pipeline_mode=pl.Buffered(1))


def _rows(tm, width):
    return pl.BlockSpec((tm, width), lambda i: (i, 0))


def _acc_spec(rows, width):
    return pl.BlockSpec((rows, width), lambda i: (0, 0))


def _dot(a, b):
    return jnp.dot(a, b, preferred_element_type=F32)


def _dot_nt(a, b):
    return lax.dot_general(a, b, (((1,), (1,)), ((), ())), preferred_element_type=F32)


def _dot_tn(a, b):
    return lax.dot_general(a, b, (((0,), (0,)), ((), ())), preferred_element_type=F32)


def _hdot(a, b, dims=(((1,), (0,)), ((), ()))):
    return lax.dot_general(a, b, dims, preferred_element_type=F32, precision=lax.Precision.HIGHEST)


def _zero_at_first_step(*refs):
    @pl.when(pl.program_id(0) == 0)
    def _():
        for r in refs:
            r[...] = jnp.zeros_like(r)


def _add_rowsum(ref, row, t):
    ref[row:row + 1, :] += jnp.sum(t, axis=0, keepdims=True)


def _rms(x):
    r = lax.rsqrt(jnp.mean(x * x, axis=-1, keepdims=True) + EPS)
    return x * r, r


def _modulate_bwd(dh, n, r, gs, dxo, sums_ref):
    _add_rowsum(sums_ref, 0, dh * n)
    _add_rowsum(sums_ref, 1, dh)
    dn = dh * gs
    return dxo + r * (dn - n * jnp.mean(dn * n, axis=-1, keepdims=True))


def _silu_grad(a, sg):
    return sg * (1.0 + a * (1.0 - sg))


def _vec(*rows):
    d = rows[0].shape[-1]
    rows = [r.reshape(1, d).astype(F32) for r in rows]
    return jnp.concatenate(rows + [jnp.zeros((8 - len(rows), d), F32)], axis=0)


def _layer(w, at):
    rest = w.shape[len(at):]
    return pl.BlockSpec((None,) * len(at) + rest, lambda *_: tuple(at) + (0,) * len(rest), pipeline_mode=pl.Buffered(1))


def _ffn_fwd(x, vec, w1, w3, w2, at):
    S, D = x.shape
    nch, _, fc = w1.shape[-3:]
    tm = _tile(S, TOKEN_TILE)

    def body(x_ref, vec_ref, w1_ref, w3_ref, w2_ref, xo_ref, a1_ref, a3_ref, y_ref):
        x = x_ref[...]
        n, _ = _rms(x)
        h = (n * vec_ref[0:1, :] + vec_ref[1:2, :]).astype(BF16)
        acc = jnp.zeros((tm, D), F32)
        for j in range(nch):
            a1 = _dot(h, w1_ref[j]).astype(BF16)
            a3 = _dot(h, w3_ref[j]).astype(BF16)
            a1_ref[j] = a1
            a3_ref[j] = a3
            a1 = a1.astype(F32)
            u = a1 * jax.nn.sigmoid(a1) * a3.astype(F32)
            acc = acc + _dot(u.astype(BF16), w2_ref[j])
        y_ref[...] = acc.astype(BF16)
        xo_ref[...] = x + vec_ref[2:3, :] * acc

    chunked = pl.BlockSpec((nch, tm, fc), lambda i: (0, i, 0))
    return pl.pallas_call(
        body, name="ffn_fwd", grid=(S // tm,),
        in_specs=[_rows(tm, D), _resident((8, D)), _layer(w1, at), _layer(w3, at), _layer(w2, at)],
        out_specs=[_rows(tm, D), chunked, chunked, _rows(tm, D)],
        out_shape=[SDS((S, D), F32), SDS((nch, S, fc), BF16), SDS((nch, S, fc), BF16), SDS((S, D), BF16)],
        compiler_params=_params("parallel"),
    )(x, vec, w1, w3, w2)


def _ffn_bwd(x, dxo, a1, a3, y, vec, w1, w3, w2, at):
    S, D = x.shape
    nch, _, fc = w1.shape[-3:]
    tm = _tile(S, TOKEN_TILE // 2)

    def body(x_ref, dxo_ref, a1_ref, a3_ref, y_ref, vec_ref, w1_ref, w3_ref, w2_ref,
             dx_ref, h_ref, dy_ref, u_ref, da1_ref, da3_ref, sums_ref):
        _zero_at_first_step(sums_ref)
        x = x_ref[...]
        dxo = dxo_ref[...]
        gs = vec_ref[0:1, :]
        n, r = _rms(x)
        h_ref[...] = (n * gs + vec_ref[1:2, :]).astype(BF16)
        _add_rowsum(sums_ref, 2, dxo * y_ref[...].astype(F32))
        dy = (vec_ref[2:3, :] * dxo).astype(BF16)
        dy_ref[...] = dy
        dh = jnp.zeros((tm, D), F32)
        for j in range(nch):
            a1 = a1_ref[j].astype(F32)
            a3 = a3_ref[j].astype(F32)
            sg = jax.nn.sigmoid(a1)
            s = a1 * sg
            du = _dot_nt(dy, w2_ref[j])
            da1 = (du * a3 * _silu_grad(a1, sg)).astype(BF16)
            da3 = (du * s).astype(BF16)
            u_ref[j] = (s * a3).astype(BF16)
            da1_ref[j] = da1
            da3_ref[j] = da3
            dh = dh + _dot_nt(da1, w1_ref[j]) + _dot_nt(da3, w3_ref[j])
        dx_ref[...] = _modulate_bwd(dh, n, r, gs, dxo, sums_ref)

    chunked = pl.BlockSpec((nch, tm, fc), lambda i: (0, i, 0))
    return pl.pallas_call(
        body, name="ffn_bwd", grid=(S // tm,),
        in_specs=[_rows(tm, D), _rows(tm, D), chunked, chunked, _rows(tm, D), _resident((8, D)),
                  _layer(w1, at), _layer(w3, at), _layer(w2, at)],
        out_specs=[_rows(tm, D), _rows(tm, D), _rows(tm, D), chunked, chunked, chunked, _acc_spec(8, D)],
        out_shape=[SDS((S, D), F32), SDS((S, D), BF16), SDS((S, D), BF16), SDS((nch, S, fc), BF16),
                   SDS((nch, S, fc), BF16), SDS((nch, S, fc), BF16), SDS((8, D), F32)],
        compiler_params=_params("arbitrary"),
    )(x, dxo, a1, a3, y, vec, w1, w3, w2)


def _wgrad(a, b, a_mode, b_mode, nch, name, acc, slot):
    S = a.shape[-2]
    M = a.shape[-1]
    N = b.shape[-1] // (nch if b_mode == 'col' else 1)
    assert acc.shape[1:] == (nch, M, N)
    ts = _tile(S, WGRAD_TILE)

    def spec(mode, width):
        if mode == 'full':
            return pl.BlockSpec((ts, width), lambda j, s: (s, 0))
        if mode == 'lead':
            return pl.BlockSpec((None, ts, width), lambda j, s: (j, s, 0))
        return pl.BlockSpec((ts, width), lambda j, s: (s, j))

    def body(a_ref, b_ref, acc_ref, o_ref):
        @pl.when(pl.program_id(1) == 0)
        def _():
            o_ref[...] = jnp.zeros_like(o_ref)
        o_ref[...] += _dot_tn(a_ref[...], b_ref[...])

    return pl.pallas_call(
        body, name=name, grid=(nch, S // ts), in_specs=[spec(a_mode, M), spec(b_mode, N), ANY],
        out_specs=pl.BlockSpec((None, None, M, N), lambda j, s: (slot, j, 0, 0)), out_shape=SDS(acc.shape, F32),
        input_output_aliases={2: 0}, compiler_params=_params("parallel", "arbitrary"),
    )(a, b, acc)


def _in_fwd(x, vec, w, bias, glu, name):
    S, D = x.shape
    nch, _, nc = w.shape
    N = nch * nc
    tm = _tile(S, TOKEN_TILE)

    def body(x_ref, vec_ref, w_ref, b_ref, p_ref, *u_ref):
        n, _ = _rms(x_ref[...])
        h = (n * vec_ref[0:1, :] + vec_ref[1:2, :]).astype(BF16)
        for j in range(nch):
            cols = slice(j * nc, (j + 1) * nc)
            p_ref[:, cols] = (_dot(h, w_ref[j]) + b_ref[:, cols]).astype(p_ref.dtype)
        if glu:
            half = N // 2
            u_ref[0][...] = p_ref[:, :half].astype(F32) * jax.nn.sigmoid(p_ref[:, half:].astype(F32))

    out_specs = [_rows(tm, N)] + ([_rows(tm, N // 2)] if glu else [])
    out_shape = [SDS((S, N), BF16 if glu else F32)] + ([SDS((S, N // 2), F32)] if glu else [])
    return pl.pallas_call(
        body, name=name, grid=(S // tm,),
        in_specs=[_rows(tm, D), _resident((8, D)), _resident(w.shape), _resident((1, N))],
        out_specs=out_specs, out_shape=out_shape, compiler_params=_params("parallel"),
    )(x, vec, w, bias)


def _in_bwd(x, dxo, dp, vec, w, name):
    S, D = x.shape
    nch, _, nc = w.shape
    tm = _tile(S, TOKEN_TILE)

    def body(x_ref, dxo_ref, dp_ref, vec_ref, w_ref, dx_ref, h_ref, sums_ref):
        _zero_at_first_step(sums_ref)
        gs = vec_ref[0:1, :]
        n, r = _rms(x_ref[...])
        h_ref[...] = (n * gs + vec_ref[1:2, :]).astype(BF16)
        dh = jnp.zeros((tm, D), F32)
        for j in range(nch):
            dh = dh + _dot_nt(dp_ref[:, j * nc:(j + 1) * nc], w_ref[j])
        dx_ref[...] = _modulate_bwd(dh, n, r, gs, dxo_ref[...], sums_ref)

    return pl.pallas_call(
        body, name=name, grid=(S // tm,),
        in_specs=[_rows(tm, D), _rows(tm, D), _rows(tm, nch * nc), _resident((8, D)), _resident(w.shape)],
        out_specs=[_rows(tm, D), _rows(tm, D), _acc_spec(8, D)],
        out_shape=[SDS((S, D), F32), SDS((S, D), BF16), SDS((8, D), F32)],
        compiler_params=_params("arbitrary"),
    )(x, dxo, dp, vec, w)


def _out_fwd(x, t, vec, w):
    S, D = x.shape
    tm = _tile(S, TOKEN_TILE)

    def body(x_ref, t_ref, vec_ref, w_ref, xo_ref, y_ref):
        y = _dot(t_ref[...], w_ref[...])
        y_ref[...] = y.astype(BF16)
        xo_ref[...] = x_ref[...] + vec_ref[2:3, :] * y

    return pl.pallas_call(
        body, name="attn_out_fwd", grid=(S // tm,),
        in_specs=[_rows(tm, D), _rows(tm, t.shape[1]), _resident((8, D)), _resident(w.shape)],
        out_specs=[_rows(tm, D), _rows(tm, D)], out_shape=[SDS((S, D), F32), SDS((S, D), BF16)],
        compiler_params=_params("parallel"),
    )(x, t, vec, w)


def _out_bwd(dxo, y, vec, w):
    S, D = dxo.shape
    K = w.shape[0]
    tm = _tile(S, TOKEN_TILE)

    def body(dxo_ref, y_ref, vec_ref, w_ref, dt_ref, dout_ref, sums_ref):
        _zero_at_first_step(sums_ref)
        dxo = dxo_ref[...]
        _add_rowsum(sums_ref, 2, dxo * y_ref[...].astype(F32))
        dout = (vec_ref[2:3, :] * dxo).astype(BF16)
        dout_ref[...] = dout
        dt_ref[...] = _dot_nt(dout, w_ref[...]).astype(BF16)

    return pl.pallas_call(
        body, name="attn_out_bwd", grid=(S // tm,),
        in_specs=[_rows(tm, D), _rows(tm, D), _resident((8, D)), _resident(w.shape)],
        out_specs=[_rows(tm, K), _rows(tm, D), _acc_spec(8, D)],
        out_shape=[SDS((S, K), BF16), SDS((S, D), BF16), SDS((8, D), F32)],
        compiler_params=_params("arbitrary"),
    )(dxo, y, vec, w)


def _prev_halo(tm, width):
    return pl.BlockSpec((HALO, width), lambda i: (jnp.maximum(i * (tm // HALO) - 1, 0), 0))


def _next_halo(tm, width, n_rows):
    last = n_rows // HALO - 1
    return pl.BlockSpec((HALO, width), lambda i: (jnp.minimum((i + 1) * (tm // HALO), last), 0))


def _layer_norm(v, g, b):
    mu = jnp.mean(v, axis=-1, keepdims=True)
    vc = v - mu
    rstd = lax.rsqrt(jnp.mean(vc * vc, axis=-1, keepdims=True) + EPS)
    vh = vc * rstd
    return vh * g + b, vh, rstd


def _fill_shifts(ext, sh, n):
    ext[n:, :] = jnp.zeros((8, ext.shape[1]), F32)
    for b in range(1, 8):
        sh[b - 1] = ext[pl.ds(b, n), :]


def _shifted(ext, sh, off, r0, rows, cols):
    b = off % 8
    base = r0 + off - b
    return ext[pl.ds(base, rows), cols] if b == 0 else sh[b - 1, pl.ds(base, rows), cols]


def _conv_mid_fwd(u, x, vec, dw, w_out):
    S, D = x.shape
    taps = dw.shape[0] - 1
    tm = _tile(S, TOKEN_TILE // 2)

    def body(u_ref, uh_ref, x_ref, vec_ref, dw_ref, w_ref, xo_ref, v_ref, y_ref, ext, sh):
        n = HALO + tm
        ext[0:HALO, :] = jnp.where(pl.program_id(0) > 0, uh_ref[...], 0.0)
        ext[HALO:n, :] = u_ref[...]
        _fill_shifts(ext, sh, n)
        cb = min(CONV_COLS, D)
        for c0 in range(0, D, cb):
            cols = slice(c0, c0 + cb)
            w = [dw_ref[k:k + 1, cols] for k in range(taps)]
            for r0 in range(0, tm, CONV_ROWS):
                acc = jnp.zeros((CONV_ROWS, cb), F32) + vec_ref[1:2, cols]
                for k in range(taps):
                    acc = acc + w[k] * _shifted(ext, sh, HALO - (taps - 1) + k, r0, CONV_ROWS, cols)
                v_ref[r0:r0 + CONV_ROWS, cols] = acc
        v = v_ref[...]
        l, _, _ = _layer_norm(v, vec_ref[2:3, :], vec_ref[3:4, :])
        q = (l * jax.nn.sigmoid(l)).astype(BF16)
        y = _dot(q, w_ref[...]) + vec_ref[4:5, :]
        y_ref[...] = y.astype(BF16)
        xo_ref[...] = x_ref[...] + vec_ref[0:1, :] * y

    return pl.pallas_call(
        body, name="conv_mid_fwd", grid=(S // tm,),
        in_specs=[_rows(tm, D), _prev_halo(tm, D), _rows(tm, D), _resident((8, D)), _resident(dw.shape), _resident(w_out.shape)],
        out_specs=[_rows(tm, D), _rows(tm, D), _rows(tm, D)],
        out_shape=[SDS((S, D), F32), SDS((S, D), F32), SDS((S, D), BF16)],
        scratch_shapes=[pltpu.VMEM((HALO + tm + 8, D), F32), pltpu.VMEM((7, HALO + tm, D), F32)],
        compiler_params=_params("parallel"),
    )(u, u, x, vec, dw, w_out)


def _conv_mid_bwd(dxo, y, v, vec, w_out):
    S, D = dxo.shape
    tm = _tile(S, TOKEN_TILE)

    def body(dxo_ref, y_ref, v_ref, vec_ref, w_ref, dv_ref, q_ref, dout_ref, sums_ref):
        _zero_at_first_step(sums_ref)
        dxo = dxo_ref[...]
        _add_rowsum(sums_ref, 0, dxo * y_ref[...].astype(F32))
        dout = vec_ref[0:1, :] * dxo
        _add_rowsum(sums_ref, 1, dout)
        dout = dout.astype(BF16)
        dout_ref[...] = dout
        ln_g = vec_ref[2:3, :]
        l, vh, rstd = _layer_norm(v_ref[...], ln_g, vec_ref[3:4, :])
        sg = jax.nn.sigmoid(l)
        q_ref[...] = (l * sg).astype(BF16)
        dl = _dot_nt(dout, w_ref[...]) * _silu_grad(l, sg)
        _add_rowsum(sums_ref, 2, dl * vh)
        _add_rowsum(sums_ref, 3, dl)
        dvh = dl * ln_g
        dv = rstd * (dvh - jnp.mean(dvh, axis=-1, keepdims=True) - vh * jnp.mean(dvh * vh, axis=-1, keepdims=True))
        _add_rowsum(sums_ref, 4, dv)
        dv_ref[...] = dv

    return pl.pallas_call(
        body, name="conv_mid_bwd", grid=(S // tm,),
        in_specs=[_rows(tm, D), _rows(tm, D), _rows(tm, D), _resident((8, D)), _resident(w_out.shape)],
        out_specs=[_rows(tm, D), _rows(tm, D), _rows(tm, D), _acc_spec(8, D)],
        out_shape=[SDS((S, D), F32), SDS((S, D), BF16), SDS((S, D), BF16), SDS((8, D), F32)],
        compiler_params=_params("arbitrary"),
    )(dxo, y, v, vec, w_out)


def _conv_transpose(dv, u, dw):
    S, D = dv.shape
    taps = dw.shape[0] - 1
    tm = _tile(S, TOKEN_TILE // 2)

    def body(dv_ref, dvn_ref, u_ref, uh_ref, dw_ref, du_ref, gdw_ref, extv, shv, extu, shu):
        _zero_at_first_step(gdw_ref)
        i = pl.program_id(0)
        n = HALO + tm
        extv[0:tm, :] = dv_ref[...]
        extv[tm:n, :] = jnp.where(i < pl.num_programs(0) - 1, dvn_ref[...], 0.0)
        _fill_shifts(extv, shv, n)
        extu[0:HALO, :] = jnp.where(i > 0, uh_ref[...], 0.0)
        extu[HALO:n, :] = u_ref[...]
        _fill_shifts(extu, shu, n)
        cb = min(CONV_COLS, D)
        for c0 in range(0, D, cb):
            cols = slice(c0, c0 + cb)
            w = [dw_ref[k:k + 1, cols] for k in range(taps)]
            for r0 in range(0, tm, CONV_ROWS):
                acc = jnp.zeros((CONV_ROWS, cb), F32)
                for k in range(taps):
                    acc = acc + w[k] * _shifted(extv, shv, taps - 1 - k, r0, CONV_ROWS, cols)
                du_ref[r0:r0 + CONV_ROWS, cols] = acc
        for c0 in range(0, D, LANES):
            cols = slice(c0, c0 + LANES)
            accs = [jnp.zeros((8, LANES), F32) for _ in range(taps)]
            for r0 in range(0, tm, CONV_ROWS):
                dvb = extv[r0:r0 + CONV_ROWS, cols]
                for k in range(taps):
                    p = dvb * _shifted(extu, shu, HALO - (taps - 1) + k, r0, CONV_ROWS, cols)
                    for s in range(0, CONV_ROWS, 8):
                        accs[k] = accs[k] + p[s:s + 8]
            for k in range(taps):
                gdw_ref[k:k + 1, cols] += jnp.sum(accs[k], axis=0, keepdims=True)

    return pl.pallas_call(
        body, name="conv_transpose", grid=(S // tm,),
        in_specs=[_rows(tm, D), _next_halo(tm, D, S), _rows(tm, D), _prev_halo(tm, D), _resident(dw.shape)],
        out_specs=[_rows(tm, D), _acc_spec(dw.shape[0], D)],
        out_shape=[SDS((S, D), F32), SDS(dw.shape, F32)],
        scratch_shapes=[pltpu.VMEM((HALO + tm + 8, D), F32), pltpu.VMEM((7, HALO + tm, D), F32),
                        pltpu.VMEM((HALO + tm + 8, D), F32), pltpu.VMEM((7, HALO + tm, D), F32)],
        compiler_params=_params("arbitrary"),
    )(dv, dv, u, u, dw)


def _glu_bwd(du, p):
    S, D = du.shape
    tm = _tile(S, TOKEN_TILE)

    def body(du_ref, p_ref, dp_ref, sums_ref):
        _zero_at_first_step(sums_ref)
        du = du_ref[...]
        a = p_ref[:, :D].astype(F32)
        sb = jax.nn.sigmoid(p_ref[:, D:].astype(F32))
        da = du * sb
        db = du * a * sb * (1.0 - sb)
        dp_ref[:, :D] = da.astype(BF16)
        dp_ref[:, D:] = db.astype(BF16)
        sums_ref[0:1, :D] += jnp.sum(da, axis=0, keepdims=True)
        sums_ref[0:1, D:] += jnp.sum(db, axis=0, keepdims=True)

    return pl.pallas_call(
        body, name="glu_bwd", grid=(S // tm,), in_specs=[_rows(tm, D), _rows(tm, 2 * D)],
        out_specs=[_rows(tm, 2 * D), _acc_spec(8, 2 * D)], out_shape=[SDS((S, 2 * D), BF16), SDS((8, 2 * D), F32)],
        compiler_params=_params("arbitrary"),
    )(du, p)


def _head_mean(t, bd):
    hi = t.astype(BF16)
    lo = (t - hi.astype(F32)).astype(BF16)
    return (_dot(hi, bd) + _dot(lo, bd)) * (1.0 / HEAD_DIM)


def _head_blocks():
    r = lax.broadcasted_iota(jnp.int32, (LANES, LANES), 0) // HEAD_DIM
    c = lax.broadcasted_iota(jnp.int32, (LANES, LANES), 1) // HEAD_DIM
    return (r == c).astype(BF16)


def _qknorm_fwd(raw, gq, gk):
    S, D3 = raw.shape
    D = D3 // 3
    tm = _tile(S, TOKEN_TILE)

    def body(raw_ref, gq_ref, gk_ref, o_ref):
        bd = _head_blocks()
        for off, g_ref in ((0, gq_ref), (D, gk_ref)):
            for c in range(D // LANES):
                cols = slice(off + c * LANES, off + (c + 1) * LANES)
                xs = raw_ref[:, cols]
                r = lax.rsqrt(_head_mean(xs * xs, bd) + EPS)
                o_ref[:, cols] = (xs * r * g_ref[:, c * LANES:(c + 1) * LANES]).astype(BF16)
        o_ref[:, 2 * D:] = raw_ref[:, 2 * D:].astype(BF16)

    return pl.pallas_call(
        body, name="qknorm_fwd", grid=(S // tm,), in_specs=[_rows(tm, D3), _resident((1, D)), _resident((1, D))],
        out_specs=_rows(tm, D3), out_shape=SDS((S, D3), BF16), compiler_params=_params("parallel"),
    )(raw, gq, gk)


def _qknorm_bwd(dq, dk, dv, raw, gq, gk):
    S, D3 = raw.shape
    D = D3 // 3
    nb = D // LANES
    tm = _tile(S, TOKEN_TILE)

    def body(dq_ref, dk_ref, dv_ref, raw_ref, gq_ref, gk_ref, o_ref, sums_ref):
        _zero_at_first_step(sums_ref)
        bd = _head_blocks()
        for row, (off, g_ref, d_ref) in enumerate(((0, gq_ref, dq_ref), (D, gk_ref, dk_ref))):
            for c in range(nb):
                lanes = slice(c * LANES, (c + 1) * LANES)
                cols = slice(off + c * LANES, off + (c + 1) * LANES)
                xs = raw_ref[:, cols]
                r = lax.rsqrt(_head_mean(xs * xs, bd) + EPS)
                n = xs * r
                dhat = d_ref[c]
                sums_ref[row:row + 1, lanes] += jnp.sum(dhat * n, axis=0, keepdims=True)
                dn = dhat * g_ref[:, lanes]
                o_ref[:, cols] = (r * (dn - n * _head_mean(dn * n, bd))).astype(BF16)
        for c in range(nb):
            o_ref[:, 2 * D + c * LANES:2 * D + (c + 1) * LANES] = dv_ref[c].astype(BF16)

    tiles = pl.BlockSpec((nb, tm, LANES), lambda i: (0, i, 0))
    return pl.pallas_call(
        body, name="qknorm_bwd", grid=(S // tm,),
        in_specs=[tiles, tiles, tiles, _rows(tm, D3), _resident((1, D)), _resident((1, D))],
        out_specs=[_rows(tm, D3), _acc_spec(8, D)], out_shape=[SDS((S, D3), BF16), SDS((8, D), F32)],
        compiler_params=_params("arbitrary"),
    )(dq, dk, dv, raw, gq, gk)


def _softplus2(z):
    return jnp.maximum(z, jnp.log2(1.0 + jnp.exp2(jnp.minimum(z, 30.0))))


def _attn_tiles(S):
    tq = _tile(S, ATTN_Q_TILE)
    tk = _tile(tq, ATTN_K_TILE)
    return tq, tk


def _attn_consts(tq, tk):
    lane = lax.broadcasted_iota(jnp.int32, (1, LANES), 1)
    first = lane < HEAD_DIM
    r = lax.broadcasted_iota(jnp.int32, (tq, tk), 0)
    c = lax.broadcasted_iota(jnp.int32, (tq, tk), 1)
    kr = lax.broadcasted_iota(jnp.int32, (tk, tk), 0)
    kc = lax.broadcasted_iota(jnp.int32, (tk, tk), 1)
    tri = c < r
    later = (kr > kc).astype(BF16)
    upto = (kr <= kc).astype(BF16)
    return first, tri, later, upto


def _row_sums(t):
    return jnp.broadcast_to(jnp.sum(t, axis=1, keepdims=True), (t.shape[0], LANES))


def _over_keys(per_row, tk):
    return per_row[:, :tk] if tk <= LANES else jnp.concatenate([per_row] * (tk // LANES), axis=1)


def _from(x, lo, axis=0):
    return x if lo == 0 else lax.slice_in_dim(x, lo, x.shape[axis], axis=axis)


def _add_from(acc, lo, part):
    return acc + (part if lo == 0 else jnp.concatenate([jnp.zeros((lo,) + part.shape[1:], part.dtype), part], axis=0))


def _attn_specs(S, tq, nb):
    q_spec = pl.BlockSpec((tq, LANES), lambda hp, i: (i, hp))
    k_spec = pl.BlockSpec((S, LANES), lambda hp, i: (0, nb + hp), pipeline_mode=pl.Buffered(1))
    v_spec = pl.BlockSpec((S, LANES), lambda hp, i: (0, 2 * nb + hp), pipeline_mode=pl.Buffered(1))
    return q_spec, k_spec, v_spec


def _attn_fwd(qkv):
    S, D3 = qkv.shape
    D = D3 // 3
    nb = D // LANES
    tq, tk = _attn_tiles(S)
    nsub = tq // tk

    def body(q_ref, k_ref, v_ref, o_ref, tot_ref, start_ref):
        hp = pl.program_id(0)
        i = pl.program_id(1)
        first, tri, later, _ = _attn_consts(tq, tk)
        q = q_ref[...]
        qs = (jnp.where(first, q, jnp.zeros_like(q)), jnp.where(first, jnp.zeros_like(q), q))

        def tiles(t0, n, carry, diagonal):
            out = []
            for h in range(2):
                o, c = carry[h]
                for a in reversed(range(n)):
                    lo = a * tk if diagonal else 0
                    mask = tri[:tq - lo] if diagonal else None
                    rows = pl.ds(pl.multiple_of((t0 + a) * tk, tk), tk)
                    kb = k_ref[rows, :]
                    z = _dot_nt(_from(qs[h], lo), kb)
                    sp = _softplus2(z)
                    logsig = z - sp
                    if diagonal:
                        sp = jnp.where(mask, sp, 0.0)
                    av = jnp.exp2(logsig - _dot(sp.astype(BF16), later) - _over_keys(_from(c, lo), tk))
                    if diagonal:
                        av = jnp.where(mask, av, 0.0)
                    o = _add_from(o, lo, _dot(av.astype(BF16), v_ref[rows, :]))
                    c = _add_from(c, lo, _row_sums(sp))
                out.append((o, c))
            return tuple(out)

        def live(carry):
            return jnp.minimum(jnp.min(carry[0][1]), jnp.min(carry[1][1])) < ATTN_DEAD_BITS

        carry = tuple((jnp.zeros((tq, LANES), F32), jnp.zeros((tq, LANES), F32)) for _ in range(2))
        carry = tiles(i * nsub, nsub, carry, True)
        band_floor = jnp.maximum((i - 1) * nsub, 0)
        t, carry = lax.while_loop(lambda st: (st[0] >= band_floor) & (st[0] >= 0) & live(st[1]),
                                  lambda st: (st[0] - 1, tiles(st[0], 1, st[1], False)), (i * nsub - 1, carry))
        t, carry = lax.while_loop(lambda st: (st[0] >= nsub - 1) & live(st[1]),
                                  lambda st: (st[0] - nsub, tiles(st[0] - (nsub - 1), nsub, st[1], False)), (t, carry))
        (o_a, c_a), (o_b, c_b) = carry
        o_ref[...] = jnp.where(first, o_a, o_b).astype(BF16)
        tot_ref[...] = jnp.where(first, c_a, c_b)
        start_ref[hp, i] = t + 1

    q_spec, k_spec, v_spec = _attn_specs(S, tq, nb)
    return pl.pallas_call(
        body, name="attn_fwd", grid=(nb, S // tq), in_specs=[q_spec, k_spec, v_spec],
        out_specs=[q_spec, q_spec, pl.BlockSpec(memory_space=pltpu.SMEM)],
        out_shape=[SDS((S, D), BF16), SDS((S, D), F32), SDS((nb, S // tq), jnp.int32)],
        compiler_params=_params("arbitrary", "arbitrary"),
    )(qkv, qkv, qkv)


def _attn_bwd(qkv, do, tot, start):
    S, D3 = qkv.shape
    D = D3 // 3
    nb = D // LANES
    tq, tk = _attn_tiles(S)
    nsub = tq // tk
    nkb = S // tk

    def body(start_ref, q_ref, k_ref, v_ref, do_ref, tot_ref, dq_ref, dk_hbm, dv_hbm, dkt_acc, dvt_acc, stage, sem):
        hp = pl.program_id(0)
        i = pl.program_id(1)

        @pl.when(i == 0)
        def _():
            dkt_acc[...] = jnp.zeros_like(dkt_acc)
            dvt_acc[...] = jnp.zeros_like(dvt_acc)

        first, tri, later, upto = _attn_consts(tq, tk)
        q = q_ref[...]
        do = do_ref[...]
        zero = jnp.zeros_like(q)
        qs = (jnp.where(first, q, zero), jnp.where(first, zero, q))
        dos = (jnp.where(first, do, zero), jnp.where(first, zero, do))
        qt = q.astype(F32).T.astype(BF16)
        dot_ = do.astype(F32).T.astype(BF16)
        qts = (qt[:HEAD_DIM], qt[HEAD_DIM:])
        dots = (dot_[:HEAD_DIM], dot_[HEAD_DIM:])
        tots = tuple(jnp.broadcast_to(tot_ref[:, l:l + 1], (tq, LANES)) for l in (0, HEAD_DIM))

        def tiles(t0, n, carry, diagonal):
            carry = list(carry)
            for a in range(n):
                lo = a * tk if diagonal else 0
                mask = tri[:tq - lo] if diagonal else None
                j = t0 + a
                rows = pl.ds(pl.multiple_of(j * tk, tk), tk)
                kb = k_ref[rows, :]
                vb = v_ref[rows, :]
                dkts, dvts = [], []
                for h in range(2):
                    dq, cum, pre = carry[h]
                    z = _dot_nt(_from(qs[h], lo), kb)
                    sp = _softplus2(z)
                    logsig = z - sp
                    if diagonal:
                        sp = jnp.where(mask, sp, 0.0)
                    cum = _add_from(cum, lo, _row_sums(sp))
                    av = jnp.exp2(logsig - _dot(sp.astype(BF16), later) - _over_keys(_from(tots[h], lo) - _from(cum, lo), tk))
                    if diagonal:
                        av = jnp.where(mask, av, 0.0)
                    g = _dot_nt(_from(dos[h], lo), vb) * av
                    dz = g - jnp.exp2(logsig) * (_over_keys(_from(pre, lo), tk) + _dot(g.astype(BF16), upto))
                    if diagonal:
                        dz = jnp.where(mask, dz, 0.0)
                    dz = dz.astype(BF16)
                    dkts.append(_dot(_from(qts[h], lo, 1), dz))
                    dvts.append(_dot(_from(dots[h], lo, 1), av.astype(BF16)))
                    carry[h] = (_add_from(dq, lo, _dot(dz, kb)), cum, _add_from(pre, lo, _row_sums(g)))
                dkt_acc[j] += jnp.concatenate(dkts, axis=0)
                dvt_acc[j] += jnp.concatenate(dvts, axis=0)
            return tuple(carry)

        carry = tuple((jnp.zeros((tq, LANES), F32),) * 3 for _ in range(2))
        t0 = start_ref[hp, i]
        odd = lax.rem(i * nsub - t0, nsub)
        carry = lax.fori_loop(0, odd, lambda s, cr: tiles(t0 + s, 1, cr, False), carry)
        carry = lax.fori_loop(0, (i * nsub - t0) // nsub, lambda b, cr: tiles(t0 + odd + b * nsub, nsub, cr, False), carry)
        carry = tiles(i * nsub, nsub, carry, True)
        dq_ref[...] = jnp.where(first, carry[0][0], carry[1][0]) * LN2

        @pl.when(i == pl.num_programs(1) - 1)
        def _():
            def flush(j, _):
                rows = pl.ds(pl.multiple_of(j * tk, tk), tk)
                stage[0] = dkt_acc[j].T * LN2
                stage[1] = dvt_acc[j].T
                ck = pltpu.make_async_copy(stage.at[0], dk_hbm.at[hp, rows], sem.at[0])
                cv = pltpu.make_async_copy(stage.at[1], dv_hbm.at[hp, rows], sem.at[1])
                ck.start()
                cv.start()
                ck.wait()
                cv.wait()
                return 0
            lax.fori_loop(0, nkb, flush, 0)

    q_spec, k_spec, v_spec = _attn_specs(S, tq, nb)
    slab = SDS((nb, S, LANES), F32)
    return pl.pallas_call(
        body, name="attn_bwd", grid=(nb, S // tq),
        in_specs=[pl.BlockSpec(memory_space=pltpu.SMEM), q_spec, k_spec, v_spec, q_spec, q_spec],
        out_specs=[pl.BlockSpec((None, tq, LANES), lambda hp, i: (hp, i, 0)), ANY, ANY], out_shape=[slab, slab, slab],
        scratch_shapes=[pltpu.VMEM((nkb, LANES, tk), F32), pltpu.VMEM((nkb, LANES, tk), F32), pltpu.VMEM((2, tk, LANES), F32),
                        pltpu.SemaphoreType.DMA((2,))],
        compiler_params=_params("arbitrary", "arbitrary"),
    )(start, qkv, qkv, qkv, do, tot)


def _trail_sum(ext, bufs, cols, levels, n):
    def src(lo, size):
        return ext[pl.ds(lo, size), cols]
    for l in range(levels):
        lo = 8 * (l + 1)
        dst = bufs[l % 2]
        dst[lo:, :] = src(lo, n - lo) + src(lo - (1 << l), n - lo)
        def src(lo_, size, d=dst):
            return d[pl.ds(lo_, size), :]
    return src(HALO, n - HALO)


def _lead_sum(ext, bufs, cols, levels, n):
    def src(lo, size):
        return ext[pl.ds(lo, size), cols]
    for l in range(levels):
        hi = n - 8 * (l + 1)
        dst = bufs[l % 2]
        dst[0:hi, :] = src(0, hi) + src(1 << l, hi)
        def src(lo_, size, d=dst):
            return d[pl.ds(lo_, size), :]
    return src(0, n - HALO)


def _pool_diffs(x_ref, xh_ref, vec_ref, ext, bufs, tm, D):
    i = pl.program_id(0)
    gs, shift = vec_ref[0:1, :], vec_ref[1:2, :]
    n, r = _rms(x_ref[...])
    nh, _ = _rms(xh_ref[...])
    ext[0:HALO, :] = jnp.where(i > 0, nh * gs + shift, 0.0)
    ext[HALO:, :] = n * gs + shift
    t = i * tm + lax.broadcasted_iota(jnp.int32, (tm, 1), 0)
    dg = D // len(POOL_LEVELS)
    out = []
    for g, lv in enumerate(POOL_LEVELS):
        cols = slice(g * dg, (g + 1) * dg)
        inv = 1.0 / jnp.minimum(t + 1, 1 << lv).astype(F32)
        out.append((_trail_sum(ext, bufs, cols, lv, HALO + tm) * inv - ext[HALO:, cols], inv))
    return out, n, r


def _pool_fwd(x, vec, pw):
    S, D = x.shape
    ng, dg, _ = pw.shape
    tm = _tile(S, TOKEN_TILE)

    def body(x_ref, xh_ref, vec_ref, pw_ref, xo_ref, ext, buf_a, buf_b):
        diffs, _, _ = _pool_diffs(x_ref, xh_ref, vec_ref, ext, (buf_a, buf_b), tm, D)
        for g, (d, _) in enumerate(diffs):
            cols = slice(g * dg, (g + 1) * dg)
            y = (_dot(d.astype(BF16), pw_ref[g]) + vec_ref[4:5, cols]) * vec_ref[3:4, cols]
            xo_ref[:, cols] = x_ref[:, cols] + vec_ref[2:3, cols] * y

    return pl.pallas_call(
        body, name="pool_fwd", grid=(S // tm,),
        in_specs=[_rows(tm, D), _prev_halo(tm, D), _resident((8, D)), _resident(pw.shape)],
        out_specs=_rows(tm, D), out_shape=SDS((S, D), F32),
        scratch_shapes=[pltpu.VMEM((HALO + tm, D), F32), pltpu.VMEM((HALO + tm, dg), F32), pltpu.VMEM((HALO + tm, dg), F32)],
        compiler_params=_params("parallel"),
    )(x, x, vec, pw)


def _pool_bwd(x, dxo, vec, pw):
    S, D = x.shape
    ng, dg, _ = pw.shape
    tm = _tile(S, TOKEN_TILE)

    def body(x_ref, xh_ref, dxo_ref, dxn_ref, vec_ref, pw_ref, dx_ref, gpw_ref, sums_ref, ext, exte, buf_a, buf_b):
        _zero_at_first_step(gpw_ref, sums_ref)
        i = pl.program_id(0)
        bufs = (buf_a, buf_b)
        diffs, n, r = _pool_diffs(x_ref, xh_ref, vec_ref, ext, bufs, tm, D)
        gate, scale = vec_ref[2:3, :], vec_ref[3:4, :]
        dxo = dxo_ref[...]
        dyp_next = jnp.where(i < pl.num_programs(0) - 1, dxn_ref[...], 0.0) * gate * scale
        t_next = (i + 1) * tm + lax.broadcasted_iota(jnp.int32, (HALO, 1), 0)
        for g, (d, inv) in enumerate(diffs):
            cols = slice(g * dg, (g + 1) * dg)
            w = pw_ref[g]
            db = d.astype(BF16)
            ypre = _dot(db, w) + vec_ref[4:5, cols]
            dy = gate[:, cols] * dxo[:, cols]
            sums_ref[2:3, cols] += jnp.sum(dxo[:, cols] * ypre * scale[:, cols], axis=0, keepdims=True)
            sums_ref[3:4, cols] += jnp.sum(dy * ypre, axis=0, keepdims=True)
            dyp = dy * scale[:, cols]
            sums_ref[4:5, cols] += jnp.sum(dyp, axis=0, keepdims=True)
            dypb = dyp.astype(BF16)
            gpw_ref[g] += _dot_tn(db, dypb)
            dd = _dot_nt(dypb, w)
            dd_next = _dot_nt(dyp_next[:, cols].astype(BF16), w)
            inv_next = 1.0 / jnp.minimum(t_next + 1, 1 << POOL_LEVELS[g]).astype(F32)
            exte[0:tm, cols] = dd * inv
            exte[tm:, cols] = dd_next * inv_next
            ext[HALO:, cols] = _lead_sum(exte, bufs, cols, POOL_LEVELS[g], tm + HALO) - dd
        dx_ref[...] = _modulate_bwd(ext[HALO:, :], n, r, vec_ref[0:1, :], dxo, sums_ref)

    return pl.pallas_call(
        body, name="pool_bwd", grid=(S // tm,),
        in_specs=[_rows(tm, D), _prev_halo(tm, D), _rows(tm, D), _next_halo(tm, D, S), _resident((8, D)), _resident(pw.shape)],
        out_specs=[_rows(tm, D), pl.BlockSpec(pw.shape, lambda i: (0, 0, 0)), _acc_spec(8, D)],
        out_shape=[SDS((S, D), F32), SDS(pw.shape, F32), SDS((8, D), F32)],
        scratch_shapes=[pltpu.VMEM((HALO + tm, D), F32), pltpu.VMEM((tm + HALO, D), F32),
                        pltpu.VMEM((HALO + tm, dg), F32), pltpu.VMEM((HALO + tm, dg), F32)],
        compiler_params=_params("arbitrary"),
    )(x, x, dxo, dxo, vec, pw)


def _loss_head(y, target):
    S, D = y.shape
    tm = _tile(S, TOKEN_TILE)

    def body(y_ref, t_ref, dy_ref, sums_ref):
        _zero_at_first_step(sums_ref)
        err = y_ref[...] - t_ref[...]
        _add_rowsum(sums_ref, 0, err * err)
        dy_ref[...] = err * (1.0 / D)

    return pl.pallas_call(
        body, name="loss_head", grid=(S // tm,), in_specs=[_rows(tm, D), _rows(tm, D)],
        out_specs=[_rows(tm, D), _acc_spec(8, D)], out_shape=[SDS((S, D), F32), SDS((8, D), F32)],
        compiler_params=_params("arbitrary"),
    )(y, target)


def _cond_pre(c_cols, cond_w):
    def body(c_ref, w_ref, o_ref):
        o_ref[...] = _hdot(c_ref[...], w_ref[...])
    return pl.pallas_call(body, name="cond_pre", out_shape=SDS((c_cols.shape[0], cond_w.shape[1]), F32))(c_cols, cond_w)


def _cond_e(parts, cond_b):
    def body(p_ref, b_ref, pre_ref, e_ref):
        pre = p_ref[0] + p_ref[2] + p_ref[4] + p_ref[6] + b_ref[...]
        pre_ref[...] = pre
        e_ref[...] = pre * jax.nn.sigmoid(pre)
    shape = SDS(parts.shape[1:], F32)
    return pl.pallas_call(body, name="cond_e", out_shape=[shape, shape])(parts, cond_b)


def _mod_cols(e, ada_w, ada_b_cols):
    L, D, nc = ada_w.shape
    B = e.shape[0]

    def body(e_ref, w_ref, b_ref, o_ref):
        o_ref[...] = _hdot(e_ref[...], w_ref[...]) + b_ref[...]

    return pl.pallas_call(
        body, name="mod_cols", grid=(L,),
        in_specs=[pl.BlockSpec((B, D), lambda l: (0, 0)), pl.BlockSpec((None, D, nc), lambda l: (l, 0, 0)),
                  pl.BlockSpec((None, 1, nc), lambda l: (l, 0, 0))],
        out_specs=pl.BlockSpec((None, B, nc), lambda l: (l, 0, 0)), out_shape=SDS((L, B, nc), F32),
        compiler_params=_params("parallel"),
    )(e, ada_w, ada_b_cols)


def _mod_bwd(e, dmod_cols, ada_w):
    L, D, nc = ada_w.shape
    B = e.shape[0]

    def body(e_ref, d_ref, w_ref, gw_ref, de_ref):
        _zero_at_first_step(de_ref)
        gw_ref[...] = _hdot(e_ref[...], d_ref[...], (((0,), (0,)), ((), ())))
        de_ref[...] += _hdot(d_ref[...], w_ref[...], (((1,), (1,)), ((), ())))

    return pl.pallas_call(
        body, name="mod_bwd", grid=(L,),
        in_specs=[pl.BlockSpec((B, D), lambda l: (0, 0)), pl.BlockSpec((None, B, nc), lambda l: (l, 0, 0)),
                  pl.BlockSpec((None, D, nc), lambda l: (l, 0, 0))],
        out_specs=[pl.BlockSpec((None, D, nc), lambda l: (l, 0, 0)), pl.BlockSpec((B, D), lambda l: (0, 0))],
        out_shape=[SDS((L, D, nc), F32), SDS((B, D), F32)], compiler_params=_params("arbitrary"),
    )(e, dmod_cols, ada_w)


def _cond_bwd(de_parts, pre, c_cols):
    def body(p_ref, pre_ref, c_ref, gw_ref, gb_ref):
        pre = pre_ref[...]
        dpre = (p_ref[0] + p_ref[2] + p_ref[4] + p_ref[6]) * _silu_grad(pre, jax.nn.sigmoid(pre))
        gb_ref[...] = jnp.sum(dpre, axis=0, keepdims=True)
        gw_ref[...] = _hdot(c_ref[...], dpre, (((0,), (0,)), ((), ())))
    D = pre.shape[1]
    return pl.pallas_call(body, name="cond_bwd", out_shape=[SDS((c_cols.shape[1], D), F32), SDS((1, D), F32)])(de_parts, pre, c_cols)


def _as_rows(a):
    return a.reshape(-1, a.shape[-1])


def _row_tile(rows, width, n_arrays):
    t = max(8, (ADAM_TILE_BYTES // (4 * width)) // 8 * 8)
    while rows % t:
        t -= 8
        if t <= 0:
            return rows
    return t


def _adamw(w, g, m, v):
    shape = w.shape
    w, g, m, v = (_as_rows(a) for a in (w, g, m, v))
    R, C = w.shape
    tr = _row_tile(R, C, 7)

    def body(w_ref, g_ref, m_ref, v_ref, d_ref, nm_ref, nv_ref):
        g = g_ref[...]
        m = ADAM_B1 * m_ref[...] + (1.0 - ADAM_B1) * g
        v = ADAM_B2 * v_ref[...] + (1.0 - ADAM_B2) * (g * g)
        m_hat = m / (1.0 - ADAM_B1 ** ADAM_STEP)
        v_hat = v / (1.0 - ADAM_B2 ** ADAM_STEP)
        d_ref[...] = -ADAM_LR * (m_hat / (jnp.sqrt(v_hat) + ADAM_EPS) + ADAM_WD * w_ref[...])
        nm_ref[...] = m
        nv_ref[...] = v

    spec = _rows(tr, C)
    outs = pl.pallas_call(
        body, name="adamw", grid=(R // tr,), in_specs=[spec] * 4, out_specs=[spec] * 3,
        out_shape=[SDS((R, C), F32)] * 3, compiler_params=_params("parallel"),
    )(w, g, m, v)
    return tuple(o.reshape(shape) for o in outs)


def _sum_lead(a, out_dtype=F32):
    n = a.shape[0]
    shape = a.shape[1:]
    a = a.reshape(n, -1, a.shape[-1])
    _, R, C = a.shape
    tr = _row_tile(R, C, n + 1)

    def body(a_ref, o_ref):
        acc = a_ref[0].astype(F32)
        for k in range(1, n):
            acc = acc + a_ref[k].astype(F32)
        o_ref[...] = acc.astype(out_dtype)

    out = pl.pallas_call(
        body, name="sum_lead", grid=(R // tr,), in_specs=[pl.BlockSpec((n, tr, C), lambda i: (0, i, 0))],
        out_specs=_rows(tr, C), out_shape=SDS((R, C), out_dtype), compiler_params=_params("parallel"),
    )(a)
    return out.reshape(shape)


def _place():
    return lax.axis_index("x"), lax.axis_index("y"), lax.axis_index("c")


def _allgather8(a):
    def body(a_ref, o_ref, send, recv, local):
        mx, my, mc = _place()
        me = 4 * mx + 2 * my + mc
        mine = pltpu.make_async_copy(a_ref, o_ref.at[me], local)
        mine.start()
        copies = []
        for k in range(1, N_DEV):
            peer = (1 - mx if k & 4 else mx, 1 - my if k & 2 else my, 1 - mc if k & 1 else mc)
            cp = pltpu.make_async_remote_copy(a_ref, o_ref.at[me], send.at[k - 1], recv.at[k - 1], device_id=peer, device_id_type=MESH)
            cp.start()
            copies.append(cp)
        for cp in copies:
            cp.wait()
        mine.wait()

    return pl.pallas_call(
        body, name="allgather8", in_specs=[ANY], out_specs=ANY, out_shape=SDS((N_DEV,) + a.shape, a.dtype),
        scratch_shapes=[pltpu.SemaphoreType.DMA((N_DEV - 1,)), pltpu.SemaphoreType.DMA((N_DEV - 1,)), pltpu.SemaphoreType.DMA],
    )(a)


def _grad_scatter(arrs):
    n = len(arrs)
    out_shape = [SDS((N_CHIPS, a.shape[0]) + a.shape[2:], a.dtype) for a in arrs]

    def body(*refs):
        ins, outs, (send, recv, local) = refs[:n], refs[n:2 * n], refs[2 * n:]
        mx, my, mc = _place()
        chip = 2 * mx + my
        copies = []
        for a in range(n):
            cp = pltpu.make_async_copy(ins[a].at[:, chip], outs[a].at[chip], local.at[a])
            cp.start()
            copies.append(cp)
        for k in range(1, N_CHIPS):
            px, py = (1 - mx if k & 2 else mx), (1 - my if k & 1 else my)
            for a in range(n):
                s = (k - 1) * n + a
                cp = pltpu.make_async_remote_copy(ins[a].at[:, 2 * px + py], outs[a].at[chip], send.at[s], recv.at[s],
                                                  device_id=(px, py, mc), device_id_type=MESH)
                cp.start()
                copies.append(cp)
        for cp in copies:
            cp.wait()

    return pl.pallas_call(
        body, name="grad_scatter", in_specs=[ANY] * n, out_specs=[ANY] * n, out_shape=out_shape,
        scratch_shapes=[pltpu.SemaphoreType.DMA((3 * n,)), pltpu.SemaphoreType.DMA((3 * n,)), pltpu.SemaphoreType.DMA((n,))],
    )(*arrs)


def _weight_gather(arrs):
    n = len(arrs)

    def body(*refs):
        ins, outs, (ici_send, ici_recv, d2d_send, d2d_recv, local) = refs[:n], refs[n:2 * n], refs[2 * n:]
        mx, my, mc = _place()
        chip = 2 * mx + my
        peers = [(1 - mx if k & 2 else mx, 1 - my if k & 1 else my) for k in range(1, N_CHIPS)]
        fetched = []
        for k, (px, py) in enumerate(peers):
            for a in range(n):
                s = k * n + a
                cp = pltpu.make_async_remote_copy(ins[a].at[:, mc], outs[a].at[:, chip, mc], ici_send.at[s], ici_recv.at[s],
                                                  device_id=(px, py, mc), device_id_type=MESH)
                cp.start()
                fetched.append(cp)
        own = []
        for a in range(n):
            cp = pltpu.make_async_copy(ins[a], outs[a].at[:, chip], local.at[a])
            cp.start()
            own.append(cp)
        passed = []
        for k, (px, py) in enumerate(peers):
            for a in range(n):
                s = k * n + a
                fetched[s].wait_recv()
                half = outs[a].at[:, 2 * px + py, mc]
                cp = pltpu.make_async_remote_copy(half, half, d2d_send.at[s], d2d_recv.at[s],
                                                  device_id=(mx, my, 1 - mc), device_id_type=MESH)
                cp.start()
                passed.append(cp)
        for cp in fetched:
            cp.wait_send()
        for cp in passed + own:
            cp.wait()

    sems = pltpu.SemaphoreType.DMA((3 * n,))
    return pl.pallas_call(
        body, name="weight_gather", in_specs=[ANY] * n, out_specs=[ANY] * n,
        out_shape=[SDS((a.shape[0], N_CHIPS) + a.shape[1:], a.dtype) for a in arrs],
        scratch_shapes=[sems, sems, sems, sems, pltpu.SemaphoreType.DMA((n,))],
    )(*arrs)


def _sibling_send(arrs, halves, name):
    n = len(arrs)
    out_shape = [SDS(a.shape[:2] + a.shape[3:] if halves else a.shape, a.dtype) for a in arrs]

    def body(*refs):
        ins, outs, (send, recv) = refs[:n], refs[n:2 * n], refs[2 * n:]
        mx, my, mc = _place()
        copies = []
        for a in range(n):
            cp = pltpu.make_async_remote_copy(ins[a].at[:, :, 1 - mc] if halves else ins[a], outs[a], send.at[a], recv.at[a],
                                              device_id=(mx, my, 1 - mc), device_id_type=MESH)
            cp.start()
            copies.append(cp)
        for cp in copies:
            cp.wait()

    return pl.pallas_call(
        body, name=name, in_specs=[ANY] * n, out_specs=[ANY] * n, out_shape=out_shape,
        scratch_shapes=[pltpu.SemaphoreType.DMA((n,)), pltpu.SemaphoreType.DMA((n,))],
    )(*arrs)


def _add_cast(a, b, dtype):
    shape = a.shape
    a, b = _as_rows(a), _as_rows(b)
    R, C = a.shape
    tr = _row_tile(R, C, 3)

    def body(a_ref, b_ref, o_ref):
        o_ref[...] = (a_ref[...] + b_ref[...]).astype(dtype)

    out = pl.pallas_call(body, name="pair_sum", grid=(R // tr,), in_specs=[_rows(tr, C)] * 2, out_specs=_rows(tr, C),
                         out_shape=SDS((R, C), dtype), compiler_params=_params("parallel"))(a, b)
    return out.reshape(shape)


def _pack(arrs):
    flat = jnp.concatenate([a.reshape(-1).astype(F32) for a in arrs])
    pad = (-flat.shape[0]) % (16 * LANES)
    return jnp.pad(flat, (0, pad)).reshape(-1, LANES)


def _unpack(buf, shapes):
    flat = buf.reshape(buf.shape[:-2] + (-1,))
    out, off = [], 0
    for s in shapes:
        size = 1
        for d in s:
            size *= d
        out.append(flat[..., off:off + size].reshape(flat.shape[:-1] + tuple(s)))
        off += size
    return out


def _unshard(stacked, axis):
    moved = jnp.moveaxis(stacked, 0, axis)
    return moved.reshape(moved.shape[:axis] + (N_CHIPS * moved.shape[axis + 1],) + moved.shape[axis + 2:])


def _my_shard(full, axis, chip):
    size = full.shape[axis] // N_CHIPS
    return lax.dynamic_slice_in_dim(full, chip * size, size, axis)


def kernel(x, c, cond_w, cond_b, ada_w, ada_b, norm_g, ffn_w1, ffn_w3, ffn_w2, a_w_in, a_b_in, a_dw, a_dw_b, a_ln_g, a_ln_b, a_w_out, a_b_out, b_w_qkv, b_q_g, b_k_g, b_w_o, p_w, p_b, p_scale, loss_target, m_cond_w, m_cond_b, m_ada_w, m_ada_b, m_norm_g, m_ffn_w1, m_ffn_w3, m_ffn_w2, m_a_w_in, m_a_b_in, m_a_dw, m_a_dw_b, m_a_ln_g, m_a_ln_b, m_a_w_out, m_a_b_out, m_b_w_qkv, m_b_q_g, m_b_k_g, m_b_w_o, m_p_w, m_p_b, m_p_scale, v_cond_w, v_cond_b, v_ada_w, v_ada_b, v_norm_g, v_ffn_w1, v_ffn_w3, v_ffn_w2, v_a_w_in, v_a_b_in, v_a_dw, v_a_dw_b, v_a_ln_g, v_a_ln_b, v_a_w_out, v_a_b_out, v_b_w_qkv, v_b_q_g, v_b_k_g, v_b_w_o, v_p_w, v_p_b, v_p_scale):
    w_in = dict(cond_w=cond_w, cond_b=cond_b, ada_w=ada_w, ada_b=ada_b, norm_g=norm_g, ffn_w1=ffn_w1, ffn_w3=ffn_w3, ffn_w2=ffn_w2,
                a_w_in=a_w_in, a_b_in=a_b_in, a_dw=a_dw, a_dw_b=a_dw_b, a_ln_g=a_ln_g, a_ln_b=a_ln_b, a_w_out=a_w_out, a_b_out=a_b_out,
                b_w_qkv=b_w_qkv, b_q_g=b_q_g, b_k_g=b_k_g, b_w_o=b_w_o, p_w=p_w, p_b=p_b, p_scale=p_scale)
    m_in = dict(zip(WEIGHTS, (m_cond_w, m_cond_b, m_ada_w, m_ada_b, m_norm_g, m_ffn_w1, m_ffn_w3, m_ffn_w2, m_a_w_in, m_a_b_in, m_a_dw,
                              m_a_dw_b, m_a_ln_g, m_a_ln_b, m_a_w_out, m_a_b_out, m_b_w_qkv, m_b_q_g, m_b_k_g, m_b_w_o, m_p_w, m_p_b, m_p_scale)))
    v_in = dict(zip(WEIGHTS, (v_cond_w, v_cond_b, v_ada_w, v_ada_b, v_norm_g, v_ffn_w1, v_ffn_w3, v_ffn_w2, v_a_w_in, v_a_b_in, v_a_dw,
                              v_a_dw_b, v_a_ln_g, v_a_ln_b, v_a_w_out, v_a_b_out, v_b_w_qkv, v_b_q_g, v_b_k_g, v_b_w_o, v_p_w, v_p_b, v_p_scale)))
    x = x[0]
    target = loss_target[0]
    S, D = x.shape
    L = ada_w.shape[0]
    assert b_q_g.shape[-1] == HEAD_DIM and D % LANES == 0 and S % HALO == 0
    mx, my, mc = _place()
    chip = 2 * mx + my
    me = 2 * chip + mc

    big = [ffn_w1.astype(BF16), ffn_w3.astype(BF16), ffn_w2.astype(BF16), a_w_in.astype(BF16), a_w_out.astype(BF16),
           b_w_qkv.astype(BF16), b_w_o.astype(BF16)]
    sharded = [k for k, ax in SMALL.items() if ax is not None]
    shards = big + [_pack([w_in[k] for k in sharded])]
    by_half = [a.reshape((-1, 2, a.shape[-2] // 2, a.shape[-1])) for a in shards]
    gathered = [g.reshape(a.shape[:-2] + (N_CHIPS,) + a.shape[-2:]) for g, a in zip(_weight_gather(by_half), shards)]
    g_w1, g_w3, g_w2, g_ain, g_aout, g_qkv, g_wo, g_small = gathered
    full = {k: _unshard(a, SMALL[k]) for k, a in zip(sharded, _unpack(g_small, [w_in[k].shape for k in sharded]))}
    n_conv, n_pool = a_w_in.shape[0], p_w.shape[0]
    conv_dw = [jnp.pad(full['a_dw'][ia], ((0, 1), (0, 0))) for ia in range(n_conv)]
    pool_w = [full['p_w'][ic].astype(BF16) for ic in range(n_pool)]
    qk_scale = LOG2E * HEAD_DIM ** -0.5
    gq = jnp.tile(b_q_g[0], D // HEAD_DIM).reshape(1, D) * qk_scale
    gk = jnp.tile(b_k_g[0], D // HEAD_DIM).reshape(1, D)

    c_all = _allgather8(c)[:, 0, :]
    c_cols = _my_shard(c_all, 1, chip)
    pre, e = _cond_e(_allgather8(_cond_pre(c_cols, cond_w)), cond_b.reshape(1, D))
    nc = ada_w.shape[2]
    mod_c = _mod_cols(e, ada_w, _my_shard(ada_b, 1, chip).reshape(L, 1, nc))
    mod_all = _allgather8(mod_c.reshape(L * N_DEV, nc)).reshape(N_CHIPS, 2, L, N_DEV, nc)[:, 0]
    mod = jnp.moveaxis(lax.dynamic_index_in_dim(mod_all, me, axis=2, keepdims=False), 0, 1).reshape(L, 3, 3, D)
    shift, scale, gate = mod[:, :, 0], mod[:, :, 1], 1.0 + mod[:, :, 2]
    gains = full['norm_g']

    def mod_vec(i, k, gate_factor=1.0):
        return _vec(gains[i, k] * (1.0 + scale[i, k]), shift[i, k], gate_factor * gate[i, k])

    saved = []
    ia = ib = ic = 0
    for i in range(L):
        for k, half in ((0, 0), (1, None), (2, 1)):
            if half is not None:
                vec = mod_vec(i, k, 0.5)
                xo, a1, a3, y = _ffn_fwd(x, vec, g_w1, g_w3, g_w2, (i, half))
                saved.append(('ffn', i, k, half, x, vec, (a1, a3, y)))
            elif i % 3 == 0:
                vec = mod_vec(i, k)
                w_a = g_ain[ia]
                w_o = g_aout[ia].reshape(D, D)
                p, u = _in_fwd(x, vec, w_a, full['a_b_in'][ia].reshape(1, 2 * D), True, "conv_in_fwd")
                cvec = _vec(gate[i, k], full['a_dw_b'][ia], full['a_ln_g'][ia], full['a_ln_b'][ia], full['a_b_out'][ia])
                xo, v, y = _conv_mid_fwd(u, x, cvec, conv_dw[ia], w_o)
                saved.append(('conv', i, k, ia, x, vec, (p, u, v, y, cvec, w_a, w_o)))
                ia += 1
            elif i % 3 == 1:
                vec = mod_vec(i, k)
                w_q = g_qkv[ib]
                w_o = g_wo[ib].reshape(D, D)
                raw, = _in_fwd(x, vec, w_q, jnp.zeros((1, 3 * D), F32), False, "attn_in_fwd")
                qkv = _qknorm_fwd(raw, gq, gk)
                o, tot, start = _attn_fwd(qkv)
                xo, y = _out_fwd(x, o, vec, w_o)
                saved.append(('attn', i, k, ib, x, vec, (raw, qkv, o, (tot, start), y, w_q, w_o)))
                ib += 1
            else:
                vec = _vec(gains[i, k] * (1.0 + scale[i, k]), shift[i, k], gate[i, k], full['p_scale'][ic], full['p_b'][ic].reshape(D))
                xo = _pool_fwd(x, vec, pool_w[ic])
                saved.append(('pool', i, k, ic, x, vec, ()))
                ic += 1
            x = xo

    dx, sq = _loss_head(x, target)
    loss = lax.psum(0.5 / D * jnp.sum(sq[0]), ("x", "y", "c"))

    zeros_like_full = lambda k: jnp.zeros(full[k].shape, F32)
    g_full = {k: zeros_like_full(k) for k in sharded}
    g_full['b_q_g'] = jnp.zeros_like(b_q_g)
    g_full['b_k_g'] = jnp.zeros_like(b_k_g)
    dmod = jnp.zeros((L, 3, 3, D), F32)
    fc = ffn_w1.shape[-1]
    n_attn = b_w_qkv.shape[0]
    big_grads = {'ffn_w1': lax.empty((2 * L, N_CHIPS, fc, D), F32), 'ffn_w3': lax.empty((2 * L, N_CHIPS, fc, D), F32),
                 'ffn_w2': lax.empty((2 * L, N_CHIPS, fc, D), F32), 'a_w_in': lax.empty((n_conv, N_CHIPS, D, 2 * D // N_CHIPS), F32),
                 'a_w_out': lax.empty((n_conv, 1, D, D), F32), 'b_w_qkv': lax.empty((n_attn, N_CHIPS, D, 3 * D // N_CHIPS), F32),
                 'b_w_o': lax.empty((n_attn, 1, D, D), F32)}
    transposed = ('ffn_w1', 'ffn_w3')

    def wgrad(name, slot, a, b, a_mode, b_mode, nch):
        big_grads[name] = _wgrad(a, b, a_mode, b_mode, nch, name + "_grad", big_grads[name], slot)

    def put(name, idx, val):
        g_full[name] = g_full[name].at[idx].set(val.reshape(g_full[name][idx].shape))

    for kind, i, k, idx, xin, vec, res in reversed(saved):
        if kind == 'ffn':
            a1, a3, y = res
            dx, h, dy, u, da1, da3, sums = _ffn_bwd(xin, dx, a1, a3, y, vec, g_w1, g_w3, g_w2, (i, idx))
            wgrad('ffn_w1', 2 * i + idx, da1, h, 'lead', 'full', N_CHIPS)
            wgrad('ffn_w3', 2 * i + idx, da3, h, 'lead', 'full', N_CHIPS)
            wgrad('ffn_w2', 2 * i + idx, u, dy, 'lead', 'full', N_CHIPS)
            dgate = 0.5 * sums[2]
        elif kind == 'conv':
            p, u, v, y, cvec, w_a, w_o = res
            dv, q, dout, csums = _conv_mid_bwd(dx, y, v, cvec, w_o)
            wgrad('a_w_out', idx, q, dout, 'full', 'full', 1)
            du, gdw = _conv_transpose(dv, u, conv_dw[idx])
            dp, psums = _glu_bwd(du, p)
            dx, h, sums = _in_bwd(xin, dx, dp, vec, w_a, "conv_in_bwd")
            wgrad('a_w_in', idx, h, dp, 'full', 'col', N_CHIPS)
            dgate = csums[0]
            put('a_b_out', idx, csums[1])
            put('a_ln_g', idx, csums[2])
            put('a_ln_b', idx, csums[3])
            put('a_dw_b', idx, csums[4])
            put('a_dw', idx, gdw[:-1])
            put('a_b_in', idx, psums[0])
        elif kind == 'attn':
            raw, qkv, o, tot, y, w_q, w_o = res
            do, dout, osums = _out_bwd(dx, y, vec, w_o)
            wgrad('b_w_o', idx, o, dout, 'full', 'full', 1)
            dq, dk, dvv = _attn_bwd(qkv, do, *tot)
            draw, qsums = _qknorm_bwd(dq, dk, dvv, raw, gq, gk)
            dx, h, sums = _in_bwd(xin, dx, draw, vec, w_q, "attn_in_bwd")
            wgrad('b_w_qkv', idx, h, draw, 'full', 'col', N_CHIPS)
            dgate = osums[2]
            put('b_q_g', idx, qk_scale * jnp.sum(qsums[0].reshape(-1, HEAD_DIM), axis=0))
            put('b_k_g', idx, jnp.sum(qsums[1].reshape(-1, HEAD_DIM), axis=0))
        else:
            dx, gpw, sums = _pool_bwd(xin, dx, vec, pool_w[idx])
            dgate = sums[2]
            put('p_w', idx, gpw)
            put('p_scale', idx, sums[3])
            put('p_b', idx, sums[4])
        put('norm_g', (i, k), sums[0] * (1.0 + scale[i, k]))
        dmod = dmod.at[i, k].set(jnp.stack([sums[1], sums[0] * gains[i, k], dgate]))
    grad_x = dx[None]

    dmod_all = _allgather8(dmod.reshape(L, 9 * D))
    g_ada_b = _sum_lead(dmod_all)
    dmod_cols = jnp.moveaxis(_my_shard(dmod_all, 2, chip), 0, 1)
    g_ada_w, de_part = _mod_bwd(e, dmod_cols, ada_w)
    g_cond_w, g_cond_b = _cond_bwd(_allgather8(de_part), pre, c_cols)

    small_names = list(SMALL)
    g_full['cond_b'] = g_cond_b.reshape(D)
    g_full['ada_b'] = g_ada_b
    reduced = [k for k in small_names if k not in ('cond_b', 'ada_b')]
    red = _unpack(_sum_lead(_allgather8(_pack([g_full[k] for k in reduced]))), [g_full[k].shape for k in reduced])
    for k, a in zip(reduced, red):
        g_full[k] = a
    grads = {k: (g_full[k] if SMALL[k] is None else _my_shard(g_full[k], SMALL[k], chip)) for k in small_names}
    grads['cond_w'] = g_cond_w
    grads['ada_w'] = g_ada_w

    big_names = list(big_grads)
    by_half = []
    for name in big_names:
        g = big_grads[name]
        rows = g.shape[1] * g.shape[2] // N_CHIPS
        by_half.append(g.reshape(g.shape[0], N_CHIPS, 2, rows // 2, g.shape[3]))
    got = _sibling_send(by_half, True, "pair_send")
    pair = [_add_cast(lax.dynamic_index_in_dim(a, mc, axis=2, keepdims=False), b, BF16) for a, b in zip(by_half, got)]
    mine = [_sum_lead(a) for a in _grad_scatter(pair)]
    theirs = _sibling_send(mine, False, "pair_return")
    flip = lambda a: jnp.swapaxes(a, -1, -2)
    for name, a, b in zip(big_names, mine, theirs):
        both = jnp.where(mc == 0, jnp.stack([a, b], axis=1), jnp.stack([b, a], axis=1))
        shape = w_in[name].shape
        grads[name] = both.reshape(shape[:-2] + (shape[-1], shape[-2])) if name in transposed else both.reshape(shape)

    delta, new_m, new_v = {}, {}, {}
    packed = [_pack([d[k] for k in small_names]) for d in (w_in, grads, m_in, v_in)]
    shapes = [w_in[k].shape for k in small_names]
    for out, buf in zip((delta, new_m, new_v), _adamw(*packed)):
        out.update(zip(small_names, _unpack(buf, shapes)))
    for k in WEIGHTS:
        if k in transposed:
            delta[k], new_m[k], new_v[k] = (flip(a) for a in _adamw(flip(w_in[k]), grads[k], flip(m_in[k]), flip(v_in[k])))
            grads[k] = flip(grads[k])
        elif k not in SMALL:
            delta[k], new_m[k], new_v[k] = _adamw(w_in[k], grads[k], m_in[k], v_in[k])
    return (loss, grad_x, *[grads[k] for k in WEIGHTS], *[delta[k] for k in WEIGHTS], *[new_m[k] for k in WEIGHTS],
            *[new_v[k] for k in WEIGHTS])
```
